```python
import math
import jax, jax.numpy as jnp
from jax import lax
import numpy as np


D_MODEL = 4096
BATCH = 4
SEQ = 2048
DEPTH = 2

HEAD_DIM = 128
N_MIXERS = 4
MIX_WIDTH = D_MODEL
GROUP_WIDTH = MIX_WIDTH // N_MIXERS
N_HEADS = GROUP_WIDTH // HEAD_DIM
MOBA_BLOCK = 256
MOBA_TOPK = 3
MOBA_Q_CHUNK = 64
SB_Q_BLOCK = 128
POOL_WINDOWS = (2, 4, 8, 16)
N_POOL_GROUPS = len(POOL_WINDOWS)
POOL_GROUP_DIM = GROUP_WIDTH // N_POOL_GROUPS
DILATED_PATTERNS = ((128, 1), (512, 4), (2048, 16))
DIL_Q_CHUNK = 64
D_FF = ((8 * D_MODEL + 3 * 256 - 1) // (3 * 256)) * 256
IN_WIDTH = 3 * 3 * GROUP_WIDTH + GROUP_WIDTH
RMS_EPS = 1e-6
NEG_INF = -1e30

kernel_name = "hybrid_moba_stickbreak_pool_dilated_block"


def rms_norm(x, g):
    xf = x.astype(jnp.float32)
    y = xf * lax.rsqrt(jnp.mean(xf * xf, axis=-1, keepdims=True) + RMS_EPS)
    return (y * g.astype(jnp.float32)).astype(x.dtype)


def moba_attention(q, k, v):
    B, H, S, hd = q.shape
    nb = -(-S // MOBA_BLOCK)
    pad = nb * MOBA_BLOCK - S
    kp = jnp.pad(k, ((0, 0), (0, 0), (0, pad), (0, 0))).reshape(B, H, nb, MOBA_BLOCK, hd)
    vp = jnp.pad(v, ((0, 0), (0, 0), (0, pad), (0, 0))).reshape(B, H, nb, MOBA_BLOCK, hd)
    kmean = jnp.mean(kp.astype(jnp.float32), axis=3)
    n_sel = min(MOBA_TOPK, nb - 1)
    scale = hd ** -0.5
    b_idx = jnp.arange(B)[:, None, None, None]
    h_idx = jnp.arange(H)[None, :, None, None]

    def chunk(ci):
        c0 = ci * MOBA_Q_CHUNK
        blk = c0 // MOBA_BLOCK
        qc = lax.dynamic_slice_in_dim(q, c0, MOBA_Q_CHUNK, axis=2).astype(jnp.float32)
        tq = c0 + jnp.arange(MOBA_Q_CHUNK)
        k_own = lax.dynamic_index_in_dim(kp, blk, axis=2, keepdims=False).astype(jnp.float32)
        v_own = lax.dynamic_index_in_dim(vp, blk, axis=2, keepdims=False).astype(jnp.float32)
        kpos = blk * MOBA_BLOCK + jnp.arange(MOBA_BLOCK)
        l_own = jnp.einsum('bhqd,bhld->bhql', qc, k_own) * scale
        l_own = jnp.where(kpos[None, :] <= tq[:, None], l_own, NEG_INF)
        if n_sel == 0:
            p_own = jax.nn.softmax(l_own, axis=-1)
            out = jnp.einsum('bhql,bhld->bhqd', p_own, v_own)
            return out.astype(q.dtype)
        gate = jnp.einsum('bhqd,bhnd->bhqn', qc, kmean)
        gate = jnp.where(jnp.arange(nb) < blk, gate, NEG_INF)
        _, idx = lax.top_k(gate, n_sel)
        sel_ok = idx < blk
        k_sel = kp[b_idx, h_idx, idx].astype(jnp.float32)
        v_sel = vp[b_idx, h_idx, idx].astype(jnp.float32)
        l_sel = jnp.einsum('bhqd,bhqnld->bhqnl', qc, k_sel) * scale
        l_sel = jnp.where(sel_ok[..., None], l_sel, NEG_INF)
        l_sel = l_sel.reshape(B, H, MOBA_Q_CHUNK, n_sel * MOBA_BLOCK)
        p = jax.nn.softmax(jnp.concatenate([l_sel, l_own], axis=-1), axis=-1)
        p_sel = p[..., : n_sel * MOBA_BLOCK].reshape(B, H, MOBA_Q_CHUNK, n_sel, MOBA_BLOCK)
        p_own = p[..., n_sel * MOBA_BLOCK:]
        out = (jnp.einsum('bhqnl,bhqnld->bhqd', p_sel, v_sel)
               + jnp.einsum('bhql,bhld->bhqd', p_own, v_own))
        return out.astype(q.dtype)

    outs = lax.map(chunk, jnp.arange(S // MOBA_Q_CHUNK))
    return outs.transpose(1, 2, 0, 3, 4).reshape(B, H, S, hd)


def stick_breaking_attention(q, k, v):
    B, H, S, hd = q.shape
    kf = k.astype(jnp.float32)
    vf = v.astype(jnp.float32)
    s_pos = jnp.arange(S)
    scale = hd ** -0.5

    def block(bi):
        t0 = bi * SB_Q_BLOCK
        qb = lax.dynamic_slice_in_dim(q, t0, SB_Q_BLOCK, axis=2).astype(jnp.float32)
        tq = t0 + jnp.arange(SB_Q_BLOCK)
        z = jnp.einsum('bhqd,bhsd->bhqs', qb, kf) * scale
        strict = s_pos[None, :] < tq[:, None]
        log_keep = jnp.where(strict, jax.nn.log_sigmoid(-z), 0.0)
        after = lax.cumsum(log_keep, axis=3, reverse=True) - log_keep
        a = jnp.where(strict, jnp.exp(jax.nn.log_sigmoid(z) + after), 0.0)
        return jnp.einsum('bhqs,bhsd->bhqd', a, vf).astype(q.dtype)

    outs = lax.map(block, jnp.arange(S // SB_Q_BLOCK))
    return outs.transpose(1, 2, 0, 3, 4).reshape(B, H, S, hd)


def multiscale_pool(u, w_pool, pool_scale):
    B, S, _ = u.shape
    uf = u.astype(jnp.float32).reshape(B, S, N_POOL_GROUPS, POOL_GROUP_DIM)
    cs = jnp.concatenate([jnp.zeros((B, 1, N_POOL_GROUPS, POOL_GROUP_DIM), jnp.float32),
                          jnp.cumsum(uf, axis=1)], axis=1)
    t = jnp.arange(S)[:, None]
    win = jnp.array(POOL_WINDOWS, dtype=jnp.int32)[None, :]
    lo = jnp.maximum(t + 1 - win, 0)
    g_idx = jnp.arange(N_POOL_GROUPS)[None, :]
    win_sum = cs[:, 1:] - cs[:, lo, g_idx]
    count = (t + 1 - lo).astype(jnp.float32)
    d = win_sum / count[None, :, :, None] - uf
    y = jnp.einsum('bsgc,gcd->bsgd', d, w_pool.astype(jnp.float32))
    y = y.reshape(B, S, GROUP_WIDTH) * pool_scale.astype(jnp.float32)
    return y.astype(u.dtype)


def dilated_attention(q, k, v):
    B, H, S, hd = q.shape
    scale = hd ** -0.5

    def chunk(ci):
        t0 = ci * DIL_Q_CHUNK
        qc = lax.dynamic_slice_in_dim(q, t0, DIL_Q_CHUNK, axis=2).astype(jnp.float32)
        tq = t0 + jnp.arange(DIL_Q_CHUNK)
        lses = []
        outs = []
        for window, dil in DILATED_PATTERNS:
            m = jnp.arange(window // dil + 1)
            pos = tq[:, None] - dil * m[None, :]
            ok = pos >= 0
            posc = jnp.maximum(pos, 0)
            kg = k[:, :, posc].astype(jnp.float32)
            vg = v[:, :, posc].astype(jnp.float32)
            logits = jnp.einsum('bhqd,bhqkd->bhqk', qc, kg) * scale
            logits = jnp.where(ok, logits, NEG_INF)
            lse = jax.nn.logsumexp(logits, axis=-1)
            p = jnp.exp(logits - lse[..., None])
            outs.append(jnp.einsum('bhqk,bhqkd->bhqd', p, vg))
            lses.append(lse)
        alpha = jax.nn.softmax(jnp.stack(lses, axis=0), axis=0)
        out = jnp.einsum('pbhq,pbhqd->bhqd', alpha, jnp.stack(outs, axis=0))
        return out.astype(q.dtype)

    outs = lax.map(chunk, jnp.arange(S // DIL_Q_CHUNK))
    return outs.transpose(1, 2, 0, 3, 4).reshape(B, H, S, hd)


def hybrid_mixer(h, w_in, w_pool, pool_scale, mix_out_norm, w_out):
    B, S, _ = h.shape
    proj = h @ w_in
    parts = jnp.split(proj, [GROUP_WIDTH * i for i in range(1, 10)], axis=-1)
    qa, ka, va, qb, kb, vb, qd, kd, vd, u = parts

    def to_heads(t):
        return t.reshape(B, S, N_HEADS, HEAD_DIM).transpose(0, 2, 1, 3)

    def from_heads(t):
        return t.transpose(0, 2, 1, 3).reshape(B, S, GROUP_WIDTH)

    y_a = from_heads(moba_attention(to_heads(qa), to_heads(ka), to_heads(va)))
    y_b = from_heads(stick_breaking_attention(to_heads(qb), to_heads(kb), to_heads(vb)))
    y_c = multiscale_pool(u, w_pool, pool_scale)
    y_d = from_heads(dilated_attention(to_heads(qd), to_heads(kd), to_heads(vd)))
    y = jnp.stack([y_a, y_b, y_c, y_d], axis=2)
    y = rms_norm(y, mix_out_norm.reshape(N_MIXERS, GROUP_WIDTH)).reshape(B, S, MIX_WIDTH)
    return y @ w_out


def swiglu(h, w_gate, w_up, w_down):
    return (jax.nn.silu(h @ w_gate) * (h @ w_up)) @ w_down


def setup_inputs(seed: int = 0) -> dict:
    key = jax.random.key(seed)
    ks = jax.random.split(key, 13)
    f32 = jnp.float32

    def gain(k, n):
        return 1.0 + 0.05 * jax.random.normal(k, (DEPTH, n), f32)

    x = jax.random.normal(ks[0], (BATCH, SEQ, D_MODEL), f32)
    ln_mix_pre = gain(ks[1], D_MODEL)
    w_in = jax.random.normal(ks[2], (DEPTH, D_MODEL, IN_WIDTH), f32) * D_MODEL ** -0.5
    w_pool = jax.random.normal(ks[3], (DEPTH, N_POOL_GROUPS, POOL_GROUP_DIM, POOL_GROUP_DIM), f32) * POOL_GROUP_DIM ** -0.5
    pool_scale = 1.0 + 0.1 * jax.random.normal(ks[4], (DEPTH, GROUP_WIDTH), f32)
    mix_out_norm = gain(ks[5], MIX_WIDTH)
    w_out = jax.random.normal(ks[6], (DEPTH, MIX_WIDTH, D_MODEL), f32) * MIX_WIDTH ** -0.5
    ln_mix_post = gain(ks[7], D_MODEL)
    ln_ffn_pre = gain(ks[8], D_MODEL)
    w_gate = jax.random.normal(ks[9], (DEPTH, D_MODEL, D_FF), f32) * D_MODEL ** -0.5
    w_up = jax.random.normal(ks[10], (DEPTH, D_MODEL, D_FF), f32) * D_MODEL ** -0.5
    w_down = jax.random.normal(ks[11], (DEPTH, D_FF, D_MODEL), f32) * D_FF ** -0.5
    ln_ffn_post = gain(ks[12], D_MODEL)
    return {"x": x, "ln_mix_pre": ln_mix_pre, "w_in": w_in, "w_pool": w_pool,
            "pool_scale": pool_scale, "mix_out_norm": mix_out_norm, "w_out": w_out,
            "ln_mix_post": ln_mix_post, "ln_ffn_pre": ln_ffn_pre, "w_gate": w_gate,
            "w_up": w_up, "w_down": w_down, "ln_ffn_post": ln_ffn_post}


def reference(x, ln_mix_pre, w_in, w_pool, pool_scale, mix_out_norm, w_out,
              ln_mix_post, ln_ffn_pre, w_gate, w_up, w_down, ln_ffn_post):
    for l in range(DEPTH):
        h = rms_norm(x, ln_mix_pre[l])
        m = hybrid_mixer(h, w_in[l], w_pool[l], pool_scale[l], mix_out_norm[l], w_out[l])
        x = x + rms_norm(m, ln_mix_post[l])
        h = rms_norm(x, ln_ffn_pre[l])
        f = swiglu(h, w_gate[l], w_up[l], w_down[l])
        x = x + rms_norm(f, ln_ffn_post[l])
    return x
```

```python
import functools

import numpy as np
import jax
import jax.numpy as jnp
from jax import lax
from jax.experimental import pallas as pl
from jax.experimental.pallas import tpu as pltpu

HEAD_DIM = 128
N_MIXERS = 4
MOBA_BLOCK = 256
MOBA_TOPK = 3
POOL_WINDOWS = (2, 4, 8, 16)
DILATED_PATTERNS = ((128, 1), (512, 4), (2048, 16))
RMS_EPS = 1e-6
NEG_INF = -1e30

ATTN_TILE = 256
V7X_VMEM_BYTES = 64 * 1024 * 1024
VMEM_LIMIT_CAP = V7X_VMEM_BYTES - 8 * 1024 * 1024

F32 = jnp.float32
BF16 = jnp.bfloat16


def _nbytes(shape, dtype):
    return int(np.prod(shape)) * jnp.dtype(dtype).itemsize


def _params(block_bytes, scratch_bytes=0, temp_bytes=0, n_axes=2):
    limit = 2 * block_bytes + scratch_bytes + temp_bytes
    limit = min(max(limit, 16 * 1024 * 1024), VMEM_LIMIT_CAP)
    return pltpu.CompilerParams(
        dimension_semantics=("arbitrary",) * n_axes, vmem_limit_bytes=int(limit))


def _dot(a, b):
    return jnp.dot(a, b, preferred_element_type=F32)


def _dot_nt(a, b):
    return lax.dot_general(a, b, (((1,), (1,)), ((), ())), preferred_element_type=F32)


def _rms_scale(x):
    return lax.rsqrt(jnp.mean(x * x, axis=-1, keepdims=True) + RMS_EPS)


def _rmsnorm_cast_kernel(x_ref, g_ref, o_ref):
    x = x_ref[...]
    o_ref[...] = (x * _rms_scale(x) * g_ref[...]).astype(o_ref.dtype)


def _rmsnorm_cast(x, g, *, tm=512):
    m, d = x.shape
    blk = _nbytes((tm, d), F32) + _nbytes((tm, d), BF16)
    return pl.pallas_call(
        _rmsnorm_cast_kernel, grid=(m // tm,),
        in_specs=[pl.BlockSpec((tm, d), lambda i: (i, 0)),
                  pl.BlockSpec((1, d), lambda i: (0, 0))],
        out_specs=pl.BlockSpec((tm, d), lambda i: (i, 0)),
        out_shape=jax.ShapeDtypeStruct((m, d), BF16),
        compiler_params=_params(blk, temp_bytes=_nbytes((tm, d), F32), n_axes=1),
        name="rmsnorm_cast",
    )(x, g.reshape(1, d))


def _residual_norm_kernel(x_ref, m_ref, g_post_ref, g_next_ref, xo_ref, ho_ref):
    m = m_ref[...]
    x = x_ref[...] + m * _rms_scale(m) * g_post_ref[...]
    xo_ref[...] = x
    ho_ref[...] = (x * _rms_scale(x) * g_next_ref[...]).astype(ho_ref.dtype)


def _residual_norm(x, m, g_post, g_next, *, tm=256):
    rows, d = x.shape
    blk = 3 * _nbytes((tm, d), F32) + _nbytes((tm, d), BF16)
    row_spec = pl.BlockSpec((tm, d), lambda i: (i, 0))
    gain_spec = pl.BlockSpec((1, d), lambda i: (0, 0))
    return pl.pallas_call(
        _residual_norm_kernel, grid=(rows // tm,),
        in_specs=[row_spec, row_spec, gain_spec, gain_spec],
        out_specs=[row_spec, row_spec],
        out_shape=[jax.ShapeDtypeStruct((rows, d), F32), jax.ShapeDtypeStruct((rows, d), BF16)],
        compiler_params=_params(blk, temp_bytes=2 * _nbytes((tm, d), F32), n_axes=1),
        name="residual_norm",
    )(x, m, g_post.reshape(1, d), g_next.reshape(1, d))


def _mixer_norm_kernel(ya_ref, yb_ref, yc_ref, yd_ref, g_ref, o_ref):
    gw = ya_ref.shape[1]
    for i, y_ref in enumerate((ya_ref, yb_ref, yc_ref, yd_ref)):
        y = y_ref[...]
        cols = slice(i * gw, (i + 1) * gw)
        o_ref[:, cols] = (y * _rms_scale(y) * g_ref[:, cols]).astype(o_ref.dtype)


def _mixer_norm(ys, g, *, tm=512):
    rows, gw = ys[0].shape
    d = gw * len(ys)
    blk = len(ys) * _nbytes((tm, gw), F32) + _nbytes((tm, d), BF16)
    y_spec = pl.BlockSpec((tm, gw), lambda i: (i, 0))
    return pl.pallas_call(
        _mixer_norm_kernel, grid=(rows // tm,),
        in_specs=[y_spec] * len(ys) + [pl.BlockSpec((1, d), lambda i: (0, 0))],
        out_specs=pl.BlockSpec((tm, d), lambda i: (i, 0)),
        out_shape=jax.ShapeDtypeStruct((rows, d), BF16),
        compiler_params=_params(blk, temp_bytes=_nbytes((tm, d), F32), n_axes=1),
        name="mixer_norm",
    )(*ys, g.reshape(1, d))


def _matmul_kernel(a_ref, w_ref, o_ref):
    o_ref[...] = _dot(a_ref[...], w_ref[...]).astype(o_ref.dtype)


def _matmul(a, w, *, out_dtype, tm, tn, col_block_offset=0, n_out=None, name="matmul"):
    m, k = a.shape
    n_out = w.shape[1] if n_out is None else n_out
    blk = _nbytes((tm, k), a.dtype) + _nbytes((k, tn), w.dtype) + _nbytes((tm, tn), out_dtype)
    return pl.pallas_call(
        _matmul_kernel, grid=(m // tm, n_out // tn),
        in_specs=[pl.BlockSpec((tm, k), lambda i, j: (i, 0)),
                  pl.BlockSpec((k, tn), lambda i, j: (0, j + col_block_offset))],
        out_specs=pl.BlockSpec((tm, tn), lambda i, j: (i, j)),
        out_shape=jax.ShapeDtypeStruct((m, n_out), out_dtype),
        compiler_params=_params(blk, temp_bytes=2 * _nbytes((tm, tn), F32)),
        name=name,
    )(a, w)


def _swiglu_up_kernel(h_ref, wg_ref, wu_ref, o_ref):
    h = h_ref[...]
    g = _dot(h, wg_ref[...])
    u = _dot(h, wu_ref[...])
    o_ref[...] = (g / (1.0 + jnp.exp(-g)) * u).astype(o_ref.dtype)


def _swiglu_up(h, wg, wu, *, tm=1024, tn=256):
    m, k = h.shape
    n = wg.shape[1]
    blk = _nbytes((tm, k), BF16) + 2 * _nbytes((k, tn), BF16) + _nbytes((tm, tn), BF16)
    w_spec = pl.BlockSpec((k, tn), lambda i, j: (0, j))
    return pl.pallas_call(
        _swiglu_up_kernel, grid=(m // tm, n // tn),
        in_specs=[pl.BlockSpec((tm, k), lambda i, j: (i, 0)), w_spec, w_spec],
        out_specs=pl.BlockSpec((tm, tn), lambda i, j: (i, j)),
        out_shape=jax.ShapeDtypeStruct((m, n), BF16),
        compiler_params=_params(blk, temp_bytes=3 * _nbytes((tm, tn), F32)),
        name="swiglu_up",
    )(h, wg, wu)


def _head_specs(seq, n_heads, mixer_slot):
    tiles = seq // ATTN_TILE
    base = 3 * mixer_slot * n_heads
    q_spec = pl.BlockSpec((ATTN_TILE, HEAD_DIM), lambda b, h, t: (b * tiles + t, base + h))
    k_spec = pl.BlockSpec((seq, HEAD_DIM), lambda b, h, t: (b, base + n_heads + h))
    v_spec = pl.BlockSpec((seq, HEAD_DIM), lambda b, h, t: (b, base + 2 * n_heads + h))
    o_spec = pl.BlockSpec((ATTN_TILE, HEAD_DIM), lambda b, h, t: (b * tiles + t, h))
    return q_spec, k_spec, v_spec, o_spec


def _attn_call(kernel, proj, *, batch, seq, n_heads, mixer_slot, scratch, extra_inputs=(),
               extra_specs=(), name):
    q_spec, k_spec, v_spec, o_spec = _head_specs(seq, n_heads, mixer_slot)
    blk = (2 * _nbytes((ATTN_TILE, HEAD_DIM), F32) + 2 * _nbytes((seq, HEAD_DIM), BF16)
           + sum(_nbytes(x.shape, x.dtype) for x in extra_inputs))
    return pl.pallas_call(
        kernel, grid=(batch, n_heads, seq // ATTN_TILE),
        in_specs=[q_spec, k_spec, v_spec, *extra_specs],
        out_specs=o_spec,
        out_shape=jax.ShapeDtypeStruct((batch * seq, n_heads * HEAD_DIM), F32),
        scratch_shapes=scratch,
        compiler_params=_params(blk, temp_bytes=16 * _nbytes((ATTN_TILE, ATTN_TILE), F32),
                                n_axes=3),
        name=name,
    )(proj, proj, proj, *extra_inputs)


def _softmax_scratch():
    return [pltpu.VMEM((ATTN_TILE, 1), F32), pltpu.VMEM((ATTN_TILE, 1), F32),
            pltpu.VMEM((ATTN_TILE, HEAD_DIM), F32)]


def _online_softmax_step(s, weight, v, m_ref, l_ref, acc_ref, first):
    m_blk = jnp.max(s, axis=1, keepdims=True)
    if first:
        m_new = m_blk
    else:
        m_old = m_ref[...]
        m_new = jnp.maximum(m_old, m_blk)
        alpha = jnp.exp(m_old - m_new)
    p = jnp.exp(s - m_new)
    if weight is not None:
        p = p * weight
    l_blk = jnp.sum(p, axis=1, keepdims=True)
    pv = _dot(p.astype(BF16), v)
    if first:
        l_ref[...] = l_blk
        acc_ref[...] = pv
    else:
        l_ref[...] = alpha * l_ref[...] + l_blk
        acc_ref[...] = alpha * acc_ref[...] + pv
    m_ref[...] = m_new


def _moba_kernel(q_ref, k_ref, v_ref, o_ref, kmean_ref, m_ref, l_ref, acc_ref):
    t = ATTN_TILE
    n_blocks = k_ref.shape[0] // t
    qi = pl.program_id(2)
    scale = HEAD_DIM ** -0.5

    @pl.when(qi == 0)
    def _():
        kf = k_ref[...].astype(F32).reshape(n_blocks, t, HEAD_DIM)
        kmean_ref[...] = jnp.sum(kf, axis=1) * (1.0 / t)

    q = q_ref[...]
    kmean = kmean_ref[...]
    kmean_hi = kmean.astype(BF16)
    kmean_lo = (kmean - kmean_hi.astype(F32)).astype(BF16)
    gate = _dot_nt(q, kmean_hi) + _dot_nt(q, kmean_lo)

    blk_id = lax.broadcasted_iota(jnp.int32, gate.shape, 1)
    rank = jnp.zeros(gate.shape, jnp.int32)
    for i in range(n_blocks):
        gi = gate[:, i:i + 1]
        beats = jnp.where(gi > gate, 1, jnp.where((gi == gate) & (i < blk_id), 1, 0))
        rank = rank + beats * jnp.where(i < qi, 1, 0)
    selected = jnp.where((blk_id < qi) & (rank < MOBA_TOPK), 1.0, 0.0)

    row = lax.broadcasted_iota(jnp.int32, (t, t), 0)
    col = lax.broadcasted_iota(jnp.int32, (t, t), 1)
    own = pl.ds(pl.multiple_of(qi * t, t), t)
    s = jnp.where(col <= row, _dot_nt(q, k_ref[own, :]) * scale, NEG_INF)
    _online_softmax_step(s, None, v_ref[own, :], m_ref, l_ref, acc_ref, first=True)

    for j in range(n_blocks - 1):
        @pl.when(j < qi)
        def _(j=j):
            rows = slice(j * t, (j + 1) * t)
            sj = jnp.where(selected[:, j:j + 1] > 0.0, _dot_nt(q, k_ref[rows, :]) * scale, NEG_INF)
            _online_softmax_step(sj, None, v_ref[rows, :], m_ref, l_ref, acc_ref, first=False)

    o_ref[...] = acc_ref[...] / l_ref[...]


def _stickbreak_kernel(q_ref, k_ref, v_ref, o_ref, carry_ref, acc_ref):
    t = ATTN_TILE
    n_blocks = k_ref.shape[0] // t
    qi = pl.program_id(2)
    scale = HEAD_DIM ** -0.5
    q = q_ref[...]
    row = lax.broadcasted_iota(jnp.int32, (t, t), 0)
    col = lax.broadcasted_iota(jnp.int32, (t, t), 1)
    later = jnp.where(row > col, 1.0, 0.0).astype(BF16)
    strict = col < row
    carry_ref[...] = jnp.zeros_like(carry_ref)
    acc_ref[...] = jnp.zeros_like(acc_ref)

    for back in range(n_blocks):
        @pl.when(back <= qi)
        def _(back=back):
            rows = pl.ds(pl.multiple_of((qi - back) * t, t), t)
            z = _dot_nt(q, k_ref[rows, :]) * scale
            softplus = jnp.maximum(z, 0.0) + jnp.log1p(jnp.exp(-jnp.abs(z)))
            log_keep = -softplus
            if back == 0:
                log_keep = jnp.where(strict, log_keep, 0.0)
            keep_hi = log_keep.astype(BF16)
            keep_lo = (log_keep - keep_hi.astype(F32)).astype(BF16)
            after = _dot(keep_hi, later) + _dot(keep_lo, later) + carry_ref[...]
            a = jnp.exp(z - softplus + after)
            if back == 0:
                a = jnp.where(strict, a, 0.0)
            acc_ref[...] += _dot(a.astype(BF16), v_ref[rows, :])
            carry_ref[...] += jnp.sum(log_keep, axis=1, keepdims=True)

    o_ref[...] = acc_ref[...]


def _dilated_counts(seq):
    t = ATTN_TILE
    n_blocks = seq // t
    delta = (np.arange(n_blocks)[:, None, None] * t
             + np.arange(t)[None, :, None] - np.arange(t)[None, None, :])
    counts = np.zeros(delta.shape, np.float32)
    for window, dil in DILATED_PATTERNS:
        counts += (delta >= 0) & (delta % dil == 0) & (delta <= window)
    return counts


def _dilated_kernel(q_ref, k_ref, v_ref, cnt_ref, o_ref, m_ref, l_ref, acc_ref):
    t = ATTN_TILE
    n_blocks = k_ref.shape[0] // t
    qi = pl.program_id(2)
    scale = HEAD_DIM ** -0.5
    q = q_ref[...]

    for back in range(n_blocks):
        @pl.when(back <= qi)
        def _(back=back):
            rows = pl.ds(pl.multiple_of((qi - back) * t, t), t)
            cnt = cnt_ref[back]
            s = jnp.where(cnt > 0.0, _dot_nt(q, k_ref[rows, :]) * scale, NEG_INF)
            _online_softmax_step(s, cnt, v_ref[rows, :], m_ref, l_ref, acc_ref, first=(back == 0))

    o_ref[...] = acc_ref[...] / l_ref[...]


def _pool_kernel(u_ref, w_ref, scale_ref, o_ref):
    seq = u_ref.shape[0]
    g = pl.program_id(1)
    t_idx = lax.broadcasted_iota(jnp.int32, (seq, 1), 0)
    for gi, window in enumerate(POOL_WINDOWS):
        @pl.when(g == gi)
        def _(window=window):
            u = u_ref[...]
            win_sum = u
            span = 1
            while span < window:
                win_sum = win_sum + jnp.where(t_idx >= span, pltpu.roll(win_sum, span, axis=0), 0.0)
                span *= 2
            count = jnp.minimum(t_idx + 1, window).astype(F32)
            d = win_sum / count - u
            o_ref[...] = _dot(d.astype(BF16), w_ref[0]) * scale_ref[...]


def _multiscale_pool(u, w_pool, pool_scale, *, batch, seq):
    n_groups, gd, _ = w_pool.shape
    assert all(w & (w - 1) == 0 for w in POOL_WINDOWS) and n_groups == len(POOL_WINDOWS)
    blk = 2 * _nbytes((seq, gd), F32) + _nbytes((gd, gd), BF16)
    return pl.pallas_call(
        _pool_kernel, grid=(batch, n_groups),
        in_specs=[pl.BlockSpec((seq, gd), lambda b, g: (b, g)),
                  pl.BlockSpec((1, gd, gd), lambda b, g: (g, 0, 0)),
                  pl.BlockSpec((1, gd), lambda b, g: (0, g))],
        out_specs=pl.BlockSpec((seq, gd), lambda b, g: (b, g)),
        out_shape=jax.ShapeDtypeStruct(u.shape, F32),
        compiler_params=_params(blk, temp_bytes=6 * _nbytes((seq, gd), F32)),
        name="multiscale_pool",
    )(u, w_pool, pool_scale.reshape(1, n_groups * gd))


def kernel(x, ln_mix_pre, w_in, w_pool, pool_scale, mix_out_norm, w_out, ln_mix_post,
           ln_ffn_pre, w_gate, w_up, w_down, ln_ffn_post):
    batch, seq, d_model = x.shape
    depth = w_in.shape[0]
    group = w_out.shape[1] // N_MIXERS
    n_heads = group // HEAD_DIM
    assert seq % ATTN_TILE == 0 and ATTN_TILE == MOBA_BLOCK and seq // MOBA_BLOCK > MOBA_TOPK
    assert w_in.shape[2] == 10 * group

    w_in, w_pool, w_out, w_gate, w_up, w_down = (
        w.astype(BF16) for w in (w_in, w_pool, w_out, w_gate, w_up, w_down))
    counts = jnp.asarray(_dilated_counts(seq))
    attn = functools.partial(_attn_call, batch=batch, seq=seq, n_heads=n_heads)
    cnt_spec = pl.BlockSpec(counts.shape, lambda b, h, t: (0, 0, 0))

    x = x.reshape(batch * seq, d_model)
    h = _rmsnorm_cast(x, ln_mix_pre[0])
    for l in range(depth):
        qkv = _matmul(h, w_in[l], out_dtype=BF16, tm=1024, tn=group, n_out=9 * group,
                      name="in_proj_qkv")
        u = _matmul(h, w_in[l], out_dtype=F32, tm=1024, tn=group, col_block_offset=9,
                    n_out=group, name="in_proj_pool")
        y_a = attn(_moba_kernel, qkv, mixer_slot=0, name="moba",
                   scratch=[pltpu.VMEM((seq // MOBA_BLOCK, HEAD_DIM), F32)] + _softmax_scratch())
        y_b = attn(_stickbreak_kernel, qkv, mixer_slot=1, name="stickbreak",
                   scratch=[pltpu.VMEM((ATTN_TILE, 1), F32), pltpu.VMEM((ATTN_TILE, HEAD_DIM), F32)])
        y_c = _multiscale_pool(u, w_pool[l], pool_scale[l], batch=batch, seq=seq)
        y_d = attn(_dilated_kernel, qkv, mixer_slot=2, name="dilated", scratch=_softmax_scratch(),
                   extra_inputs=(counts,), extra_specs=(cnt_spec,))
        hm = _mixer_norm((y_a, y_b, y_c, y_d), mix_out_norm[l])
        m = _matmul(hm, w_out[l], out_dtype=F32, tm=1024, tn=1024, name="out_proj")
        x, h = _residual_norm(x, m, ln_mix_post[l], ln_ffn_pre[l])
        act = _swiglu_up(h, w_gate[l], w_up[l])
        f = _matmul(act, w_down[l], out_dtype=F32, tm=512, tn=512, name="ffn_down")
        g_next = ln_mix_pre[l + 1] if l + 1 < depth else ln_mix_pre[0]
        x, h = _residual_norm(x, f, ln_ffn_post[l], g_next)
    return x.reshape(batch, seq, d_model)
```

```python
import functools
import math

import numpy as np
import jax
import jax.numpy as jnp
from jax import lax
from jax.experimental import pallas as pl
from jax.experimental.pallas import tpu as pltpu

HEAD_DIM = 128
N_MIXERS = 4
MOBA_BLOCK = 256
MOBA_TOPK = 3
POOL_WINDOWS = (2, 4, 8, 16)
DILATED_PATTERNS = ((128, 1), (512, 4), (2048, 16))
RMS_EPS = 1e-6
NEG_INF = -1e30

ATTN_TILE = 256
BF16_SUBLANES = 16
LANES = 128
V7X_VMEM_BYTES = 64 * 1024 * 1024
VMEM_LIMIT_CAP = V7X_VMEM_BYTES - 8 * 1024 * 1024
LOG2_E = math.log2(math.e)
SCORE_SCALE = HEAD_DIM ** -0.5

F32 = jnp.float32
BF16 = jnp.bfloat16


def _nbytes(shape, dtype):
    return int(np.prod(shape)) * jnp.dtype(dtype).itemsize


def _params(block_bytes, scratch_bytes=0, temp_bytes=0, n_axes=2):
    limit = 2 * block_bytes + scratch_bytes + temp_bytes
    limit = min(max(limit, 16 * 1024 * 1024), VMEM_LIMIT_CAP)
    return pltpu.CompilerParams(
        dimension_semantics=("arbitrary",) * n_axes, vmem_limit_bytes=int(limit))


def _dot(a, b):
    return jnp.dot(a, b, preferred_element_type=F32)


def _dot_nt(a, b):
    return lax.dot_general(a, b, (((1,), (1,)), ((), ())), preferred_element_type=F32)


def _dot_tn(a, b):
    return lax.dot_general(a, b, (((0,), (0,)), ((), ())), preferred_element_type=F32)


def _rms_scale(x):
    return lax.rsqrt(jnp.mean(x * x, axis=-1, keepdims=True) + RMS_EPS)


def _rmsnorm_cast_kernel(x_ref, g_ref, o_ref):
    x = x_ref[...]
    o_ref[...] = (x * _rms_scale(x) * g_ref[...]).astype(o_ref.dtype)


def _rmsnorm_cast(x, g, *, tm=512):
    m, d = x.shape
    blk = _nbytes((tm, d), F32) + _nbytes((tm, d), BF16)
    return pl.pallas_call(
        _rmsnorm_cast_kernel, grid=(m // tm,),
        in_specs=[pl.BlockSpec((tm, d), lambda i: (i, 0)),
                  pl.BlockSpec((1, d), lambda i: (0, 0))],
        out_specs=pl.BlockSpec((tm, d), lambda i: (i, 0)),
        out_shape=jax.ShapeDtypeStruct((m, d), BF16),
        compiler_params=_params(blk, temp_bytes=_nbytes((tm, d), F32), n_axes=1),
        name="rmsnorm_cast",
    )(x, g.reshape(1, d))


def _residual_norm_kernel(x_ref, m_ref, g_post_ref, g_next_ref, xo_ref, ho_ref):
    m = m_ref[...]
    x = x_ref[...] + m * _rms_scale(m) * g_post_ref[...]
    xo_ref[...] = x
    ho_ref[...] = (x * _rms_scale(x) * g_next_ref[...]).astype(ho_ref.dtype)


def _residual_norm(x, m, g_post, g_next, *, tm=256):
    rows, d = x.shape
    blk = 3 * _nbytes((tm, d), F32) + _nbytes((tm, d), BF16)
    row_spec = pl.BlockSpec((tm, d), lambda i: (i, 0))
    gain_spec = pl.BlockSpec((1, d), lambda i: (0, 0))
    return pl.pallas_call(
        _residual_norm_kernel, grid=(rows // tm,),
        in_specs=[row_spec, row_spec, gain_spec, gain_spec],
        out_specs=[row_spec, row_spec],
        out_shape=[jax.ShapeDtypeStruct((rows, d), F32), jax.ShapeDtypeStruct((rows, d), BF16)],
        compiler_params=_params(blk, temp_bytes=2 * _nbytes((tm, d), F32), n_axes=1),
        name="residual_norm",
    )(x, m, g_post.reshape(1, d), g_next.reshape(1, d))


def _mixer_norm_kernel(ya_ref, yb_ref, yc_ref, yd_ref, g_ref, o_ref):
    gw = ya_ref.shape[1]
    for i, y_ref in enumerate((ya_ref, yb_ref, yc_ref, yd_ref)):
        y = y_ref[...]
        cols = slice(i * gw, (i + 1) * gw)
        o_ref[:, cols] = (y * _rms_scale(y) * g_ref[:, cols]).astype(o_ref.dtype)


def _mixer_norm(ys, g, *, tm=512):
    rows, gw = ys[0].shape
    d = gw * len(ys)
    blk = len(ys) * _nbytes((tm, gw), F32) + _nbytes((tm, d), BF16)
    y_spec = pl.BlockSpec((tm, gw), lambda i: (i, 0))
    return pl.pallas_call(
        _mixer_norm_kernel, grid=(rows // tm,),
        in_specs=[y_spec] * len(ys) + [pl.BlockSpec((1, d), lambda i: (0, 0))],
        out_specs=pl.BlockSpec((tm, d), lambda i: (i, 0)),
        out_shape=jax.ShapeDtypeStruct((rows, d), BF16),
        compiler_params=_params(blk, temp_bytes=_nbytes((tm, d), F32), n_axes=1),
        name="mixer_norm",
    )(*ys, g.reshape(1, d))


def _matmul_kernel(a_ref, w_ref, o_ref):
    o_ref[...] = _dot(a_ref[...], w_ref[...]).astype(o_ref.dtype)


def _matmul(a, w, layer, *, out_dtype, tm, tn, col_block_offset=0, n_out=None, name="matmul"):
    m, k = a.shape
    n_out = w.shape[2] if n_out is None else n_out
    blk = _nbytes((tm, k), a.dtype) + _nbytes((k, tn), w.dtype) + _nbytes((tm, tn), out_dtype)
    return pl.pallas_call(
        _matmul_kernel, grid=(m // tm, n_out // tn),
        in_specs=[pl.BlockSpec((tm, k), lambda i, j: (i, 0)),
                  pl.BlockSpec((None, k, tn), lambda i, j: (layer, 0, j + col_block_offset))],
        out_specs=pl.BlockSpec((tm, tn), lambda i, j: (i, j)),
        out_shape=jax.ShapeDtypeStruct((m, n_out), out_dtype),
        compiler_params=_params(blk, temp_bytes=2 * _nbytes((tm, tn), F32)),
        name=name,
    )(a, w)


def _swiglu_up_kernel(h_ref, wg_ref, wu_ref, o_ref):
    h = h_ref[...]
    g = _dot(h, wg_ref[...])
    u = _dot(h, wu_ref[...])
    o_ref[...] = (g / (1.0 + jnp.exp(-g)) * u).astype(o_ref.dtype)


def _swiglu_up(h, wg, wu, layer, *, tm=1024, tn=256):
    m, k = h.shape
    n = wg.shape[2]
    blk = _nbytes((tm, k), BF16) + 2 * _nbytes((k, tn), BF16) + _nbytes((tm, tn), BF16)
    w_spec = pl.BlockSpec((None, k, tn), lambda i, j: (layer, 0, j))
    return pl.pallas_call(
        _swiglu_up_kernel, grid=(m // tm, n // tn),
        in_specs=[pl.BlockSpec((tm, k), lambda i, j: (i, 0)), w_spec, w_spec],
        out_specs=pl.BlockSpec((tm, tn), lambda i, j: (i, j)),
        out_shape=jax.ShapeDtypeStruct((m, n), BF16),
        compiler_params=_params(blk, temp_bytes=3 * _nbytes((tm, tn), F32)),
        name="swiglu_up",
    )(h, wg, wu)


def _attn_call(kernel, proj, *, batch, seq, n_heads, mixer_slot, extra_inputs=(), name):
    base = 3 * mixer_slot * n_heads
    head_blk = (seq, HEAD_DIM)
    in_specs = [pl.BlockSpec(head_blk, lambda b, h, off=base + i * n_heads: (b, off + h))
                for i in range(3)]
    in_specs += [pl.BlockSpec(x.shape, lambda b, h, nd=x.ndim: (0,) * nd) for x in extra_inputs]
    blk = (3 * _nbytes(head_blk, BF16) + _nbytes(head_blk, F32)
           + sum(_nbytes(x.shape, x.dtype) for x in extra_inputs))
    return pl.pallas_call(
        kernel, grid=(batch, n_heads),
        in_specs=in_specs,
        out_specs=pl.BlockSpec(head_blk, lambda b, h: (b, h)),
        out_shape=jax.ShapeDtypeStruct((batch * seq, n_heads * HEAD_DIM), F32),
        compiler_params=_params(blk, temp_bytes=8 * _nbytes((ATTN_TILE, seq), F32)),
        name=name,
    )(proj, proj, proj, *extra_inputs)


def _causal_tile_mask():
    row = lax.broadcasted_iota(jnp.int32, (ATTN_TILE, ATTN_TILE), 0)
    col = lax.broadcasted_iota(jnp.int32, (ATTN_TILE, ATTN_TILE), 1)
    return row, col


def _softmax_pv(s2, v):
    m = jnp.max(s2, axis=1, keepdims=True)
    p = jnp.exp2(s2 - m)
    l = jnp.sum(p, axis=1, keepdims=True)
    return _dot(p.astype(BF16), v) / l


def _moba_kernel(q_ref, k_ref, v_ref, o_ref):
    t = ATTN_TILE
    seq = k_ref.shape[0]
    n_blocks = seq // t
    rows_pad = BF16_SUBLANES
    q = q_ref[...]
    k = k_ref[...]

    kmean = jnp.sum(k.astype(F32).reshape(n_blocks, t, HEAD_DIM), axis=1) * (1.0 / t)
    kmean = jnp.concatenate([kmean, jnp.zeros((rows_pad - n_blocks, HEAD_DIM), F32)], axis=0)
    kmean_hi = kmean.astype(BF16)
    kmean_lo = (kmean - kmean_hi.astype(F32)).astype(BF16)
    gate = _dot_nt(kmean_hi, q) + _dot_nt(kmean_lo, q)

    blk_id = lax.broadcasted_iota(jnp.int32, gate.shape, 0)
    q_blk = lax.shift_right_logical(lax.broadcasted_iota(jnp.int32, gate.shape, 1),
                                    int(math.log2(t)))
    rank = jnp.zeros(gate.shape, jnp.int32)
    for i in range(n_blocks):
        gi = gate[i:i + 1, :]
        beats = jnp.where(gi > gate, 1, jnp.where((gi == gate) & (i < blk_id), 1, 0))
        rank = rank + jnp.where(i < q_blk, beats, 0)
    keep = (blk_id == q_blk) | ((blk_id < q_blk) & (rank < MOBA_TOPK))
    bias = jnp.where(keep, 0.0, NEG_INF).astype(BF16)
    key_blk = lax.shift_right_logical(lax.broadcasted_iota(jnp.int32, gate.shape, 1),
                                      int(math.log2(t)))
    indicator = jnp.where(blk_id == key_blk, 1.0, 0.0).astype(BF16)

    row, col = _causal_tile_mask()
    causal = col <= row
    for qi in range(n_blocks):
        rows = slice(qi * t, (qi + 1) * t)
        w = (qi + 1) * t
        s = _dot_nt(q[rows], k[:w]) + _dot_tn(bias[:, rows], indicator[:, :w])
        s_own = jnp.where(causal, s[:, w - t:], NEG_INF)
        s = s_own if qi == 0 else jnp.concatenate([s[:, :w - t], s_own], axis=1)
        o_ref[rows, :] = _softmax_pv(s * (SCORE_SCALE * LOG2_E), v_ref[:w, :])


def _stickbreak_kernel(q_ref, k_ref, v_ref, o_ref):
    t = ATTN_TILE
    seq = k_ref.shape[0]
    n_blocks = seq // t
    q = q_ref[...]
    k = k_ref[...]
    row, col = _causal_tile_mask()
    strict = col < row
    later = jnp.where(row > col, 1.0, 0.0).astype(BF16)
    later2 = jnp.concatenate([later, later], axis=0)
    ones2 = jnp.ones((2 * t, LANES), BF16)

    for qi in range(n_blocks):
        rows = slice(qi * t, (qi + 1) * t)
        n = qi + 1
        z = _dot_nt(q[rows], k[:n * t]) * SCORE_SCALE
        log_keep = jnp.minimum(-z, 0.0) - jnp.log(1.0 + jnp.exp(-jnp.abs(z)))
        blocks = [log_keep[:, j * t:(j + 1) * t] for j in range(n - 1)]
        blocks.append(jnp.where(strict, log_keep[:, (n - 1) * t:], 0.0))
        keep_rows = jnp.concatenate(blocks, axis=0)
        keep_hi = keep_rows.astype(BF16)
        keep_lo = (keep_rows - keep_hi.astype(F32)).astype(BF16)
        keep_hl = jnp.concatenate([keep_hi, keep_lo], axis=1)
        after_in_block = _dot(keep_hl, later2)
        block_sum = _dot(keep_hl, ones2)

        carry = jnp.zeros((t, LANES), F32)
        weights = [None] * n
        for j in range(n - 1, -1, -1):
            blk_rows = slice(j * t, (j + 1) * t)
            after = after_in_block[blk_rows] + jnp.concatenate([carry] * (t // LANES), axis=1)
            a = jnp.exp(z[:, blk_rows] + blocks[j] + after)
            weights[j] = jnp.where(strict, a, 0.0) if j == n - 1 else a
            carry = carry + block_sum[blk_rows]
        a_all = weights[0] if n == 1 else jnp.concatenate(weights, axis=1)
        o_ref[rows, :] = _dot(a_all.astype(BF16), v_ref[:n * t, :])


def _dilated_log2_counts(seq):
    t = ATTN_TILE
    delta = np.arange(t)[:, None] - np.arange(seq)[None, :] + (seq - t)
    counts = np.zeros(delta.shape, np.float64)
    for window, dil in DILATED_PATTERNS:
        counts += (delta >= 0) & (delta % dil == 0) & (delta <= window)
    return np.where(counts > 0, np.log2(np.maximum(counts, 1.0)), NEG_INF).astype(np.float32)


def _dilated_kernel(q_ref, k_ref, v_ref, log2cnt_ref, o_ref):
    t = ATTN_TILE
    seq = k_ref.shape[0]
    n_blocks = seq // t
    q = q_ref[...]
    k = k_ref[...]
    for qi in range(n_blocks):
        rows = slice(qi * t, (qi + 1) * t)
        w = (qi + 1) * t
        s2 = _dot_nt(q[rows], k[:w]) * (SCORE_SCALE * LOG2_E) + log2cnt_ref[:, seq - w:]
        o_ref[rows, :] = _softmax_pv(s2, v_ref[:w, :])


def _pool_kernel(u_ref, w_ref, scale_ref, o_ref):
    seq = u_ref.shape[0]
    g = pl.program_id(1)
    t_idx = lax.broadcasted_iota(jnp.int32, (seq, 1), 0)
    for gi, window in enumerate(POOL_WINDOWS):
        @pl.when(g == gi)
        def _(window=window):
            u = u_ref[...]
            win_sum = u
            span = 1
            while span < window:
                win_sum = win_sum + jnp.where(t_idx >= span, pltpu.roll(win_sum, span, axis=0), 0.0)
                span *= 2
            count = jnp.minimum(t_idx + 1, window).astype(F32)
            d = win_sum / count - u
            o_ref[...] = _dot(d.astype(BF16), w_ref[...]) * scale_ref[...]


def _multiscale_pool(u, w_pool, pool_scale, layer, *, batch, seq):
    _, n_groups, gd, _ = w_pool.shape
    assert all(w & (w - 1) == 0 for w in POOL_WINDOWS) and n_groups == len(POOL_WINDOWS)
    blk = 2 * _nbytes((seq, gd), F32) + _nbytes((gd, gd), BF16)
    return pl.pallas_call(
        _pool_kernel, grid=(batch, n_groups),
        in_specs=[pl.BlockSpec((seq, gd), lambda b, g: (b, g)),
                  pl.BlockSpec((None, None, gd, gd), lambda b, g: (layer, g, 0, 0)),
                  pl.BlockSpec((1, gd), lambda b, g: (0, g))],
        out_specs=pl.BlockSpec((seq, gd), lambda b, g: (b, g)),
        out_shape=jax.ShapeDtypeStruct(u.shape, F32),
        compiler_params=_params(blk, temp_bytes=6 * _nbytes((seq, gd), F32)),
        name="multiscale_pool",
    )(u, w_pool, pool_scale.reshape(1, n_groups * gd))


def kernel(x, ln_mix_pre, w_in, w_pool, pool_scale, mix_out_norm, w_out, ln_mix_post,
           ln_ffn_pre, w_gate, w_up, w_down, ln_ffn_post):
    batch, seq, d_model = x.shape
    depth = w_in.shape[0]
    group = w_out.shape[1] // N_MIXERS
    n_heads = group // HEAD_DIM
    n_blocks = seq // MOBA_BLOCK
    assert seq % ATTN_TILE == 0 and ATTN_TILE == MOBA_BLOCK and ATTN_TILE & (ATTN_TILE - 1) == 0
    assert MOBA_TOPK < n_blocks <= BF16_SUBLANES
    assert w_in.shape[2] == 10 * group

    w_in, w_pool, w_out, w_gate, w_up, w_down = (
        w.astype(BF16) for w in (w_in, w_pool, w_out, w_gate, w_up, w_down))
    log2cnt = jnp.asarray(_dilated_log2_counts(seq))
    attn = functools.partial(_attn_call, batch=batch, seq=seq, n_heads=n_heads)

    x = x.reshape(batch * seq, d_model)
    h = _rmsnorm_cast(x, ln_mix_pre[0])
    for l in range(depth):
        qkv = _matmul(h, w_in, l, out_dtype=BF16, tm=1024, tn=group, n_out=9 * group,
                      name="in_proj_qkv")
        u = _matmul(h, w_in, l, out_dtype=F32, tm=1024, tn=group, col_block_offset=9,
                    n_out=group, name="in_proj_pool")
        y_a = attn(_moba_kernel, qkv, mixer_slot=0, name="moba")
        y_b = attn(_stickbreak_kernel, qkv, mixer_slot=1, name="stickbreak")
        y_c = _multiscale_pool(u, w_pool, pool_scale[l], l, batch=batch, seq=seq)
        y_d = attn(_dilated_kernel, qkv, mixer_slot=2, name="dilated", extra_inputs=(log2cnt,))
        hm = _mixer_norm((y_a, y_b, y_c, y_d), mix_out_norm[l])
        m = _matmul(hm, w_out, l, out_dtype=F32, tm=1024, tn=1024, name="out_proj")
        x, h = _residual_norm(x, m, ln_mix_post[l], ln_ffn_pre[l])
        act = _swiglu_up(h, w_gate, w_up, l)
        f = _matmul(act, w_down, l, out_dtype=F32, tm=512, tn=512, name="ffn_down")
        g_next = ln_mix_pre[l + 1] if l + 1 < depth else ln_mix_pre[0]
        x, h = _residual_norm(x, f, ln_ffn_post[l], g_next)
    return x.reshape(batch, seq, d_model)
```

```python
import functools
import math

import numpy as np
import jax
import jax.numpy as jnp
from jax import lax
from jax.experimental import pallas as pl
from jax.experimental.pallas import tpu as pltpu

HEAD_DIM = 128
N_MIXERS = 4
MOBA_BLOCK = 256
MOBA_TOPK = 3
POOL_WINDOWS = (2, 4, 8, 16)
DILATED_PATTERNS = ((128, 1), (512, 4), (2048, 16))
RMS_EPS = 1e-6
NEG_INF = -1e30

ATTN_TILE = 256
BF16_SUBLANES = 16
LANES = 128
V7X_VMEM_BYTES = 64 * 1024 * 1024
VMEM_LIMIT_CAP = V7X_VMEM_BYTES - 8 * 1024 * 1024
LOG2_E = math.log2(math.e)
UNDERFLOW_LOG2 = -160.0
SCORE_SCALE = HEAD_DIM ** -0.5

F32 = jnp.float32
BF16 = jnp.bfloat16


def _nbytes(shape, dtype):
    return int(np.prod(shape)) * jnp.dtype(dtype).itemsize


def _params(block_bytes, scratch_bytes=0, temp_bytes=0, n_axes=2):
    limit = 2 * block_bytes + scratch_bytes + temp_bytes
    limit = min(max(limit, 16 * 1024 * 1024), VMEM_LIMIT_CAP)
    return pltpu.CompilerParams(
        dimension_semantics=("arbitrary",) * n_axes, vmem_limit_bytes=int(limit))


def _dot(a, b):
    return jnp.dot(a, b, preferred_element_type=F32)


def _dot_nt(a, b):
    return lax.dot_general(a, b, (((1,), (1,)), ((), ())), preferred_element_type=F32)


def _dot_tn(a, b):
    return lax.dot_general(a, b, (((0,), (0,)), ((), ())), preferred_element_type=F32)


def _rms_scale(x):
    return lax.rsqrt(jnp.mean(x * x, axis=-1, keepdims=True) + RMS_EPS)


def _rmsnorm_cast_kernel(x_ref, g_ref, o_ref):
    x = x_ref[...]
    o_ref[...] = (x * _rms_scale(x) * g_ref[...]).astype(o_ref.dtype)


def _rmsnorm_cast(x, g, *, tm=512):
    m, d = x.shape
    blk = _nbytes((tm, d), F32) + _nbytes((tm, d), BF16)
    return pl.pallas_call(
        _rmsnorm_cast_kernel, grid=(m // tm,),
        in_specs=[pl.BlockSpec((tm, d), lambda i: (i, 0)),
                  pl.BlockSpec((1, d), lambda i: (0, 0))],
        out_specs=pl.BlockSpec((tm, d), lambda i: (i, 0)),
        out_shape=jax.ShapeDtypeStruct((m, d), BF16),
        compiler_params=_params(blk, temp_bytes=_nbytes((tm, d), F32), n_axes=1),
        name="rmsnorm_cast",
    )(x, g.reshape(1, d))


def _residual_norm_kernel(x_ref, m_ref, g_post_ref, g_next_ref, xo_ref, ho_ref=None):
    m = m_ref[...]
    x = x_ref[...] + m * _rms_scale(m) * g_post_ref[...]
    xo_ref[...] = x
    if ho_ref is not None:
        ho_ref[...] = (x * _rms_scale(x) * g_next_ref[...]).astype(ho_ref.dtype)


def _residual_norm(x, m, g_post, g_next, *, emit_next=True, tm=256):
    rows, d = x.shape
    blk = 3 * _nbytes((tm, d), F32) + (_nbytes((tm, d), BF16) if emit_next else 0)
    row_spec = pl.BlockSpec((tm, d), lambda i: (i, 0))
    gain_spec = pl.BlockSpec((1, d), lambda i: (0, 0))
    out_shape = [jax.ShapeDtypeStruct((rows, d), F32)]
    if emit_next:
        out_shape.append(jax.ShapeDtypeStruct((rows, d), BF16))
    outs = pl.pallas_call(
        _residual_norm_kernel, grid=(rows // tm,),
        in_specs=[row_spec, row_spec, gain_spec, gain_spec],
        out_specs=[row_spec] * len(out_shape),
        out_shape=out_shape,
        compiler_params=_params(blk, temp_bytes=2 * _nbytes((tm, d), F32), n_axes=1),
        name="residual_norm",
    )(x, m, g_post.reshape(1, d), g_next.reshape(1, d))
    return (outs[0], outs[1]) if emit_next else (outs[0], None)


def _mixer_norm_kernel(ya_ref, yb_ref, yc_ref, yd_ref, g_ref, o_ref):
    gw = ya_ref.shape[1]
    for i, y_ref in enumerate((ya_ref, yb_ref, yc_ref, yd_ref)):
        y = y_ref[...]
        cols = slice(i * gw, (i + 1) * gw)
        o_ref[:, cols] = (y * _rms_scale(y) * g_ref[:, cols]).astype(o_ref.dtype)


def _mixer_norm(ys, g, *, tm=512):
    rows, gw = ys[0].shape
    d = gw * len(ys)
    blk = len(ys) * _nbytes((tm, gw), F32) + _nbytes((tm, d), BF16)
    y_spec = pl.BlockSpec((tm, gw), lambda i: (i, 0))
    return pl.pallas_call(
        _mixer_norm_kernel, grid=(rows // tm,),
        in_specs=[y_spec] * len(ys) + [pl.BlockSpec((1, d), lambda i: (0, 0))],
        out_specs=pl.BlockSpec((tm, d), lambda i: (i, 0)),
        out_shape=jax.ShapeDtypeStruct((rows, d), BF16),
        compiler_params=_params(blk, temp_bytes=_nbytes((tm, d), F32), n_axes=1),
        name="mixer_norm",
    )(*ys, g.reshape(1, d))


def _matmul_kernel(a_ref, w_ref, o_ref):
    o_ref[...] = _dot(a_ref[...], w_ref[...].astype(BF16)).astype(o_ref.dtype)


def _matmul(a, w, layer, *, out_dtype, tm, tn, col_block_offset=0, n_out=None, name="matmul"):
    m, k = a.shape
    n_out = w.shape[2] if n_out is None else n_out
    blk = _nbytes((tm, k), a.dtype) + _nbytes((k, tn), w.dtype) + _nbytes((tm, tn), out_dtype)
    blk += _nbytes((k, tn), BF16) // 2 if w.dtype != BF16 else 0
    return pl.pallas_call(
        _matmul_kernel, grid=(m // tm, n_out // tn),
        in_specs=[pl.BlockSpec((tm, k), lambda i, j: (i, 0)),
                  pl.BlockSpec((None, k, tn), lambda i, j: (layer, 0, j + col_block_offset))],
        out_specs=pl.BlockSpec((tm, tn), lambda i, j: (i, j)),
        out_shape=jax.ShapeDtypeStruct((m, n_out), out_dtype),
        compiler_params=_params(blk, temp_bytes=2 * _nbytes((tm, tn), F32)),
        name=name,
    )(a, w)


def _swiglu_up_kernel(h_ref, wg_ref, wu_ref, o_ref):
    h = h_ref[...]
    g = _dot(h, wg_ref[...].astype(BF16))
    u = _dot(h, wu_ref[...].astype(BF16))
    o_ref[...] = (g / (1.0 + jnp.exp(-g)) * u).astype(o_ref.dtype)


def _swiglu_up(h, wg, wu, layer, *, tm=1024, tn=256):
    m, k = h.shape
    n = wg.shape[2]
    blk = (_nbytes((tm, k), BF16) + 2 * _nbytes((k, tn), wg.dtype) + _nbytes((k, tn), BF16)
           + _nbytes((tm, tn), BF16))
    w_spec = pl.BlockSpec((None, k, tn), lambda i, j: (layer, 0, j))
    return pl.pallas_call(
        _swiglu_up_kernel, grid=(m // tm, n // tn),
        in_specs=[pl.BlockSpec((tm, k), lambda i, j: (i, 0)), w_spec, w_spec],
        out_specs=pl.BlockSpec((tm, tn), lambda i, j: (i, j)),
        out_shape=jax.ShapeDtypeStruct((m, n), BF16),
        compiler_params=_params(blk, temp_bytes=3 * _nbytes((tm, tn), F32)),
        name="swiglu_up",
    )(h, wg, wu)


def _attn_call(kernel, proj, *, batch, seq, n_heads, mixer_slot, extra_inputs=(), name):
    base = 3 * mixer_slot * n_heads
    head_blk = (seq, HEAD_DIM)
    in_specs = [pl.BlockSpec(head_blk, lambda b, h, off=base + i * n_heads: (b, off + h))
                for i in range(3)]
    in_specs += [pl.BlockSpec(x.shape, lambda b, h, nd=x.ndim: (0,) * nd) for x in extra_inputs]
    blk = (3 * _nbytes(head_blk, BF16) + _nbytes(head_blk, F32)
           + sum(_nbytes(x.shape, x.dtype) for x in extra_inputs))
    return pl.pallas_call(
        kernel, grid=(batch, n_heads),
        in_specs=in_specs,
        out_specs=pl.BlockSpec(head_blk, lambda b, h: (b, h)),
        out_shape=jax.ShapeDtypeStruct((batch * seq, n_heads * HEAD_DIM), F32),
        compiler_params=_params(blk, temp_bytes=8 * _nbytes((ATTN_TILE, seq), F32)),
        name=name,
    )(proj, proj, proj, *extra_inputs)


def _causal_tile_mask():
    row = lax.broadcasted_iota(jnp.int32, (ATTN_TILE, ATTN_TILE), 0)
    col = lax.broadcasted_iota(jnp.int32, (ATTN_TILE, ATTN_TILE), 1)
    return row, col


def _softmax_pv(s2, v):
    m = jnp.max(s2, axis=1, keepdims=True)
    p = jnp.exp2(s2 - m)
    l = jnp.sum(p, axis=1, keepdims=True)
    return _dot(p.astype(BF16), v) / l


def _moba_kernel(q_ref, k_ref, v_ref, o_ref):
    t = ATTN_TILE
    seq = k_ref.shape[0]
    n_blocks = seq // t
    rows_pad = BF16_SUBLANES
    q = q_ref[...]
    k = k_ref[...]

    kmean = jnp.sum(k.astype(F32).reshape(n_blocks, t, HEAD_DIM), axis=1) * (1.0 / t)
    kmean = jnp.concatenate([kmean, jnp.zeros((rows_pad - n_blocks, HEAD_DIM), F32)], axis=0)
    kmean_hi = kmean.astype(BF16)
    kmean_lo = (kmean - kmean_hi.astype(F32)).astype(BF16)
    gate = _dot_nt(kmean_hi, q) + _dot_nt(kmean_lo, q)

    blk_id = lax.broadcasted_iota(jnp.int32, gate.shape, 0)
    q_blk = lax.shift_right_logical(lax.broadcasted_iota(jnp.int32, gate.shape, 1),
                                    int(math.log2(t)))
    rank = jnp.zeros(gate.shape, jnp.int32)
    for i in range(n_blocks):
        gi = gate[i:i + 1, :]
        beats = jnp.where(gi > gate, 1, jnp.where((gi == gate) & (i < blk_id), 1, 0))
        rank = rank + jnp.where(i < q_blk, beats, 0)
    keep = (blk_id == q_blk) | ((blk_id < q_blk) & (rank < MOBA_TOPK))
    bias = jnp.where(keep, 0.0, NEG_INF).astype(BF16)

    lane_id = lax.broadcasted_iota(jnp.int32, (rows_pad, LANES), 1)
    eye = jnp.where(lane_id == lax.broadcasted_iota(jnp.int32, (rows_pad, LANES), 0), 1.0, 0.0)
    bias_cols = _dot_tn(bias, eye.astype(BF16)).astype(BF16)
    key_blk = lax.shift_right_logical(lax.broadcasted_iota(jnp.int32, (seq, LANES), 0),
                                      int(math.log2(t)))
    one_hot = jnp.where(key_blk == lax.broadcasted_iota(jnp.int32, (seq, LANES), 1), 1.0, 0.0)
    q_aug = jnp.concatenate([q, bias_cols], axis=1)
    k_aug = jnp.concatenate([k, one_hot.astype(BF16)], axis=1)

    row, col = _causal_tile_mask()
    causal = col <= row
    to_log2 = SCORE_SCALE * LOG2_E
    for qi in range(n_blocks):
        rows = slice(qi * t, (qi + 1) * t)
        past = qi * t
        s_own = jnp.where(causal, _dot_nt(q_aug[rows], k_aug[past:past + t]), NEG_INF) * to_log2
        m = jnp.max(s_own, axis=1, keepdims=True)
        if qi > 0:
            s_past = _dot_nt(q_aug[rows], k_aug[:past]) * to_log2
            m = jnp.maximum(m, jnp.max(s_past, axis=1, keepdims=True))
        p_own = jnp.exp2(s_own - m)
        l = jnp.sum(p_own, axis=1, keepdims=True)
        pv = _dot(p_own.astype(BF16), v_ref[past:past + t, :])
        if qi > 0:
            p_past = jnp.exp2(s_past - m)
            l = l + jnp.sum(p_past, axis=1, keepdims=True)
            pv = pv + _dot(p_past.astype(BF16), v_ref[:past, :])
        o_ref[rows, :] = pv / l


def _stickbreak_kernel(q_ref, k_ref, v_ref, o_ref):
    t = ATTN_TILE
    seq = k_ref.shape[0]
    n_blocks = seq // t
    q = q_ref[...]
    k = k_ref[...]
    row, col = _causal_tile_mask()
    strict = col < row
    later = jnp.where(row > col, 1.0, 0.0).astype(BF16)
    later2 = jnp.concatenate([later, later], axis=0)

    def block_weights(q_tile, first, n, carry, diagonal_last):
        neg_z = _dot_nt(q_tile, k[first * t:(first + n) * t]) * (-SCORE_SCALE * LOG2_E)
        neg_abs = lax.bitcast_convert_type(
            lax.bitcast_convert_type(neg_z, jnp.uint32) | jnp.uint32(0x80000000), F32)
        log_keep = jnp.minimum(neg_z, 0.0) - jnp.log2(1.0 + jnp.exp2(neg_abs))
        blocks = [log_keep[:, j * t:(j + 1) * t] for j in range(n)]
        if diagonal_last:
            blocks[-1] = jnp.where(strict, blocks[-1], 0.0)
        keep_rows = jnp.concatenate(blocks, axis=0)
        hi = lax.bitcast_convert_type(
            lax.bitcast_convert_type(keep_rows, jnp.uint32) & jnp.uint32(0xFFFF0000), F32)
        keep_hl = jnp.concatenate([hi.astype(BF16), (keep_rows - hi).astype(BF16)], axis=1)
        after_in_block = _dot(keep_hl, later2)
        weights = [None] * n
        for j in range(n - 1, -1, -1):
            after = after_in_block[j * t:(j + 1) * t]
            a = jnp.exp2(blocks[j] - neg_z[:, j * t:(j + 1) * t] + after + carry)
            weights[j] = jnp.where(strict, a, 0.0) if (diagonal_last and j == n - 1) else a
            carry = carry + after[:, 0:1] + blocks[j][:, 0:1]
        return (weights[0] if n == 1 else jnp.concatenate(weights, axis=1)), carry

    for qi in range(n_blocks):
        rows = slice(qi * t, (qi + 1) * t)
        q_tile = q[rows]
        first = max(qi - 1, 0)
        near = qi + 1 - first
        a, carry = block_weights(q_tile, first, near, jnp.zeros((t, 1), F32), True)
        o_ref[rows, :] = _dot(a.astype(BF16), v_ref[first * t:(qi + 1) * t, :])
        if first > 0:
            @pl.when(jnp.max(carry) >= UNDERFLOW_LOG2)
            def _(rows=rows, q_tile=q_tile, first=first, carry=carry):
                a_far, _ = block_weights(q_tile, 0, first, carry, False)
                o_ref[rows, :] += _dot(a_far.astype(BF16), v_ref[:first * t, :])


def _dilated_log2_counts(seq):
    t = ATTN_TILE
    delta = np.arange(t)[:, None] - np.arange(seq)[None, :] + (seq - t)
    counts = np.zeros(delta.shape, np.float64)
    for window, dil in DILATED_PATTERNS:
        counts += (delta >= 0) & (delta % dil == 0) & (delta <= window)
    return np.where(counts > 0, np.log2(np.maximum(counts, 1.0)), NEG_INF).astype(np.float32)


def _dilated_kernel(q_ref, k_ref, v_ref, log2cnt_ref, o_ref):
    t = ATTN_TILE
    seq = k_ref.shape[0]
    n_blocks = seq // t
    q = q_ref[...]
    k = k_ref[...]
    for qi in range(n_blocks):
        rows = slice(qi * t, (qi + 1) * t)
        w = (qi + 1) * t
        s2 = _dot_nt(q[rows], k[:w]) * (SCORE_SCALE * LOG2_E) + log2cnt_ref[:, seq - w:]
        o_ref[rows, :] = _softmax_pv(s2, v_ref[:w, :])


def _pool_kernel(u_ref, w_ref, scale_ref, o_ref):
    seq = u_ref.shape[0]
    g = pl.program_id(1)
    t_idx = lax.broadcasted_iota(jnp.int32, (seq, 1), 0)
    for gi, window in enumerate(POOL_WINDOWS):
        @pl.when(g == gi)
        def _(window=window):
            u = u_ref[...]
            win_sum = u
            span = 1
            while span < window:
                win_sum = win_sum + jnp.where(t_idx >= span, pltpu.roll(win_sum, span, axis=0), 0.0)
                span *= 2
            count = jnp.minimum(t_idx + 1, window).astype(F32)
            d = win_sum / count - u
            o_ref[...] = _dot(d.astype(BF16), w_ref[...]) * scale_ref[...]


def _multiscale_pool(u, w_pool, pool_scale, layer, *, batch, seq):
    _, n_groups, gd, _ = w_pool.shape
    assert all(w & (w - 1) == 0 for w in POOL_WINDOWS) and n_groups == len(POOL_WINDOWS)
    blk = 2 * _nbytes((seq, gd), F32) + _nbytes((gd, gd), BF16)
    return pl.pallas_call(
        _pool_kernel, grid=(batch, n_groups),
        in_specs=[pl.BlockSpec((seq, gd), lambda b, g: (b, g)),
                  pl.BlockSpec((None, None, gd, gd), lambda b, g: (layer, g, 0, 0)),
                  pl.BlockSpec((1, gd), lambda b, g: (0, g))],
        out_specs=pl.BlockSpec((seq, gd), lambda b, g: (b, g)),
        out_shape=jax.ShapeDtypeStruct(u.shape, F32),
        compiler_params=_params(blk, temp_bytes=6 * _nbytes((seq, gd), F32)),
        name="multiscale_pool",
    )(u, w_pool, pool_scale.reshape(1, n_groups * gd))


def kernel(x, ln_mix_pre, w_in, w_pool, pool_scale, mix_out_norm, w_out, ln_mix_post,
           ln_ffn_pre, w_gate, w_up, w_down, ln_ffn_post):
    batch, seq, d_model = x.shape
    depth = w_in.shape[0]
    group = w_out.shape[1] // N_MIXERS
    n_heads = group // HEAD_DIM
    n_blocks = seq // MOBA_BLOCK
    assert seq % ATTN_TILE == 0 and ATTN_TILE == MOBA_BLOCK and ATTN_TILE & (ATTN_TILE - 1) == 0
    assert MOBA_TOPK < n_blocks <= BF16_SUBLANES
    assert w_in.shape[2] == 10 * group

    w_pool, w_down = w_pool.astype(BF16), w_down.astype(BF16)
    proj_tn = group // 2
    log2cnt = jnp.asarray(_dilated_log2_counts(seq))
    attn = functools.partial(_attn_call, batch=batch, seq=seq, n_heads=n_heads)

    x = x.reshape(batch * seq, d_model)
    h = _rmsnorm_cast(x, ln_mix_pre[0])
    for l in range(depth):
        qkv = _matmul(h, w_in, l, out_dtype=BF16, tm=1024, tn=proj_tn, n_out=9 * group,
                      name="in_proj_qkv")
        u = _matmul(h, w_in, l, out_dtype=F32, tm=1024, tn=proj_tn,
                    col_block_offset=9 * group // proj_tn, n_out=group, name="in_proj_pool")
        y_a = attn(_moba_kernel, qkv, mixer_slot=0, name="moba")
        y_b = attn(_stickbreak_kernel, qkv, mixer_slot=1, name="stickbreak")
        y_c = _multiscale_pool(u, w_pool, pool_scale[l], l, batch=batch, seq=seq)
        y_d = attn(_dilated_kernel, qkv, mixer_slot=2, name="dilated", extra_inputs=(log2cnt,))
        hm = _mixer_norm((y_a, y_b, y_c, y_d), mix_out_norm[l])
        m = _matmul(hm, w_out, l, out_dtype=F32, tm=1024, tn=proj_tn, name="out_proj")
        x, h = _residual_norm(x, m, ln_mix_post[l], ln_ffn_pre[l])
        act = _swiglu_up(h, w_gate, w_up, l)
        f = _matmul(act, w_down, l, out_dtype=F32, tm=512, tn=512, name="ffn_down")
        last = l + 1 == depth
        x, h = _residual_norm(x, f, ln_ffn_post[l], ln_mix_pre[0 if last else l + 1],
                              emit_next=not last)
    return x.reshape(batch, seq, d_model)
```

```python
import functools
import math

import numpy as np
import jax
import jax.numpy as jnp
from jax import lax
from jax.experimental import pallas as pl
from jax.experimental.pallas import tpu as pltpu

HEAD_DIM = 128
N_MIXERS = 4
MOBA_BLOCK = 256
MOBA_TOPK = 3
POOL_WINDOWS = (2, 4, 8, 16)
DILATED_PATTERNS = ((128, 1), (512, 4), (2048, 16))
RMS_EPS = 1e-6
NEG_INF = -1e30

ATTN_TILE = 256
BF16_SUBLANES = 16
LANES = 128
V7X_VMEM_BYTES = 64 * 1024 * 1024
VMEM_LIMIT_CAP = V7X_VMEM_BYTES - 8 * 1024 * 1024
LOG2_E = math.log2(math.e)
UNDERFLOW_LOG2 = -160.0
SCORE_SCALE = HEAD_DIM ** -0.5

F32 = jnp.float32
BF16 = jnp.bfloat16


def _nbytes(shape, dtype):
    return int(np.prod(shape)) * jnp.dtype(dtype).itemsize


def _params(block_bytes, scratch_bytes=0, temp_bytes=0, n_axes=2):
    limit = 2 * block_bytes + scratch_bytes + temp_bytes
    limit = min(max(limit, 16 * 1024 * 1024), VMEM_LIMIT_CAP)
    return pltpu.CompilerParams(
        dimension_semantics=("arbitrary",) * n_axes, vmem_limit_bytes=int(limit))


def _dot(a, b):
    return jnp.dot(a, b, preferred_element_type=F32)


def _dot_nt(a, b):
    return lax.dot_general(a, b, (((1,), (1,)), ((), ())), preferred_element_type=F32)


def _dot_tn(a, b):
    return lax.dot_general(a, b, (((0,), (0,)), ((), ())), preferred_element_type=F32)


def _rms_scale(x):
    return lax.rsqrt(jnp.mean(x * x, axis=-1, keepdims=True) + RMS_EPS)


def _rmsnorm_cast_kernel(x_ref, g_ref, o_ref):
    x = x_ref[...]
    o_ref[...] = (x * _rms_scale(x) * g_ref[...]).astype(o_ref.dtype)


def _rmsnorm_cast(x, g, *, tm=512):
    m, d = x.shape
    blk = _nbytes((tm, d), F32) + _nbytes((tm, d), BF16)
    return pl.pallas_call(
        _rmsnorm_cast_kernel, grid=(m // tm,),
        in_specs=[pl.BlockSpec((tm, d), lambda i: (i, 0)),
                  pl.BlockSpec((1, d), lambda i: (0, 0))],
        out_specs=pl.BlockSpec((tm, d), lambda i: (i, 0)),
        out_shape=jax.ShapeDtypeStruct((m, d), BF16),
        compiler_params=_params(blk, temp_bytes=_nbytes((tm, d), F32), n_axes=1),
        name="rmsnorm_cast",
    )(x, g.reshape(1, d))


def _residual_norm_kernel(x_ref, m_ref, g_post_ref, g_next_ref, xo_ref, ho_ref=None):
    m = m_ref[...]
    x = x_ref[...] + m * _rms_scale(m) * g_post_ref[...]
    xo_ref[...] = x
    if ho_ref is not None:
        ho_ref[...] = (x * _rms_scale(x) * g_next_ref[...]).astype(ho_ref.dtype)


def _residual_norm(x, m, g_post, g_next, *, emit_next=True, tm=256):
    rows, d = x.shape
    blk = 3 * _nbytes((tm, d), F32) + (_nbytes((tm, d), BF16) if emit_next else 0)
    row_spec = pl.BlockSpec((tm, d), lambda i: (i, 0))
    gain_spec = pl.BlockSpec((1, d), lambda i: (0, 0))
    out_shape = [jax.ShapeDtypeStruct((rows, d), F32)]
    if emit_next:
        out_shape.append(jax.ShapeDtypeStruct((rows, d), BF16))
    outs = pl.pallas_call(
        _residual_norm_kernel, grid=(rows // tm,),
        in_specs=[row_spec, row_spec, gain_spec, gain_spec],
        out_specs=[row_spec] * len(out_shape),
        out_shape=out_shape,
        compiler_params=_params(blk, temp_bytes=2 * _nbytes((tm, d), F32), n_axes=1),
        name="residual_norm",
    )(x, m, g_post.reshape(1, d), g_next.reshape(1, d))
    return (outs[0], outs[1]) if emit_next else (outs[0], None)


def _matmul_kernel(a_ref, w_ref, o_ref):
    o_ref[...] = _dot(a_ref[...], w_ref[...].astype(BF16)).astype(o_ref.dtype)


def _matmul(a, w, layer, *, out_dtype, tm, tn, col_block_offset=0, n_out=None, name="matmul"):
    m, k = a.shape
    n_out = w.shape[2] if n_out is None else n_out
    blk = _nbytes((tm, k), a.dtype) + _nbytes((k, tn), w.dtype) + _nbytes((tm, tn), out_dtype)
    blk += _nbytes((k, tn), BF16) // 2 if w.dtype != BF16 else 0
    return pl.pallas_call(
        _matmul_kernel, grid=(m // tm, n_out // tn),
        in_specs=[pl.BlockSpec((tm, k), lambda i, j: (i, 0)),
                  pl.BlockSpec((None, k, tn), lambda i, j: (layer, 0, j + col_block_offset))],
        out_specs=pl.BlockSpec((tm, tn), lambda i, j: (i, j)),
        out_shape=jax.ShapeDtypeStruct((m, n_out), out_dtype),
        compiler_params=_params(blk, temp_bytes=2 * _nbytes((tm, tn), F32)),
        name=name,
    )(a, w)


def _in_proj_kernel(h_ref, w_ref, qkv_ref, u_ref, *, qkv_blocks):
    j = pl.program_id(1)
    y = _dot(h_ref[...], w_ref[...].astype(BF16))

    @pl.when(j < qkv_blocks)
    def _():
        qkv_ref[...] = y.astype(qkv_ref.dtype)

    @pl.when(j >= qkv_blocks)
    def _():
        u_ref[...] = y


def _in_proj(h, w_in, layer, *, qkv_width, tm, tn):
    m, k = h.shape
    n = w_in.shape[2]
    qkv_blocks = qkv_width // tn
    blk = (_nbytes((tm, k), BF16) + _nbytes((k, tn), w_in.dtype) + _nbytes((k, tn), BF16) // 2
           + _nbytes((tm, tn), BF16) + _nbytes((tm, tn), F32))
    return pl.pallas_call(
        functools.partial(_in_proj_kernel, qkv_blocks=qkv_blocks), grid=(m // tm, n // tn),
        in_specs=[pl.BlockSpec((tm, k), lambda i, j: (i, 0)),
                  pl.BlockSpec((None, k, tn), lambda i, j: (layer, 0, j))],
        out_specs=[pl.BlockSpec((tm, tn), lambda i, j: (i, jnp.minimum(j, qkv_blocks - 1))),
                   pl.BlockSpec((tm, tn), lambda i, j: (i, jnp.maximum(j - qkv_blocks, 0)))],
        out_shape=[jax.ShapeDtypeStruct((m, qkv_width), BF16),
                   jax.ShapeDtypeStruct((m, n - qkv_width), F32)],
        compiler_params=_params(blk, temp_bytes=2 * _nbytes((tm, tn), F32)),
        name="in_proj",
    )(h, w_in)


def _out_proj_kernel(ya_ref, yb_ref, yc_ref, yd_ref, g_ref, w_ref, o_ref, hm_ref):
    @pl.when(pl.program_id(1) == 0)
    def _():
        gw = ya_ref.shape[1]
        for i, y_ref in enumerate((ya_ref, yb_ref, yc_ref, yd_ref)):
            y = y_ref[...]
            cols = slice(i * gw, (i + 1) * gw)
            hm_ref[:, cols] = (y * _rms_scale(y) * g_ref[:, cols]).astype(hm_ref.dtype)

    o_ref[...] = _dot(hm_ref[...], w_ref[...])


def _out_proj(ys, g, w_out, layer, *, tm=512, tn=1024):
    rows, gw = ys[0].shape
    d = gw * len(ys)
    n = w_out.shape[2]
    blk = (len(ys) * _nbytes((tm, gw), F32) + _nbytes((d, tn), w_out.dtype)
           + _nbytes((tm, tn), F32))
    y_spec = pl.BlockSpec((tm, gw), lambda i, j: (i, 0))
    return pl.pallas_call(
        _out_proj_kernel, grid=(rows // tm, n // tn),
        in_specs=[y_spec] * len(ys) + [pl.BlockSpec((1, d), lambda i, j: (0, 0)),
                                       pl.BlockSpec((None, d, tn), lambda i, j: (layer, 0, j))],
        out_specs=pl.BlockSpec((tm, tn), lambda i, j: (i, j)),
        out_shape=jax.ShapeDtypeStruct((rows, n), F32),
        scratch_shapes=[pltpu.VMEM((tm, d), BF16)],
        compiler_params=_params(blk, scratch_bytes=_nbytes((tm, d), BF16),
                                temp_bytes=2 * _nbytes((tm, tn), F32)),
        name="out_proj",
    )(*ys, g.reshape(1, d), w_out)


def _swiglu_up_kernel(h_ref, wg_ref, wu_ref, wd_ref, o_ref, wd_bf16_ref):
    h = h_ref[...]
    g = _dot(h, wg_ref[...].astype(BF16))
    u = _dot(h, wu_ref[...].astype(BF16))
    o_ref[...] = (g / (1.0 + jnp.exp(-g)) * u).astype(o_ref.dtype)
    wd_bf16_ref[...] = wd_ref[...].astype(wd_bf16_ref.dtype)


def _swiglu_up(h, wg, wu, wd, layer, *, tm=1024, tn=256):
    m, k = h.shape
    n = wg.shape[2]
    grid = (m // tm, n // tn)
    d_ff, d_out = wd.shape[1:]
    slab = d_ff // (grid[0] * grid[1])
    assert slab * grid[0] * grid[1] == d_ff and slab % BF16_SUBLANES == 0
    blk = (_nbytes((tm, k), BF16) + 2 * _nbytes((k, tn), wg.dtype) + _nbytes((k, tn), BF16)
           + _nbytes((tm, tn), BF16) + _nbytes((slab, d_out), F32) + _nbytes((slab, d_out), BF16))
    w_spec = pl.BlockSpec((None, k, tn), lambda i, j: (layer, 0, j))
    return pl.pallas_call(
        _swiglu_up_kernel, grid=grid,
        in_specs=[pl.BlockSpec((tm, k), lambda i, j: (i, 0)), w_spec, w_spec,
                  pl.BlockSpec((None, slab, d_out), lambda i, j: (layer, i * grid[1] + j, 0))],
        out_specs=[pl.BlockSpec((tm, tn), lambda i, j: (i, j)),
                   pl.BlockSpec((slab, d_out), lambda i, j: (i * grid[1] + j, 0))],
        out_shape=[jax.ShapeDtypeStruct((m, n), BF16), jax.ShapeDtypeStruct((d_ff, d_out), BF16)],
        compiler_params=_params(blk, temp_bytes=3 * _nbytes((tm, tn), F32)),
        name="swiglu_up",
    )(h, wg, wu, wd)


def _attn_call(kernel, proj, *, batch, seq, n_heads, mixer_slot, extra_inputs=(), name):
    base = 3 * mixer_slot * n_heads
    head_blk = (seq, HEAD_DIM)
    in_specs = [pl.BlockSpec(head_blk, lambda b, h, off=base + i * n_heads: (b, off + h))
                for i in range(3)]
    in_specs += [pl.BlockSpec(x.shape, lambda b, h, nd=x.ndim: (0,) * nd) for x in extra_inputs]
    blk = (3 * _nbytes(head_blk, BF16) + _nbytes(head_blk, F32)
           + sum(_nbytes(x.shape, x.dtype) for x in extra_inputs))
    return pl.pallas_call(
        kernel, grid=(batch, n_heads),
        in_specs=in_specs,
        out_specs=pl.BlockSpec(head_blk, lambda b, h: (b, h)),
        out_shape=jax.ShapeDtypeStruct((batch * seq, n_heads * HEAD_DIM), F32),
        compiler_params=_params(blk, temp_bytes=8 * _nbytes((2 * seq, ATTN_TILE), F32)),
        name=name,
    )(proj, proj, proj, *extra_inputs)


def _causal_tile_mask():
    row = lax.broadcasted_iota(jnp.int32, (ATTN_TILE, ATTN_TILE), 0)
    col = lax.broadcasted_iota(jnp.int32, (ATTN_TILE, ATTN_TILE), 1)
    return row, col


def _softmax_pv(s2, v):
    m = jnp.max(s2, axis=1, keepdims=True)
    p = jnp.exp2(s2 - m)
    l = jnp.sum(p, axis=1, keepdims=True)
    return _dot(p.astype(BF16), v) / l


def _moba_kernel(q_ref, k_ref, v_ref, o_ref):
    t = ATTN_TILE
    seq = k_ref.shape[0]
    n_blocks = seq // t
    rows_pad = BF16_SUBLANES
    q = q_ref[...]
    k = k_ref[...]

    kmean = jnp.sum(k.astype(F32).reshape(n_blocks, t, HEAD_DIM), axis=1) * (1.0 / t)
    kmean = jnp.concatenate([kmean, jnp.zeros((rows_pad - n_blocks, HEAD_DIM), F32)], axis=0)
    kmean_hi = kmean.astype(BF16)
    kmean_lo = (kmean - kmean_hi.astype(F32)).astype(BF16)
    gate = _dot_nt(kmean_hi, q) + _dot_nt(kmean_lo, q)

    blk_id = lax.broadcasted_iota(jnp.int32, gate.shape, 0)
    q_blk = lax.shift_right_logical(lax.broadcasted_iota(jnp.int32, gate.shape, 1),
                                    int(math.log2(t)))
    rank = jnp.zeros(gate.shape, jnp.int32)
    for i in range(n_blocks):
        gi = gate[i:i + 1, :]
        beats = jnp.where(gi > gate, 1, jnp.where((gi == gate) & (i < blk_id), 1, 0))
        rank = rank + jnp.where(i < q_blk, beats, 0)
    keep = (blk_id == q_blk) | ((blk_id < q_blk) & (rank < MOBA_TOPK))
    bias = jnp.where(keep, 0.0, NEG_INF).astype(BF16)

    lane_id = lax.broadcasted_iota(jnp.int32, (rows_pad, LANES), 1)
    eye = jnp.where(lane_id == lax.broadcasted_iota(jnp.int32, (rows_pad, LANES), 0), 1.0, 0.0)
    bias_cols = _dot_tn(bias, eye.astype(BF16)).astype(BF16)
    key_blk = lax.shift_right_logical(lax.broadcasted_iota(jnp.int32, (seq, LANES), 0),
                                      int(math.log2(t)))
    one_hot = jnp.where(key_blk == lax.broadcasted_iota(jnp.int32, (seq, LANES), 1), 1.0, 0.0)
    q_aug = jnp.concatenate([q, bias_cols], axis=1)
    k_aug = jnp.concatenate([k, one_hot.astype(BF16)], axis=1)

    row, col = _causal_tile_mask()
    causal = col <= row
    to_log2 = SCORE_SCALE * LOG2_E
    for qi in range(n_blocks):
        rows = slice(qi * t, (qi + 1) * t)
        past = qi * t
        s_own = jnp.where(causal, _dot_nt(q_aug[rows], k_aug[past:past + t]), NEG_INF) * to_log2
        m = jnp.max(s_own, axis=1, keepdims=True)
        if qi > 0:
            s_past = _dot_nt(q_aug[rows], k_aug[:past]) * to_log2
            m = jnp.maximum(m, jnp.max(s_past, axis=1, keepdims=True))
        p_own = jnp.exp2(s_own - m)
        l = jnp.sum(p_own, axis=1, keepdims=True)
        pv = _dot(p_own.astype(BF16), v_ref[past:past + t, :])
        if qi > 0:
            p_past = jnp.exp2(s_past - m)
            l = l + jnp.sum(p_past, axis=1, keepdims=True)
            pv = pv + _dot(p_past.astype(BF16), v_ref[:past, :])
        o_ref[rows, :] = pv / l


def _log2_keep(neg_z):
    neg_abs = lax.bitcast_convert_type(
        lax.bitcast_convert_type(neg_z, jnp.uint32) | jnp.uint32(0x80000000), F32)
    return jnp.minimum(neg_z, 0.0) - jnp.log2(1.0 + jnp.exp2(neg_abs))


def _split_hi_lo(x):
    hi = lax.bitcast_convert_type(
        lax.bitcast_convert_type(x, jnp.uint32) & jnp.uint32(0xFFFF0000), F32)
    return jnp.concatenate([hi.astype(BF16), (x - hi).astype(BF16)], axis=1)


def _stickbreak_kernel(q_ref, k_ref, v_ref, o_ref):
    t = ATTN_TILE
    seq = k_ref.shape[0]
    n_blocks = seq // t
    q = q_ref[...]
    k = k_ref[...]
    to_neg_log2 = -SCORE_SCALE * LOG2_E
    row, col = _causal_tile_mask()
    later = jnp.where(row > col, 1.0, 0.0).astype(BF16)
    later2 = jnp.concatenate([later, later], axis=0)

    def tile(x, i):
        return x[i * t:(i + 1) * t]

    neg_z = jnp.concatenate(
        [_dot_nt(tile(q, i), tile(k, i)) for i in range(n_blocks)]
        + [_dot_nt(tile(q, i), tile(k, i - 1)) for i in range(1, n_blocks)], axis=0) * to_neg_log2
    row_in_tile = lax.broadcasted_iota(jnp.int32, (seq, t), 0) & (t - 1)
    strict = lax.broadcasted_iota(jnp.int32, (seq, t), 1) < row_in_tile
    log_keep = _log2_keep(neg_z)
    log_keep = jnp.concatenate([jnp.where(strict, log_keep[:seq], 0.0), log_keep[seq:]], axis=0)
    after = _dot(_split_hi_lo(log_keep), later2)
    block_sum = after[:, 0:1] + log_keep[:, 0:1]
    log_w = log_keep - neg_z + after
    a_own = jnp.where(strict, jnp.exp2(log_w[:seq]), 0.0)
    a_prev = jnp.exp2(log_w[seq:] + block_sum[t:seq])
    carry = block_sum[t:seq] + block_sum[seq:]

    o_ref[:t, :] = _dot(tile(a_own, 0).astype(BF16), v_ref[:t, :])
    for i in range(1, n_blocks):
        a = jnp.concatenate([tile(a_prev, i - 1), tile(a_own, i)], axis=1).astype(BF16)
        o_ref[i * t:(i + 1) * t, :] = _dot(a, v_ref[(i - 1) * t:(i + 1) * t, :])

    far_needed = [jnp.max(tile(carry, i - 1)) >= UNDERFLOW_LOG2 for i in range(2, n_blocks)]
    for i in range(2, n_blocks):
        @pl.when(far_needed[i - 2])
        def _(i=i):
            n = i - 1
            neg_z = _dot_nt(tile(q, i), k[:n * t]) * to_neg_log2
            log_keep = _log2_keep(neg_z)
            stacked = jnp.concatenate([log_keep[:, j * t:(j + 1) * t] for j in range(n)], axis=0)
            after = _dot(_split_hi_lo(stacked), later2)
            c = tile(carry, i - 1)
            weights = [None] * n
            for j in range(n - 1, -1, -1):
                blk = slice(j * t, (j + 1) * t)
                weights[j] = jnp.exp2(log_keep[:, blk] - neg_z[:, blk] + after[blk] + c)
                c = c + after[blk][:, 0:1] + log_keep[:, j * t:j * t + 1]
            a = weights[0] if n == 1 else jnp.concatenate(weights, axis=1)
            o_ref[i * t:(i + 1) * t, :] += _dot(a.astype(BF16), v_ref[:n * t, :])


def _dilated_log2_counts(seq):
    t = ATTN_TILE
    delta = np.arange(t)[:, None] - np.arange(seq)[None, :] + (seq - t)
    counts = np.zeros(delta.shape, np.float64)
    for window, dil in DILATED_PATTERNS:
        counts += (delta >= 0) & (delta % dil == 0) & (delta <= window)
    return np.where(counts > 0, np.log2(np.maximum(counts, 1.0)), NEG_INF).astype(np.float32)


def _dilated_kernel(q_ref, k_ref, v_ref, log2cnt_ref, o_ref):
    t = ATTN_TILE
    seq = k_ref.shape[0]
    n_blocks = seq // t
    q = q_ref[...]
    k = k_ref[...]
    for qi in range(n_blocks):
        rows = slice(qi * t, (qi + 1) * t)
        w = (qi + 1) * t
        s2 = _dot_nt(q[rows], k[:w]) * (SCORE_SCALE * LOG2_E) + log2cnt_ref[:, seq - w:]
        o_ref[rows, :] = _softmax_pv(s2, v_ref[:w, :])


def _pool_kernel(u_ref, w_ref, scale_ref, o_ref):
    seq = u_ref.shape[0]
    g = pl.program_id(1)
    t_idx = lax.broadcasted_iota(jnp.int32, (seq, 1), 0)
    for gi, window in enumerate(POOL_WINDOWS):
        @pl.when(g == gi)
        def _(window=window):
            u = u_ref[...]
            win_sum = u
            span = 1
            while span < window:
                win_sum = win_sum + jnp.where(t_idx >= span, pltpu.roll(win_sum, span, axis=0), 0.0)
                span *= 2
            count = jnp.minimum(t_idx + 1, window).astype(F32)
            d = win_sum / count - u
            o_ref[...] = _dot(d.astype(BF16), w_ref[...]) * scale_ref[...]


def _multiscale_pool(u, w_pool, pool_scale, layer, *, batch, seq):
    _, n_groups, gd, _ = w_pool.shape
    assert all(w & (w - 1) == 0 for w in POOL_WINDOWS) and n_groups == len(POOL_WINDOWS)
    blk = 2 * _nbytes((seq, gd), F32) + _nbytes((gd, gd), BF16)
    return pl.pallas_call(
        _pool_kernel, grid=(batch, n_groups),
        in_specs=[pl.BlockSpec((seq, gd), lambda b, g: (b, g)),
                  pl.BlockSpec((None, None, gd, gd), lambda b, g: (layer, g, 0, 0)),
                  pl.BlockSpec((1, gd), lambda b, g: (0, g))],
        out_specs=pl.BlockSpec((seq, gd), lambda b, g: (b, g)),
        out_shape=jax.ShapeDtypeStruct(u.shape, F32),
        compiler_params=_params(blk, temp_bytes=6 * _nbytes((seq, gd), F32)),
        name="multiscale_pool",
    )(u, w_pool, pool_scale.reshape(1, n_groups * gd))


def kernel(x, ln_mix_pre, w_in, w_pool, pool_scale, mix_out_norm, w_out, ln_mix_post,
           ln_ffn_pre, w_gate, w_up, w_down, ln_ffn_post):
    batch, seq, d_model = x.shape
    depth = w_in.shape[0]
    group = w_out.shape[1] // N_MIXERS
    n_heads = group // HEAD_DIM
    n_blocks = seq // MOBA_BLOCK
    assert seq % ATTN_TILE == 0 and ATTN_TILE == MOBA_BLOCK and ATTN_TILE & (ATTN_TILE - 1) == 0
    assert MOBA_TOPK < n_blocks <= BF16_SUBLANES
    assert w_in.shape[2] == 10 * group

    w_pool, w_out = w_pool.astype(BF16), w_out.astype(BF16)
    log2cnt = jnp.asarray(_dilated_log2_counts(seq))
    attn = functools.partial(_attn_call, batch=batch, seq=seq, n_heads=n_heads)

    x = x.reshape(batch * seq, d_model)
    h = _rmsnorm_cast(x, ln_mix_pre[0])
    for l in range(depth):
        qkv, u = _in_proj(h, w_in, l, qkv_width=9 * group, tm=1024, tn=group // 2)
        y_a = attn(_moba_kernel, qkv, mixer_slot=0, name="moba")
        y_b = attn(_stickbreak_kernel, qkv, mixer_slot=1, name="stickbreak")
        y_c = _multiscale_pool(u, w_pool, pool_scale[l], l, batch=batch, seq=seq)
        y_d = attn(_dilated_kernel, qkv, mixer_slot=2, name="dilated", extra_inputs=(log2cnt,))
        m = _out_proj((y_a, y_b, y_c, y_d), mix_out_norm[l], w_out, l)
        x, h = _residual_norm(x, m, ln_mix_post[l], ln_ffn_pre[l])
        act, w_down_bf16 = _swiglu_up(h, w_gate, w_up, w_down, l)
        f = _matmul(act, w_down_bf16[None], 0, out_dtype=F32, tm=512, tn=512, name="ffn_down")
        last = l + 1 == depth
        x, h = _residual_norm(x, f, ln_ffn_post[l], ln_mix_pre[0 if last else l + 1],
                              emit_next=not last)
    return x.reshape(batch, seq, d_model)
```

```python
import functools
import math

import numpy as np
import jax
import jax.numpy as jnp
from jax import lax
from jax.experimental import pallas as pl
from jax.experimental.pallas import tpu as pltpu

HEAD_DIM = 128
N_MIXERS = 4
MOBA_BLOCK = 256
MOBA_TOPK = 3
POOL_WINDOWS = (2, 4, 8, 16)
DILATED_PATTERNS = ((128, 1), (512, 4), (2048, 16))
RMS_EPS = 1e-6
NEG_INF = -1e30

ATTN_TILE = 256
BF16_SUBLANES = 16
LANES = 128
V7X_VMEM_BYTES = 64 * 1024 * 1024
VMEM_LIMIT_CAP = V7X_VMEM_BYTES - 8 * 1024 * 1024
LOG2_E = math.log2(math.e)
UNDERFLOW_LOG2 = -160.0
SCORE_SCALE = HEAD_DIM ** -0.5

F32 = jnp.float32
BF16 = jnp.bfloat16


def _nbytes(shape, dtype):
    return int(np.prod(shape)) * jnp.dtype(dtype).itemsize


def _params(block_bytes, scratch_bytes=0, temp_bytes=0, n_axes=2):
    limit = 2 * block_bytes + scratch_bytes + temp_bytes
    limit = min(max(limit, 16 * 1024 * 1024), VMEM_LIMIT_CAP)
    return pltpu.CompilerParams(
        dimension_semantics=("arbitrary",) * n_axes, vmem_limit_bytes=int(limit))


def _dot(a, b):
    return jnp.dot(a, b, preferred_element_type=F32)


def _dot_nt(a, b):
    return lax.dot_general(a, b, (((1,), (1,)), ((), ())), preferred_element_type=F32)


def _dot_tn(a, b):
    return lax.dot_general(a, b, (((0,), (0,)), ((), ())), preferred_element_type=F32)


def _rms_scale(x):
    return lax.rsqrt(jnp.mean(x * x, axis=-1, keepdims=True) + RMS_EPS)


def _rmsnorm_cast_kernel(x_ref, g_ref, o_ref):
    x = x_ref[...]
    o_ref[...] = (x * _rms_scale(x) * g_ref[...]).astype(o_ref.dtype)


def _rmsnorm_cast(x, g, *, tm=512):
    m, d = x.shape
    blk = _nbytes((tm, d), F32) + _nbytes((tm, d), BF16)
    return pl.pallas_call(
        _rmsnorm_cast_kernel, grid=(m // tm,),
        in_specs=[pl.BlockSpec((tm, d), lambda i: (i, 0)),
                  pl.BlockSpec((1, d), lambda i: (0, 0))],
        out_specs=pl.BlockSpec((tm, d), lambda i: (i, 0)),
        out_shape=jax.ShapeDtypeStruct((m, d), BF16),
        compiler_params=_params(blk, temp_bytes=_nbytes((tm, d), F32), n_axes=1),
        name="rmsnorm_cast",
    )(x, g.reshape(1, d))


def _residual_norm_kernel(x_ref, m_ref, g_post_ref, g_next_ref, xo_ref, ho_ref=None):
    m = m_ref[...]
    x = x_ref[...] + m * _rms_scale(m) * g_post_ref[...]
    xo_ref[...] = x
    if ho_ref is not None:
        ho_ref[...] = (x * _rms_scale(x) * g_next_ref[...]).astype(ho_ref.dtype)


def _residual_norm(x, m, g_post, g_next, *, emit_next=True, tm=256):
    rows, d = x.shape
    blk = 3 * _nbytes((tm, d), F32) + (_nbytes((tm, d), BF16) if emit_next else 0)
    row_spec = pl.BlockSpec((tm, d), lambda i: (i, 0))
    gain_spec = pl.BlockSpec((1, d), lambda i: (0, 0))
    out_shape = [jax.ShapeDtypeStruct((rows, d), F32)]
    if emit_next:
        out_shape.append(jax.ShapeDtypeStruct((rows, d), BF16))
    outs = pl.pallas_call(
        _residual_norm_kernel, grid=(rows // tm,),
        in_specs=[row_spec, row_spec, gain_spec, gain_spec],
        out_specs=[row_spec] * len(out_shape),
        out_shape=out_shape,
        compiler_params=_params(blk, temp_bytes=2 * _nbytes((tm, d), F32), n_axes=1),
        name="residual_norm",
    )(x, m, g_post.reshape(1, d), g_next.reshape(1, d))
    return (outs[0], outs[1]) if emit_next else (outs[0], None)


def _matmul_kernel(a_ref, w_ref, o_ref):
    o_ref[...] = _dot(a_ref[...], w_ref[...].astype(BF16)).astype(o_ref.dtype)


def _matmul(a, w, layer, *, out_dtype, tm, tn, col_block_offset=0, n_out=None, name="matmul"):
    m, k = a.shape
    n_out = w.shape[2] if n_out is None else n_out
    blk = _nbytes((tm, k), a.dtype) + _nbytes((k, tn), w.dtype) + _nbytes((tm, tn), out_dtype)
    blk += _nbytes((k, tn), BF16) // 2 if w.dtype != BF16 else 0
    return pl.pallas_call(
        _matmul_kernel, grid=(m // tm, n_out // tn),
        in_specs=[pl.BlockSpec((tm, k), lambda i, j: (i, 0)),
                  pl.BlockSpec((None, k, tn), lambda i, j: (layer, 0, j + col_block_offset))],
        out_specs=pl.BlockSpec((tm, tn), lambda i, j: (i, j)),
        out_shape=jax.ShapeDtypeStruct((m, n_out), out_dtype),
        compiler_params=_params(blk, temp_bytes=2 * _nbytes((tm, tn), F32)),
        name=name,
    )(a, w)


def _in_proj_kernel(h_ref, w_ref, qkv_ref, u_ref, *, qkv_blocks):
    j = pl.program_id(1)
    y = _dot(h_ref[...], w_ref[...].astype(BF16))

    @pl.when(j < qkv_blocks)
    def _():
        qkv_ref[...] = y.astype(qkv_ref.dtype)

    @pl.when(j >= qkv_blocks)
    def _():
        u_ref[...] = y


def _in_proj(h, w_in, layer, *, qkv_width, tm, tn):
    m, k = h.shape
    n = w_in.shape[2]
    qkv_blocks = qkv_width // tn
    blk = (_nbytes((tm, k), BF16) + _nbytes((k, tn), w_in.dtype) + _nbytes((k, tn), BF16) // 2
           + _nbytes((tm, tn), BF16) + _nbytes((tm, tn), F32))
    return pl.pallas_call(
        functools.partial(_in_proj_kernel, qkv_blocks=qkv_blocks), grid=(m // tm, n // tn),
        in_specs=[pl.BlockSpec((tm, k), lambda i, j: (i, 0)),
                  pl.BlockSpec((None, k, tn), lambda i, j: (layer, 0, j))],
        out_specs=[pl.BlockSpec((tm, tn), lambda i, j: (i, jnp.minimum(j, qkv_blocks - 1))),
                   pl.BlockSpec((tm, tn), lambda i, j: (i, jnp.maximum(j - qkv_blocks, 0)))],
        out_shape=[jax.ShapeDtypeStruct((m, qkv_width), BF16),
                   jax.ShapeDtypeStruct((m, n - qkv_width), F32)],
        compiler_params=_params(blk, temp_bytes=2 * _nbytes((tm, tn), F32)),
        name="in_proj",
    )(h, w_in)


def _out_proj_kernel(ya_ref, yb_ref, yc_ref, yd_ref, g_ref, w_ref, o_ref, hm_ref):
    @pl.when(pl.program_id(1) == 0)
    def _():
        gw = ya_ref.shape[1]
        for i, y_ref in enumerate((ya_ref, yb_ref, yc_ref, yd_ref)):
            y = y_ref[...].astype(F32)
            cols = slice(i * gw, (i + 1) * gw)
            hm_ref[:, cols] = (y * _rms_scale(y) * g_ref[:, cols]).astype(hm_ref.dtype)

    o_ref[...] = _dot(hm_ref[...], w_ref[...])


def _out_proj(ys, g, w_out, layer, *, tm=1024, tn=512):
    rows, gw = ys[0].shape
    d = gw * len(ys)
    n = w_out.shape[2]
    blk = (len(ys) * _nbytes((tm, gw), ys[0].dtype) + _nbytes((d, tn), w_out.dtype)
           + _nbytes((tm, tn), F32))
    y_spec = pl.BlockSpec((tm, gw), lambda i, j: (i, 0))
    return pl.pallas_call(
        _out_proj_kernel, grid=(rows // tm, n // tn),
        in_specs=[y_spec] * len(ys) + [pl.BlockSpec((1, d), lambda i, j: (0, 0)),
                                       pl.BlockSpec((None, d, tn), lambda i, j: (layer, 0, j))],
        out_specs=pl.BlockSpec((tm, tn), lambda i, j: (i, j)),
        out_shape=jax.ShapeDtypeStruct((rows, n), F32),
        scratch_shapes=[pltpu.VMEM((tm, d), BF16)],
        compiler_params=_params(blk, scratch_bytes=_nbytes((tm, d), BF16),
                                temp_bytes=2 * _nbytes((tm, tn), F32)),
        name="out_proj",
    )(*ys, g.reshape(1, d), w_out)


def _swiglu_up_kernel(h_ref, wg_ref, wu_ref, wd_ref, o_ref, wd_bf16_ref):
    h = h_ref[...]
    g = _dot(h, wg_ref[...].astype(BF16))
    u = _dot(h, wu_ref[...].astype(BF16))
    o_ref[...] = (g / (1.0 + jnp.exp(-g)) * u).astype(o_ref.dtype)
    wd_bf16_ref[...] = wd_ref[...].astype(wd_bf16_ref.dtype)


def _swiglu_up(h, wg, wu, wd, layer, *, tm=1024, tn=256):
    m, k = h.shape
    n = wg.shape[2]
    grid = (m // tm, n // tn)
    d_ff, d_out = wd.shape[1:]
    slab = d_ff // (grid[0] * grid[1])
    assert slab * grid[0] * grid[1] == d_ff and slab % BF16_SUBLANES == 0
    blk = (_nbytes((tm, k), BF16) + 2 * _nbytes((k, tn), wg.dtype) + _nbytes((k, tn), BF16)
           + _nbytes((tm, tn), BF16) + _nbytes((slab, d_out), F32) + _nbytes((slab, d_out), BF16))
    w_spec = pl.BlockSpec((None, k, tn), lambda i, j: (layer, 0, j))
    return pl.pallas_call(
        _swiglu_up_kernel, grid=grid,
        in_specs=[pl.BlockSpec((tm, k), lambda i, j: (i, 0)), w_spec, w_spec,
                  pl.BlockSpec((None, slab, d_out), lambda i, j: (layer, i * grid[1] + j, 0))],
        out_specs=[pl.BlockSpec((tm, tn), lambda i, j: (i, j)),
                   pl.BlockSpec((slab, d_out), lambda i, j: (i * grid[1] + j, 0))],
        out_shape=[jax.ShapeDtypeStruct((m, n), BF16), jax.ShapeDtypeStruct((d_ff, d_out), BF16)],
        compiler_params=_params(blk, temp_bytes=3 * _nbytes((tm, tn), F32)),
        name="swiglu_up",
    )(h, wg, wu, wd)


def _attn_call(kernel, proj, *, batch, seq, n_heads, mixer_slot, extra_inputs=(), scratch=(),
               name):
    base = 3 * mixer_slot * n_heads
    head_blk = (seq, HEAD_DIM)
    in_specs = [pl.BlockSpec(head_blk, lambda b, h, off=base + i * n_heads: (b, off + h))
                for i in range(3)]
    in_specs += [pl.BlockSpec(x.shape, lambda b, h, nd=x.ndim: (0,) * nd) for x in extra_inputs]
    blk = 4 * _nbytes(head_blk, BF16) + sum(_nbytes(x.shape, x.dtype) for x in extra_inputs)
    return pl.pallas_call(
        kernel, grid=(batch, n_heads),
        in_specs=in_specs,
        out_specs=pl.BlockSpec(head_blk, lambda b, h: (b, h)),
        out_shape=jax.ShapeDtypeStruct((batch * seq, n_heads * HEAD_DIM), BF16),
        scratch_shapes=list(scratch),
        compiler_params=_params(blk, temp_bytes=8 * _nbytes((2 * seq, ATTN_TILE), F32)),
        name=name,
    )(proj, proj, proj, *extra_inputs)


def _causal_tile_mask():
    row = lax.broadcasted_iota(jnp.int32, (ATTN_TILE, ATTN_TILE), 0)
    col = lax.broadcasted_iota(jnp.int32, (ATTN_TILE, ATTN_TILE), 1)
    return row, col


def _softmax_pv(s2, v):
    m = jnp.max(s2, axis=1, keepdims=True)
    p = jnp.exp2(s2 - m)
    l = jnp.sum(p, axis=1, keepdims=True)
    return _dot(p.astype(BF16), v) / l


def _moba_kernel(q_ref, k_ref, v_ref, o_ref):
    t = ATTN_TILE
    seq = k_ref.shape[0]
    n_blocks = seq // t
    rows_pad = BF16_SUBLANES
    q = q_ref[...]
    k = k_ref[...]

    kmean = jnp.sum(k.astype(F32).reshape(n_blocks, t, HEAD_DIM), axis=1) * (1.0 / t)
    kmean = jnp.concatenate([kmean, jnp.zeros((rows_pad - n_blocks, HEAD_DIM), F32)], axis=0)
    kmean_hi = kmean.astype(BF16)
    kmean_lo = (kmean - kmean_hi.astype(F32)).astype(BF16)
    gate = _dot_nt(kmean_hi, q) + _dot_nt(kmean_lo, q)

    blk_id = lax.broadcasted_iota(jnp.int32, gate.shape, 0)
    q_blk = lax.shift_right_logical(lax.broadcasted_iota(jnp.int32, gate.shape, 1),
                                    int(math.log2(t)))
    rank = jnp.zeros(gate.shape, jnp.int32)
    for i in range(n_blocks):
        gi = gate[i:i + 1, :]
        beats = jnp.where(gi > gate, 1, jnp.where((gi == gate) & (i < blk_id), 1, 0))
        rank = rank + jnp.where(i < q_blk, beats, 0)
    keep = (blk_id == q_blk) | ((blk_id < q_blk) & (rank < MOBA_TOPK))
    bias = jnp.where(keep, 0.0, NEG_INF).astype(BF16)

    lane_id = lax.broadcasted_iota(jnp.int32, (rows_pad, LANES), 1)
    eye = jnp.where(lane_id == lax.broadcasted_iota(jnp.int32, (rows_pad, LANES), 0), 1.0, 0.0)
    bias_cols = _dot_tn(bias, eye.astype(BF16)).astype(BF16)
    key_blk = lax.shift_right_logical(lax.broadcasted_iota(jnp.int32, (seq, LANES), 0),
                                      int(math.log2(t)))
    one_hot = jnp.where(key_blk == lax.broadcasted_iota(jnp.int32, (seq, LANES), 1), 1.0, 0.0)
    q_aug = jnp.concatenate([q, bias_cols], axis=1)
    k_aug = jnp.concatenate([k, one_hot.astype(BF16)], axis=1)

    row, col = _causal_tile_mask()
    causal = col <= row
    to_log2 = SCORE_SCALE * LOG2_E
    tiles = [slice(qi * t, (qi + 1) * t) for qi in range(n_blocks)]
    s_own = [jnp.where(causal, _dot_nt(q_aug[r], k_aug[r]), NEG_INF) * to_log2 for r in tiles]
    s_past = [_dot_nt(q_aug[r], k_aug[:r.start]) * to_log2 for r in tiles[1:]]
    m = [jnp.max(x, axis=1, keepdims=True) for x in s_own]
    m = m[:1] + [jnp.maximum(mo, jnp.max(x, axis=1, keepdims=True)) for mo, x in zip(m[1:], s_past)]
    p_own = [jnp.exp2(x - mx) for x, mx in zip(s_own, m)]
    p_past = [jnp.exp2(x - mx) for x, mx in zip(s_past, m[1:])]
    l = [jnp.sum(x, axis=1, keepdims=True) for x in p_own]
    l = l[:1] + [lo + jnp.sum(x, axis=1, keepdims=True) for lo, x in zip(l[1:], p_past)]
    for qi, r in enumerate(tiles):
        pv = _dot(p_own[qi].astype(BF16), v_ref[r, :])
        if qi > 0:
            pv = pv + _dot(p_past[qi - 1].astype(BF16), v_ref[:r.start, :])
        o_ref[r, :] = (pv / l[qi]).astype(o_ref.dtype)


def _log2_keep(neg_z):
    neg_abs = lax.bitcast_convert_type(
        lax.bitcast_convert_type(neg_z, jnp.uint32) | jnp.uint32(0x80000000), F32)
    return jnp.minimum(neg_z, 0.0) - jnp.log2(1.0 + jnp.exp2(neg_abs))


def _split_hi_lo(x):
    hi = lax.bitcast_convert_type(
        lax.bitcast_convert_type(x, jnp.uint32) & jnp.uint32(0xFFFF0000), F32)
    return jnp.concatenate([hi.astype(BF16), (x - hi).astype(BF16)], axis=1)


def _stickbreak_kernel(q_ref, k_ref, v_ref, o_ref, acc_ref):
    t = ATTN_TILE
    seq = k_ref.shape[0]
    n_blocks = seq // t
    q = q_ref[...]
    k = k_ref[...]
    to_neg_log2 = -SCORE_SCALE * LOG2_E
    row, col = _causal_tile_mask()
    later = jnp.where(row > col, 1.0, 0.0).astype(BF16)
    later2 = jnp.concatenate([later, later], axis=0)

    def tile(x, i):
        return x[i * t:(i + 1) * t]

    neg_z = jnp.concatenate(
        [_dot_nt(tile(q, i), tile(k, i)) for i in range(n_blocks)]
        + [_dot_nt(tile(q, i), tile(k, i - 1)) for i in range(1, n_blocks)], axis=0) * to_neg_log2
    row_in_tile = lax.broadcasted_iota(jnp.int32, (seq, t), 0) & (t - 1)
    strict = lax.broadcasted_iota(jnp.int32, (seq, t), 1) < row_in_tile
    log_keep = _log2_keep(neg_z)
    log_keep = jnp.concatenate([jnp.where(strict, log_keep[:seq], 0.0), log_keep[seq:]], axis=0)
    after = _dot(_split_hi_lo(log_keep), later2)
    block_sum = after[:, 0:1] + log_keep[:, 0:1]
    log_w = log_keep - neg_z + after
    a_own = jnp.where(strict, jnp.exp2(log_w[:seq]), 0.0)
    a_prev = jnp.exp2(log_w[seq:] + block_sum[t:seq])
    carry = block_sum[t:seq] + block_sum[seq:]

    acc_ref[:t, :] = _dot(tile(a_own, 0).astype(BF16), v_ref[:t, :])
    for i in range(1, n_blocks):
        a = jnp.concatenate([tile(a_prev, i - 1), tile(a_own, i)], axis=1).astype(BF16)
        acc_ref[i * t:(i + 1) * t, :] = _dot(a, v_ref[(i - 1) * t:(i + 1) * t, :])

    far_needed = [jnp.max(tile(carry, i - 1)) >= UNDERFLOW_LOG2 for i in range(2, n_blocks)]
    for i in range(2, n_blocks):
        @pl.when(far_needed[i - 2])
        def _(i=i):
            n = i - 1
            neg_z = _dot_nt(tile(q, i), k[:n * t]) * to_neg_log2
            log_keep = _log2_keep(neg_z)
            stacked = jnp.concatenate([log_keep[:, j * t:(j + 1) * t] for j in range(n)], axis=0)
            after = _dot(_split_hi_lo(stacked), later2)
            c = tile(carry, i - 1)
            weights = [None] * n
            for j in range(n - 1, -1, -1):
                blk = slice(j * t, (j + 1) * t)
                weights[j] = jnp.exp2(log_keep[:, blk] - neg_z[:, blk] + after[blk] + c)
                c = c + after[blk][:, 0:1] + log_keep[:, j * t:j * t + 1]
            a = weights[0] if n == 1 else jnp.concatenate(weights, axis=1)
            acc_ref[i * t:(i + 1) * t, :] += _dot(a.astype(BF16), v_ref[:n * t, :])

    o_ref[...] = acc_ref[...].astype(o_ref.dtype)


def _dilated_log2_counts(seq):
    t = ATTN_TILE
    delta = np.arange(t)[:, None] - np.arange(seq)[None, :] + (seq - t)
    counts = np.zeros(delta.shape, np.float64)
    for window, dil in DILATED_PATTERNS:
        counts += (delta >= 0) & (delta % dil == 0) & (delta <= window)
    return np.where(counts > 0, np.log2(np.maximum(counts, 1.0)), NEG_INF).astype(np.float32)


def _dilated_kernel(q_ref, k_ref, v_ref, log2cnt_ref, o_ref):
    t = ATTN_TILE
    seq = k_ref.shape[0]
    n_blocks = seq // t
    q = q_ref[...]
    k = k_ref[...]
    widths = [(qi + 1) * t for qi in range(n_blocks)]
    s2 = [_dot_nt(q[w - t:w], k[:w]) * (SCORE_SCALE * LOG2_E) + log2cnt_ref[:, seq - w:]
          for w in widths]
    m = [jnp.max(x, axis=1, keepdims=True) for x in s2]
    p = [jnp.exp2(x - mx) for x, mx in zip(s2, m)]
    l = [jnp.sum(x, axis=1, keepdims=True) for x in p]
    for w, pw, lw in zip(widths, p, l):
        o_ref[w - t:w, :] = (_dot(pw.astype(BF16), v_ref[:w, :]) / lw).astype(o_ref.dtype)


def _pool_kernel(u_ref, w_ref, scale_ref, o_ref):
    seq = u_ref.shape[0]
    g = pl.program_id(1)
    t_idx = lax.broadcasted_iota(jnp.int32, (seq, 1), 0)
    for gi, window in enumerate(POOL_WINDOWS):
        @pl.when(g == gi)
        def _(window=window):
            u = u_ref[...]
            win_sum = u
            span = 1
            while span < window:
                win_sum = win_sum + jnp.where(t_idx >= span, pltpu.roll(win_sum, span, axis=0), 0.0)
                span *= 2
            count = jnp.minimum(t_idx + 1, window).astype(F32)
            d = win_sum / count - u
            o_ref[...] = (_dot(d.astype(BF16), w_ref[...]) * scale_ref[...]).astype(o_ref.dtype)


def _multiscale_pool(u, w_pool, pool_scale, layer, *, batch, seq):
    _, n_groups, gd, _ = w_pool.shape
    assert all(w & (w - 1) == 0 for w in POOL_WINDOWS) and n_groups == len(POOL_WINDOWS)
    blk = _nbytes((seq, gd), F32) + _nbytes((seq, gd), BF16) + _nbytes((gd, gd), BF16)
    return pl.pallas_call(
        _pool_kernel, grid=(batch, n_groups),
        in_specs=[pl.BlockSpec((seq, gd), lambda b, g: (b, g)),
                  pl.BlockSpec((None, None, gd, gd), lambda b, g: (layer, g, 0, 0)),
                  pl.BlockSpec((1, gd), lambda b, g: (0, g))],
        out_specs=pl.BlockSpec((seq, gd), lambda b, g: (b, g)),
        out_shape=jax.ShapeDtypeStruct(u.shape, BF16),
        compiler_params=_params(blk, temp_bytes=6 * _nbytes((seq, gd), F32)),
        name="multiscale_pool",
    )(u, w_pool, pool_scale.reshape(1, n_groups * gd))


def kernel(x, ln_mix_pre, w_in, w_pool, pool_scale, mix_out_norm, w_out, ln_mix_post,
           ln_ffn_pre, w_gate, w_up, w_down, ln_ffn_post):
    batch, seq, d_model = x.shape
    depth = w_in.shape[0]
    group = w_out.shape[1] // N_MIXERS
    n_heads = group // HEAD_DIM
    n_blocks = seq // MOBA_BLOCK
    assert seq % ATTN_TILE == 0 and ATTN_TILE == MOBA_BLOCK and ATTN_TILE & (ATTN_TILE - 1) == 0
    assert MOBA_TOPK < n_blocks <= BF16_SUBLANES
    assert w_in.shape[2] == 10 * group

    w_pool, w_out = w_pool.astype(BF16), w_out.astype(BF16)
    log2cnt = jnp.asarray(_dilated_log2_counts(seq))
    attn = functools.partial(_attn_call, batch=batch, seq=seq, n_heads=n_heads)

    x = x.reshape(batch * seq, d_model)
    h = _rmsnorm_cast(x, ln_mix_pre[0])
    for l in range(depth):
        qkv, u = _in_proj(h, w_in, l, qkv_width=9 * group, tm=1024, tn=group // 2)
        y_a = attn(_moba_kernel, qkv, mixer_slot=0, name="moba")
        y_b = attn(_stickbreak_kernel, qkv, mixer_slot=1, name="stickbreak",
                   scratch=[pltpu.VMEM((seq, HEAD_DIM), F32)])
        y_c = _multiscale_pool(u, w_pool, pool_scale[l], l, batch=batch, seq=seq)
        y_d = attn(_dilated_kernel, qkv, mixer_slot=2, name="dilated", extra_inputs=(log2cnt,))
        m = _out_proj((y_a, y_b, y_c, y_d), mix_out_norm[l], w_out, l)
        x, h = _residual_norm(x, m, ln_mix_post[l], ln_ffn_pre[l])
        act, w_down_bf16 = _swiglu_up(h, w_gate, w_up, w_down, l)
        f = _matmul(act, w_down_bf16[None], 0, out_dtype=F32, tm=512, tn=512, name="ffn_down")
        last = l + 1 == depth
        x, h = _residual_norm(x, f, ln_ffn_post[l], ln_mix_pre[0 if last else l + 1],
                              emit_next=not last)
    return x.reshape(batch, seq, d_model)
```

```python
import math

import numpy as np
import jax
import jax.numpy as jnp
from jax import lax
from jax.experimental import pallas as pl
from jax.experimental.pallas import tpu as pltpu

HEAD_DIM = 128
N_MIXERS = 4
MOBA_BLOCK = 256
MOBA_TOPK = 3
POOL_WINDOWS = (2, 4, 8, 16)
DILATED_PATTERNS = ((128, 1), (512, 4), (2048, 16))
RMS_EPS = 1e-6
NEG_INF = -1e30

ATTN_TILE = 256
BF16_SUBLANES = 16
LANES = 128
V7X_VMEM_BYTES = 64 * 1024 * 1024
VMEM_LIMIT_CAP = V7X_VMEM_BYTES - 8 * 1024 * 1024
LOG2_E = math.log2(math.e)
UNDERFLOW_LOG2 = -160.0
SCORE_SCALE = HEAD_DIM ** -0.5

F32 = jnp.float32
BF16 = jnp.bfloat16
BRANCH_DTYPE = BF16


def _nbytes(shape, dtype):
    return int(np.prod(shape)) * jnp.dtype(dtype).itemsize


def _params(block_bytes, scratch_bytes=0, temp_bytes=0, n_axes=2):
    limit = 2 * block_bytes + scratch_bytes + temp_bytes
    limit = min(max(limit, 16 * 1024 * 1024), VMEM_LIMIT_CAP)
    return pltpu.CompilerParams(
        dimension_semantics=("arbitrary",) * n_axes, vmem_limit_bytes=int(limit))


def _dot(a, b):
    return jnp.dot(a, b, preferred_element_type=F32)


def _dot_nt(a, b):
    return lax.dot_general(a, b, (((1,), (1,)), ((), ())), preferred_element_type=F32)


def _dot_tn(a, b):
    return lax.dot_general(a, b, (((0,), (0,)), ((), ())), preferred_element_type=F32)


def _rms_scale(x):
    return lax.rsqrt(jnp.mean(x * x, axis=-1, keepdims=True) + RMS_EPS)


def _rmsnorm_cast_kernel(x_ref, g_ref, o_ref):
    x = x_ref[...]
    o_ref[...] = (x * _rms_scale(x) * g_ref[...]).astype(o_ref.dtype)


def _rmsnorm_cast(x, g, *, tm=512):
    m, d = x.shape
    blk = _nbytes((tm, d), F32) + _nbytes((tm, d), BF16)
    return pl.pallas_call(
        _rmsnorm_cast_kernel, grid=(m // tm,),
        in_specs=[pl.BlockSpec((tm, d), lambda i: (i, 0)),
                  pl.BlockSpec((1, d), lambda i: (0, 0))],
        out_specs=pl.BlockSpec((tm, d), lambda i: (i, 0)),
        out_shape=jax.ShapeDtypeStruct((m, d), BF16),
        compiler_params=_params(blk, temp_bytes=_nbytes((tm, d), F32), n_axes=1),
        name="rmsnorm_cast",
    )(x, g.reshape(1, d))


def _residual_norm_kernel(x_ref, m_ref, g_post_ref, g_next_ref, xo_ref, ho_ref=None):
    m = m_ref[...].astype(F32)
    x = x_ref[...] + m * _rms_scale(m) * g_post_ref[...]
    xo_ref[...] = x
    if ho_ref is not None:
        ho_ref[...] = (x * _rms_scale(x) * g_next_ref[...]).astype(ho_ref.dtype)


def _residual_norm(x, m, g_post, g_next, *, emit_next=True, tm=256):
    rows, d = x.shape
    blk = (2 * _nbytes((tm, d), F32) + _nbytes((tm, d), m.dtype)
           + (_nbytes((tm, d), BF16) if emit_next else 0))
    row_spec = pl.BlockSpec((tm, d), lambda i: (i, 0))
    gain_spec = pl.BlockSpec((1, d), lambda i: (0, 0))
    out_shape = [jax.ShapeDtypeStruct((rows, d), F32)]
    if emit_next:
        out_shape.append(jax.ShapeDtypeStruct((rows, d), BF16))
    outs = pl.pallas_call(
        _residual_norm_kernel, grid=(rows // tm,),
        in_specs=[row_spec, row_spec, gain_spec, gain_spec],
        out_specs=[row_spec] * len(out_shape),
        out_shape=out_shape,
        compiler_params=_params(blk, temp_bytes=2 * _nbytes((tm, d), F32), n_axes=1),
        name="residual_norm",
    )(x, m, g_post.reshape(1, d), g_next.reshape(1, d))
    return (outs[0], outs[1]) if emit_next else (outs[0], None)


def _matmul_kernel(a_ref, w_ref, o_ref):
    o_ref[...] = _dot(a_ref[...], w_ref[...].astype(BF16)).astype(o_ref.dtype)


def _matmul(a, w, layer, *, out_dtype, tm, tn, col_block_offset=0, n_out=None, name="matmul"):
    m, k = a.shape
    n_out = w.shape[2] if n_out is None else n_out
    blk = _nbytes((tm, k), a.dtype) + _nbytes((k, tn), w.dtype) + _nbytes((tm, tn), out_dtype)
    blk += _nbytes((k, tn), BF16) // 2 if w.dtype != BF16 else 0
    return pl.pallas_call(
        _matmul_kernel, grid=(m // tm, n_out // tn),
        in_specs=[pl.BlockSpec((tm, k), lambda i, j: (i, 0)),
                  pl.BlockSpec((None, k, tn), lambda i, j: (layer, 0, j + col_block_offset))],
        out_specs=pl.BlockSpec((tm, tn), lambda i, j: (i, j)),
        out_shape=jax.ShapeDtypeStruct((m, n_out), out_dtype),
        compiler_params=_params(blk, temp_bytes=2 * _nbytes((tm, tn), F32)),
        name=name,
    )(a, w)


def _in_proj_kernel(h_ref, w_ref, qkv_ref, u_ref):
    y = _dot(h_ref[...], w_ref[...].astype(BF16))
    qkv_ref[...] = y.astype(qkv_ref.dtype)
    u_ref[...] = y


def _in_proj(h, w_in, layer, *, qkv_width, tm, tn):
    m, k = h.shape
    n = w_in.shape[2]
    qkv_blocks = qkv_width // tn
    u_blocks = (n - qkv_width) // tn
    blk = (_nbytes((tm, k), BF16) + _nbytes((k, tn), w_in.dtype) + _nbytes((k, tn), BF16) // 2
           + _nbytes((tm, tn), BF16) + _nbytes((tm, tn), F32))
    return pl.pallas_call(
        _in_proj_kernel, grid=(m // tm, n // tn),
        in_specs=[pl.BlockSpec((tm, k), lambda i, j: (i, 0)),
                  pl.BlockSpec((None, k, tn), lambda i, j: (layer, 0, j))],
        out_specs=[pl.BlockSpec((tm, tn), lambda i, j: (i, jnp.minimum(j, qkv_blocks))),
                   pl.BlockSpec((tm, tn), lambda i, j: (
                       i, jnp.where(j < qkv_blocks, u_blocks, j - qkv_blocks)))],
        out_shape=[jax.ShapeDtypeStruct((m, qkv_width + tn), BF16),
                   jax.ShapeDtypeStruct((m, n - qkv_width + tn), F32)],
        compiler_params=_params(blk, temp_bytes=2 * _nbytes((tm, tn), F32)),
        name="in_proj",
    )(h, w_in)


def _out_proj_kernel(ya_ref, yb_ref, yc_ref, yd_ref, g_ref, w_ref, o_ref, hm_ref):
    @pl.when(pl.program_id(1) == 0)
    def _():
        gw = ya_ref.shape[1]
        for i, y_ref in enumerate((ya_ref, yb_ref, yc_ref, yd_ref)):
            y = y_ref[...].astype(F32)
            cols = slice(i * gw, (i + 1) * gw)
            hm_ref[:, cols] = (y * _rms_scale(y) * g_ref[:, cols]).astype(hm_ref.dtype)

    o_ref[...] = _dot(hm_ref[...], w_ref[...]).astype(o_ref.dtype)


def _out_proj(ys, g, w_out, layer, *, tm=1024, tn=512):
    rows, gw = ys[0].shape
    d = gw * len(ys)
    n = w_out.shape[2]
    blk = (len(ys) * _nbytes((tm, gw), ys[0].dtype) + _nbytes((d, tn), w_out.dtype)
           + _nbytes((tm, tn), BRANCH_DTYPE))
    y_spec = pl.BlockSpec((tm, gw), lambda i, j: (i, 0))
    return pl.pallas_call(
        _out_proj_kernel, grid=(rows // tm, n // tn),
        in_specs=[y_spec] * len(ys) + [pl.BlockSpec((1, d), lambda i, j: (0, 0)),
                                       pl.BlockSpec((None, d, tn), lambda i, j: (layer, 0, j))],
        out_specs=pl.BlockSpec((tm, tn), lambda i, j: (i, j)),
        out_shape=jax.ShapeDtypeStruct((rows, n), BRANCH_DTYPE),
        scratch_shapes=[pltpu.VMEM((tm, d), BF16)],
        compiler_params=_params(blk, scratch_bytes=_nbytes((tm, d), BF16),
                                temp_bytes=2 * _nbytes((tm, tn), F32)),
        name="out_proj",
    )(*ys, g.reshape(1, d), w_out)


def _swiglu_up_kernel(h_ref, wg_ref, wu_ref, wd_ref, o_ref, wd_bf16_ref):
    h = h_ref[...]
    g = _dot(h, wg_ref[...].astype(BF16))
    u = _dot(h, wu_ref[...].astype(BF16))
    o_ref[...] = (g / (1.0 + jnp.exp(-g)) * u).astype(o_ref.dtype)
    wd_bf16_ref[...] = wd_ref[...].astype(wd_bf16_ref.dtype)


def _swiglu_up(h, wg, wu, wd, layer, *, tm=1024, tn=256):
    m, k = h.shape
    n = wg.shape[2]
    grid = (m // tm, n // tn)
    d_ff, d_out = wd.shape[1:]
    slab = d_ff // (grid[0] * grid[1])
    assert slab * grid[0] * grid[1] == d_ff and slab % BF16_SUBLANES == 0
    blk = (_nbytes((tm, k), BF16) + 2 * _nbytes((k, tn), wg.dtype) + _nbytes((k, tn), BF16)
           + _nbytes((tm, tn), BF16) + _nbytes((slab, d_out), F32) + _nbytes((slab, d_out), BF16))
    w_spec = pl.BlockSpec((None, k, tn), lambda i, j: (layer, 0, j))
    return pl.pallas_call(
        _swiglu_up_kernel, grid=grid,
        in_specs=[pl.BlockSpec((tm, k), lambda i, j: (i, 0)), w_spec, w_spec,
                  pl.BlockSpec((None, slab, d_out), lambda i, j: (layer, i * grid[1] + j, 0))],
        out_specs=[pl.BlockSpec((tm, tn), lambda i, j: (i, j)),
                   pl.BlockSpec((slab, d_out), lambda i, j: (i * grid[1] + j, 0))],
        out_shape=[jax.ShapeDtypeStruct((m, n), BF16), jax.ShapeDtypeStruct((d_ff, d_out), BF16)],
        compiler_params=_params(blk, temp_bytes=3 * _nbytes((tm, tn), F32)),
        name="swiglu_up",
    )(h, wg, wu, wd)


def _causal_tile_mask():
    row = lax.broadcasted_iota(jnp.int32, (ATTN_TILE, ATTN_TILE), 0)
    col = lax.broadcasted_iota(jnp.int32, (ATTN_TILE, ATTN_TILE), 1)
    return row, col


def _softmax_pv(s2, v):
    m = jnp.max(s2, axis=1, keepdims=True)
    p = jnp.exp2(s2 - m)
    l = jnp.sum(p, axis=1, keepdims=True)
    return _dot(p.astype(BF16), v) / l


def _moba_kernel(q_ref, k_ref, v_ref, o_ref):
    t = ATTN_TILE
    seq = k_ref.shape[0]
    n_blocks = seq // t
    rows_pad = BF16_SUBLANES
    q = q_ref[...]
    k = k_ref[...]

    kmean = jnp.sum(k.astype(F32).reshape(n_blocks, t, HEAD_DIM), axis=1) * (1.0 / t)
    kmean = jnp.concatenate([kmean, jnp.zeros((rows_pad - n_blocks, HEAD_DIM), F32)], axis=0)
    kmean_hi = kmean.astype(BF16)
    kmean_lo = (kmean - kmean_hi.astype(F32)).astype(BF16)
    gate = _dot_nt(kmean_hi, q) + _dot_nt(kmean_lo, q)

    blk_id = lax.broadcasted_iota(jnp.int32, gate.shape, 0)
    q_blk = lax.shift_right_logical(lax.broadcasted_iota(jnp.int32, gate.shape, 1),
                                    int(math.log2(t)))
    rank = jnp.zeros(gate.shape, jnp.int32)
    for i in range(n_blocks):
        gi = gate[i:i + 1, :]
        beats = jnp.where(gi > gate, 1, jnp.where((gi == gate) & (i < blk_id), 1, 0))
        rank = rank + jnp.where(i < q_blk, beats, 0)
    keep = (blk_id == q_blk) | ((blk_id < q_blk) & (rank < MOBA_TOPK))
    bias = jnp.where(keep, 0.0, NEG_INF).astype(BF16)

    lane_id = lax.broadcasted_iota(jnp.int32, (rows_pad, LANES), 1)
    eye = jnp.where(lane_id == lax.broadcasted_iota(jnp.int32, (rows_pad, LANES), 0), 1.0, 0.0)
    bias_cols = _dot_tn(bias, eye.astype(BF16)).astype(BF16)
    key_blk = lax.shift_right_logical(lax.broadcasted_iota(jnp.int32, (seq, LANES), 0),
                                      int(math.log2(t)))
    one_hot = jnp.where(key_blk == lax.broadcasted_iota(jnp.int32, (seq, LANES), 1), 1.0, 0.0)
    q_aug = jnp.concatenate([q, bias_cols], axis=1)
    k_aug = jnp.concatenate([k, one_hot.astype(BF16)], axis=1)

    row, col = _causal_tile_mask()
    causal = col <= row
    to_log2 = SCORE_SCALE * LOG2_E
    tiles = [slice(qi * t, (qi + 1) * t) for qi in range(n_blocks)]
    s_own = [jnp.where(causal, _dot_nt(q_aug[r], k_aug[r]), NEG_INF) * to_log2 for r in tiles]
    s_past = [_dot_nt(q_aug[r], k_aug[:r.start]) * to_log2 for r in tiles[1:]]
    m = [jnp.max(x, axis=1, keepdims=True) for x in s_own]
    m = m[:1] + [jnp.maximum(mo, jnp.max(x, axis=1, keepdims=True)) for mo, x in zip(m[1:], s_past)]
    p_own = [jnp.exp2(x - mx) for x, mx in zip(s_own, m)]
    p_past = [jnp.exp2(x - mx) for x, mx in zip(s_past, m[1:])]
    l = [jnp.sum(x, axis=1, keepdims=True) for x in p_own]
    l = l[:1] + [lo + jnp.sum(x, axis=1, keepdims=True) for lo, x in zip(l[1:], p_past)]
    for qi, r in enumerate(tiles):
        pv = _dot(p_own[qi].astype(BF16), v_ref[r, :])
        if qi > 0:
            pv = pv + _dot(p_past[qi - 1].astype(BF16), v_ref[:r.start, :])
        o_ref[r, :] = (pv / l[qi]).astype(o_ref.dtype)


def _log2_keep(neg_z):
    neg_abs = lax.bitcast_convert_type(
        lax.bitcast_convert_type(neg_z, jnp.uint32) | jnp.uint32(0x80000000), F32)
    return jnp.minimum(neg_z, 0.0) - jnp.log2(1.0 + jnp.exp2(neg_abs))


def _split_hi_lo(x):
    hi = lax.bitcast_convert_type(
        lax.bitcast_convert_type(x, jnp.uint32) & jnp.uint32(0xFFFF0000), F32)
    return jnp.concatenate([hi.astype(BF16), (x - hi).astype(BF16)], axis=1)


def _stickbreak_kernel(q_ref, k_ref, v_ref, o_ref, acc_ref):
    t = ATTN_TILE
    seq = k_ref.shape[0]
    n_blocks = seq // t
    q = q_ref[...]
    k = k_ref[...]
    to_neg_log2 = -SCORE_SCALE * LOG2_E
    row, col = _causal_tile_mask()
    later = jnp.where(row > col, 1.0, 0.0).astype(BF16)
    later2 = jnp.concatenate([later, later], axis=0)

    def tile(x, i):
        return x[i * t:(i + 1) * t]

    neg_z = jnp.concatenate(
        [_dot_nt(tile(q, i), tile(k, i)) for i in range(n_blocks)]
        + [_dot_nt(tile(q, i), tile(k, i - 1)) for i in range(1, n_blocks)], axis=0) * to_neg_log2
    row_in_tile = lax.broadcasted_iota(jnp.int32, (seq, t), 0) & (t - 1)
    strict = lax.broadcasted_iota(jnp.int32, (seq, t), 1) < row_in_tile
    log_keep = _log2_keep(neg_z)
    log_keep = jnp.concatenate([jnp.where(strict, log_keep[:seq], 0.0), log_keep[seq:]], axis=0)
    after = _dot(_split_hi_lo(log_keep), later2)
    block_sum = after[:, 0:1] + log_keep[:, 0:1]
    log_w = log_keep - neg_z + after
    a_own = jnp.where(strict, jnp.exp2(log_w[:seq]), 0.0)
    a_prev = jnp.exp2(log_w[seq:] + block_sum[t:seq])
    carry = block_sum[t:seq] + block_sum[seq:]

    acc_ref[:t, :] = _dot(tile(a_own, 0).astype(BF16), v_ref[:t, :])
    for i in range(1, n_blocks):
        a = jnp.concatenate([tile(a_prev, i - 1), tile(a_own, i)], axis=1).astype(BF16)
        acc_ref[i * t:(i + 1) * t, :] = _dot(a, v_ref[(i - 1) * t:(i + 1) * t, :])

    far_needed = [jnp.max(tile(carry, i - 1)) >= UNDERFLOW_LOG2 for i in range(2, n_blocks)]
    for i in range(2, n_blocks):
        @pl.when(far_needed[i - 2])
        def _(i=i):
            n = i - 1
            neg_z = _dot_nt(tile(q, i), k[:n * t]) * to_neg_log2
            log_keep = _log2_keep(neg_z)
            stacked = jnp.concatenate([log_keep[:, j * t:(j + 1) * t] for j in range(n)], axis=0)
            after = _dot(_split_hi_lo(stacked), later2)
            c = tile(carry, i - 1)
            weights = [None] * n
            for j in range(n - 1, -1, -1):
                blk = slice(j * t, (j + 1) * t)
                weights[j] = jnp.exp2(log_keep[:, blk] - neg_z[:, blk] + after[blk] + c)
                c = c + after[blk][:, 0:1] + log_keep[:, j * t:j * t + 1]
            a = weights[0] if n == 1 else jnp.concatenate(weights, axis=1)
            acc_ref[i * t:(i + 1) * t, :] += _dot(a.astype(BF16), v_ref[:n * t, :])

    o_ref[...] = acc_ref[...].astype(o_ref.dtype)


def _dilated_log2_counts(seq):
    t = ATTN_TILE
    delta = np.arange(t)[:, None] - np.arange(seq)[None, :] + (seq - t)
    counts = np.zeros(delta.shape, np.float64)
    for window, dil in DILATED_PATTERNS:
        counts += (delta >= 0) & (delta % dil == 0) & (delta <= window)
    return np.where(counts > 0, np.log2(np.maximum(counts, 1.0)), NEG_INF).astype(np.float32)


def _dilated_kernel(q_ref, k_ref, v_ref, log2cnt_ref, o_ref):
    t = ATTN_TILE
    seq = k_ref.shape[0]
    n_blocks = seq // t
    q = q_ref[...]
    k = k_ref[...]
    widths = [(qi + 1) * t for qi in range(n_blocks)]
    s2 = [_dot_nt(q[w - t:w], k[:w]) * (SCORE_SCALE * LOG2_E) + log2cnt_ref[:, seq - w:]
          for w in widths]
    m = [jnp.max(x, axis=1, keepdims=True) for x in s2]
    p = [jnp.exp2(x - mx) for x, mx in zip(s2, m)]
    l = [jnp.sum(x, axis=1, keepdims=True) for x in p]
    for w, pw, lw in zip(widths, p, l):
        o_ref[w - t:w, :] = (_dot(pw.astype(BF16), v_ref[:w, :]) / lw).astype(o_ref.dtype)


def _mixers_kernel(qa_ref, ka_ref, va_ref, qb_ref, kb_ref, vb_ref, qd_ref, kd_ref, vd_ref,
                   log2cnt_ref, oa_ref, ob_ref, od_ref, acc_ref):
    _moba_kernel(qa_ref, ka_ref, va_ref, oa_ref)
    _dilated_kernel(qd_ref, kd_ref, vd_ref, log2cnt_ref, od_ref)
    _stickbreak_kernel(qb_ref, kb_ref, vb_ref, ob_ref, acc_ref)


def _attention_mixers(proj, log2cnt, *, batch, seq, n_heads):
    head_blk = (seq, HEAD_DIM)
    in_specs = [pl.BlockSpec(head_blk, lambda b, h, off=i * n_heads: (b, off + h))
                for i in range(9)]
    in_specs.append(pl.BlockSpec(log2cnt.shape, lambda b, h: (0, 0)))
    out_spec = pl.BlockSpec(head_blk, lambda b, h: (b, h))
    out_sds = jax.ShapeDtypeStruct((batch * seq, n_heads * HEAD_DIM), BF16)
    blk = 12 * _nbytes(head_blk, BF16) + _nbytes(log2cnt.shape, log2cnt.dtype)
    return pl.pallas_call(
        _mixers_kernel, grid=(batch, n_heads),
        in_specs=in_specs, out_specs=[out_spec] * 3, out_shape=[out_sds] * 3,
        scratch_shapes=[pltpu.VMEM(head_blk, F32)],
        compiler_params=_params(blk, scratch_bytes=_nbytes(head_blk, F32),
                                temp_bytes=8 * _nbytes((2 * seq, ATTN_TILE), F32)),
        name="attention_mixers",
    )(*([proj] * 9), log2cnt)


def _pool_kernel(u_ref, w_ref, scale_ref, o_ref):
    seq = u_ref.shape[0]
    g = pl.program_id(1)
    t_idx = lax.broadcasted_iota(jnp.int32, (seq, 1), 0)
    for gi, window in enumerate(POOL_WINDOWS):
        @pl.when(g == gi)
        def _(window=window):
            u = u_ref[...]
            win_sum = u
            span = 1
            while span < window:
                win_sum = win_sum + jnp.where(t_idx >= span, pltpu.roll(win_sum, span, axis=0), 0.0)
                span *= 2
            count = jnp.minimum(t_idx + 1, window).astype(F32)
            d = win_sum / count - u
            o_ref[...] = (_dot(d.astype(BF16), w_ref[...]) * scale_ref[...]).astype(o_ref.dtype)


def _multiscale_pool(u, w_pool, pool_scale, layer, *, batch, seq):
    _, n_groups, gd, _ = w_pool.shape
    assert all(w & (w - 1) == 0 for w in POOL_WINDOWS) and n_groups == len(POOL_WINDOWS)
    blk = _nbytes((seq, gd), F32) + _nbytes((seq, gd), BF16) + _nbytes((gd, gd), BF16)
    return pl.pallas_call(
        _pool_kernel, grid=(batch, n_groups),
        in_specs=[pl.BlockSpec((seq, gd), lambda b, g: (b, g)),
                  pl.BlockSpec((None, None, gd, gd), lambda b, g: (layer, g, 0, 0)),
                  pl.BlockSpec((1, gd), lambda b, g: (0, g))],
        out_specs=pl.BlockSpec((seq, gd), lambda b, g: (b, g)),
        out_shape=jax.ShapeDtypeStruct((u.shape[0], n_groups * gd), BF16),
        compiler_params=_params(blk, temp_bytes=6 * _nbytes((seq, gd), F32)),
        name="multiscale_pool",
    )(u, w_pool, pool_scale.reshape(1, n_groups * gd))


def kernel(x, ln_mix_pre, w_in, w_pool, pool_scale, mix_out_norm, w_out, ln_mix_post,
           ln_ffn_pre, w_gate, w_up, w_down, ln_ffn_post):
    batch, seq, d_model = x.shape
    depth = w_in.shape[0]
    group = w_out.shape[1] // N_MIXERS
    n_heads = group // HEAD_DIM
    n_blocks = seq // MOBA_BLOCK
    assert seq % ATTN_TILE == 0 and ATTN_TILE == MOBA_BLOCK and ATTN_TILE & (ATTN_TILE - 1) == 0
    assert MOBA_TOPK < n_blocks <= BF16_SUBLANES
    assert w_in.shape[2] == 10 * group

    w_pool, w_out = w_pool.astype(BF16), w_out.astype(BF16)
    log2cnt = jnp.asarray(_dilated_log2_counts(seq))

    x = x.reshape(batch * seq, d_model)
    h = _rmsnorm_cast(x, ln_mix_pre[0])
    for l in range(depth):
        qkv, u = _in_proj(h, w_in, l, qkv_width=9 * group, tm=1024, tn=group // 2)
        y_a, y_b, y_d = _attention_mixers(qkv, log2cnt, batch=batch, seq=seq, n_heads=n_heads)
        y_c = _multiscale_pool(u, w_pool, pool_scale[l], l, batch=batch, seq=seq)
        m = _out_proj((y_a, y_b, y_c, y_d), mix_out_norm[l], w_out, l)
        x, h = _residual_norm(x, m, ln_mix_post[l], ln_ffn_pre[l])
        act, w_down_bf16 = _swiglu_up(h, w_gate, w_up, w_down, l)
        f = _matmul(act, w_down_bf16[None], 0, out_dtype=BRANCH_DTYPE, tm=512, tn=512,
                    name="ffn_down")
        last = l + 1 == depth
        x, h = _residual_norm(x, f, ln_ffn_post[l], ln_mix_pre[0 if last else l + 1],
                              emit_next=not last)
    return x.reshape(batch, seq, d_model)
```

```python
import functools
import math

import numpy as np
import jax
import jax.numpy as jnp
from jax import lax
from jax.experimental import pallas as pl
from jax.experimental.pallas import tpu as pltpu

HEAD_DIM = 128
N_MIXERS = 4
MOBA_BLOCK = 256
MOBA_TOPK = 3
POOL_WINDOWS = (2, 4, 8, 16)
DILATED_PATTERNS = ((128, 1), (512, 4), (2048, 16))
RMS_EPS = 1e-6
NEG_INF = -1e30

ATTN_TILE = 256
BF16_SUBLANES = 16
LANES = 128
V7X_VMEM_BYTES = 64 * 1024 * 1024
VMEM_LIMIT_CAP = V7X_VMEM_BYTES - 8 * 1024 * 1024
LOG2_E = math.log2(math.e)
UNDERFLOW_LOG2 = -160.0
SCORE_SCALE = HEAD_DIM ** -0.5

F32 = jnp.float32
BF16 = jnp.bfloat16
BRANCH_DTYPE = BF16


def _nbytes(shape, dtype):
    return int(np.prod(shape)) * jnp.dtype(dtype).itemsize


def _params(block_bytes, scratch_bytes=0, temp_bytes=0, n_axes=2):
    limit = 2 * block_bytes + scratch_bytes + temp_bytes
    limit = min(max(limit, 16 * 1024 * 1024), VMEM_LIMIT_CAP)
    return pltpu.CompilerParams(
        dimension_semantics=("arbitrary",) * n_axes, vmem_limit_bytes=int(limit))


def _dot(a, b):
    return jnp.dot(a, b, preferred_element_type=F32)


def _dot_nt(a, b):
    return lax.dot_general(a, b, (((1,), (1,)), ((), ())), preferred_element_type=F32)


def _dot_tn(a, b):
    return lax.dot_general(a, b, (((0,), (0,)), ((), ())), preferred_element_type=F32)


def _rms_scale(x):
    return lax.rsqrt(jnp.mean(x * x, axis=-1, keepdims=True) + RMS_EPS)


def _rmsnorm_cast_kernel(x_ref, g_ref, o_ref):
    x = x_ref[...]
    o_ref[...] = (x * _rms_scale(x) * g_ref[...]).astype(o_ref.dtype)


def _rmsnorm_cast(x, g, *, tm=512):
    m, d = x.shape
    blk = _nbytes((tm, d), F32) + _nbytes((tm, d), BF16)
    return pl.pallas_call(
        _rmsnorm_cast_kernel, grid=(m // tm,),
        in_specs=[pl.BlockSpec((tm, d), lambda i: (i, 0)),
                  pl.BlockSpec((1, d), lambda i: (0, 0))],
        out_specs=pl.BlockSpec((tm, d), lambda i: (i, 0)),
        out_shape=jax.ShapeDtypeStruct((m, d), BF16),
        compiler_params=_params(blk, temp_bytes=_nbytes((tm, d), F32), n_axes=1),
        name="rmsnorm_cast",
    )(x, g.reshape(1, d))


def _residual_norm_kernel(x_ref, m_ref, g_post_ref, g_next_ref, xo_ref, ho_ref=None):
    m = m_ref[...].astype(F32)
    x = x_ref[...] + m * _rms_scale(m) * g_post_ref[...]
    xo_ref[...] = x
    if ho_ref is not None:
        ho_ref[...] = (x * _rms_scale(x) * g_next_ref[...]).astype(ho_ref.dtype)


def _residual_norm(x, m, g_post, g_next, *, emit_next=True, tm=256):
    rows, d = x.shape
    blk = (2 * _nbytes((tm, d), F32) + _nbytes((tm, d), m.dtype)
           + (_nbytes((tm, d), BF16) if emit_next else 0))
    row_spec = pl.BlockSpec((tm, d), lambda i: (i, 0))
    gain_spec = pl.BlockSpec((1, d), lambda i: (0, 0))
    out_shape = [jax.ShapeDtypeStruct((rows, d), F32)]
    if emit_next:
        out_shape.append(jax.ShapeDtypeStruct((rows, d), BF16))
    outs = pl.pallas_call(
        _residual_norm_kernel, grid=(rows // tm,),
        in_specs=[row_spec, row_spec, gain_spec, gain_spec],
        out_specs=[row_spec] * len(out_shape),
        out_shape=out_shape,
        compiler_params=_params(blk, temp_bytes=2 * _nbytes((tm, d), F32), n_axes=1),
        name="residual_norm",
    )(x, m, g_post.reshape(1, d), g_next.reshape(1, d))
    return (outs[0], outs[1]) if emit_next else (outs[0], None)


def _matmul_kernel(a_ref, w_ref, o_ref):
    o_ref[...] = _dot(a_ref[...], w_ref[...].astype(BF16)).astype(o_ref.dtype)


def _matmul(a, w, layer, *, out_dtype, tm, tn, col_block_offset=0, n_out=None, name="matmul"):
    m, k = a.shape
    n_out = w.shape[2] if n_out is None else n_out
    blk = _nbytes((tm, k), a.dtype) + _nbytes((k, tn), w.dtype) + _nbytes((tm, tn), out_dtype)
    blk += _nbytes((k, tn), BF16) // 2 if w.dtype != BF16 else 0
    return pl.pallas_call(
        _matmul_kernel, grid=(m // tm, n_out // tn),
        in_specs=[pl.BlockSpec((tm, k), lambda i, j: (i, 0)),
                  pl.BlockSpec((None, k, tn), lambda i, j: (layer, 0, j + col_block_offset))],
        out_specs=pl.BlockSpec((tm, tn), lambda i, j: (i, j)),
        out_shape=jax.ShapeDtypeStruct((m, n_out), out_dtype),
        compiler_params=_params(blk, temp_bytes=2 * _nbytes((tm, tn), F32)),
        name=name,
    )(a, w)


def _in_proj_kernel(h_ref, w_ref, qkv_ref, u_ref):
    y = _dot(h_ref[...], w_ref[...].astype(BF16))
    qkv_ref[...] = y.astype(qkv_ref.dtype)
    u_ref[...] = y


def _in_proj(h, w_in, layer, *, qkv_width, tm, tn):
    m, k = h.shape
    n = w_in.shape[2]
    qkv_blocks = qkv_width // tn
    u_blocks = (n - qkv_width) // tn
    blk = (_nbytes((tm, k), BF16) + _nbytes((k, tn), w_in.dtype) + _nbytes((k, tn), BF16) // 2
           + _nbytes((tm, tn), BF16) + _nbytes((tm, tn), F32))
    return pl.pallas_call(
        _in_proj_kernel, grid=(m // tm, n // tn),
        in_specs=[pl.BlockSpec((tm, k), lambda i, j: (i, 0)),
                  pl.BlockSpec((None, k, tn), lambda i, j: (layer, 0, j))],
        out_specs=[pl.BlockSpec((tm, tn), lambda i, j: (i, jnp.minimum(j, qkv_blocks))),
                   pl.BlockSpec((tm, tn), lambda i, j: (
                       i, jnp.where(j < qkv_blocks, u_blocks, j - qkv_blocks)))],
        out_shape=[jax.ShapeDtypeStruct((m, qkv_width + tn), BF16),
                   jax.ShapeDtypeStruct((m, n - qkv_width + tn), F32)],
        compiler_params=_params(blk, temp_bytes=2 * _nbytes((tm, tn), F32)),
        name="in_proj",
    )(h, w_in)


def _out_proj_kernel(ya_ref, yb_ref, yc_ref, yd_ref, g_ref, w_ref, o_ref, hm_ref):
    @pl.when(pl.program_id(1) == 0)
    def _():
        gw = ya_ref.shape[1]
        for i, y_ref in enumerate((ya_ref, yb_ref, yc_ref, yd_ref)):
            y = y_ref[...].astype(F32)
            cols = slice(i * gw, (i + 1) * gw)
            hm_ref[:, cols] = (y * _rms_scale(y) * g_ref[:, cols]).astype(hm_ref.dtype)

    o_ref[...] = _dot(hm_ref[...], w_ref[...]).astype(o_ref.dtype)


def _out_proj(ys, g, w_out, layer, *, tm=1024, tn=512):
    rows, gw = ys[0].shape
    d = gw * len(ys)
    n = w_out.shape[2]
    blk = (len(ys) * _nbytes((tm, gw), ys[0].dtype) + _nbytes((d, tn), w_out.dtype)
           + _nbytes((tm, tn), BRANCH_DTYPE))
    y_spec = pl.BlockSpec((tm, gw), lambda i, j: (i, 0))
    return pl.pallas_call(
        _out_proj_kernel, grid=(rows // tm, n // tn),
        in_specs=[y_spec] * len(ys) + [pl.BlockSpec((1, d), lambda i, j: (0, 0)),
                                       pl.BlockSpec((None, d, tn), lambda i, j: (layer, 0, j))],
        out_specs=pl.BlockSpec((tm, tn), lambda i, j: (i, j)),
        out_shape=jax.ShapeDtypeStruct((rows, n), BRANCH_DTYPE),
        scratch_shapes=[pltpu.VMEM((tm, d), BF16)],
        compiler_params=_params(blk, scratch_bytes=_nbytes((tm, d), BF16),
                                temp_bytes=2 * _nbytes((tm, tn), F32)),
        name="out_proj",
    )(*ys, g.reshape(1, d), w_out)


def _ffn_up_kernel(x_ref, m_ref, g_post_ref, g_pre_ref, wg_ref, wu_ref, wd_ref,
                   xo_ref, act_ref, wd_bf16_ref, h_even, h_odd, *, slab_steps):
    r = pl.program_id(0)
    j = pl.program_id(1)
    slab = x_ref.shape[0]

    def residual_norm_slab(h_ref):
        m = m_ref[...].astype(F32)
        x = x_ref[...] + m * _rms_scale(m) * g_post_ref[...]
        xo_ref[...] = x
        first_row = pl.multiple_of(jnp.minimum(j, slab_steps - 1) * slab, slab)
        h_ref[pl.ds(first_row, slab), :] = (x * _rms_scale(x) * g_pre_ref[...]).astype(h_ref.dtype)

    def swiglu(h_ref):
        h = h_ref[...]
        g = _dot(h, wg_ref[...].astype(BF16))
        u = _dot(h, wu_ref[...].astype(BF16))
        act_ref[...] = (g / (1.0 + jnp.exp(-g)) * u).astype(act_ref.dtype)

    @pl.when(r == 0)
    def _():
        residual_norm_slab(h_even)

    @pl.when((r > 0) & (r % 2 == 0))
    def _():
        residual_norm_slab(h_even)
        swiglu(h_odd)

    @pl.when(r % 2 == 1)
    def _():
        residual_norm_slab(h_odd)
        swiglu(h_even)

    wd_bf16_ref[...] = wd_ref[...].astype(wd_bf16_ref.dtype)


def _ffn_up(x, m, g_post, g_pre, wg, wu, wd, layer, *, tm=1024, tn=256, slab=32):
    rows, k = x.shape
    n = wg.shape[2]
    n_tiles, n_cols = rows // tm, n // tn
    slab_steps = tm // slab
    d_ff, d_out = wd.shape[1:]
    wd_slab = d_ff // (n_tiles * n_cols)
    assert slab_steps <= n_cols and slab_steps * slab == tm and slab % BF16_SUBLANES == 0
    assert wd_slab * n_tiles * n_cols == d_ff and wd_slab % BF16_SUBLANES == 0

    def slab_index(r, j):
        tile = jnp.minimum(r, n_tiles - 1)
        step = jnp.where(r < n_tiles, jnp.minimum(j, slab_steps - 1), slab_steps - 1)
        return tile * slab_steps + step

    def col(r, j):
        return jnp.where(r > 0, j, 0)

    def wd_index(r, j):
        return jnp.maximum((r - 1) * n_cols + j, 0)

    slab_spec = pl.BlockSpec((slab, k), lambda r, j: (slab_index(r, j), 0))
    gain_spec = pl.BlockSpec((1, k), lambda r, j: (0, 0))
    w_spec = pl.BlockSpec((None, k, tn), lambda r, j: (layer, 0, col(r, j)))
    blk = (2 * _nbytes((slab, k), F32) + _nbytes((slab, k), m.dtype)
           + 2 * _nbytes((k, tn), wg.dtype) + _nbytes((k, tn), BF16) + _nbytes((tm, tn), BF16)
           + _nbytes((wd_slab, d_out), F32) + _nbytes((wd_slab, d_out), BF16))
    kernel = functools.partial(_ffn_up_kernel, slab_steps=slab_steps)
    return pl.pallas_call(
        kernel, grid=(n_tiles + 1, n_cols),
        in_specs=[slab_spec, slab_spec, gain_spec, gain_spec, w_spec, w_spec,
                  pl.BlockSpec((None, wd_slab, d_out), lambda r, j: (layer, wd_index(r, j), 0))],
        out_specs=[slab_spec,
                   pl.BlockSpec((tm, tn), lambda r, j: (jnp.maximum(r - 1, 0), col(r, j))),
                   pl.BlockSpec((wd_slab, d_out), lambda r, j: (wd_index(r, j), 0))],
        out_shape=[jax.ShapeDtypeStruct((rows, k), F32), jax.ShapeDtypeStruct((rows, n), BF16),
                   jax.ShapeDtypeStruct((d_ff, d_out), BF16)],
        scratch_shapes=[pltpu.VMEM((tm, k), BF16)] * 2,
        compiler_params=_params(blk, scratch_bytes=_nbytes((2, tm, k), BF16),
                                temp_bytes=3 * _nbytes((tm, tn), F32)),
        name="ffn_up",
    )(x, m, g_post.reshape(1, k), g_pre.reshape(1, k), wg, wu, wd)


def _causal_tile_mask():
    row = lax.broadcasted_iota(jnp.int32, (ATTN_TILE, ATTN_TILE), 0)
    col = lax.broadcasted_iota(jnp.int32, (ATTN_TILE, ATTN_TILE), 1)
    return row, col


def _softmax_pv(s2, v):
    m = jnp.max(s2, axis=1, keepdims=True)
    p = jnp.exp2(s2 - m)
    l = jnp.sum(p, axis=1, keepdims=True)
    return _dot(p.astype(BF16), v) / l


def _moba_kernel(q_ref, k_ref, v_ref, o_ref):
    t = ATTN_TILE
    seq = k_ref.shape[0]
    n_blocks = seq // t
    rows_pad = BF16_SUBLANES
    q = q_ref[...]
    k = k_ref[...]

    kmean = jnp.sum(k.astype(F32).reshape(n_blocks, t, HEAD_DIM), axis=1) * (1.0 / t)
    kmean = jnp.concatenate([kmean, jnp.zeros((rows_pad - n_blocks, HEAD_DIM), F32)], axis=0)
    kmean_hi = kmean.astype(BF16)
    kmean_lo = (kmean - kmean_hi.astype(F32)).astype(BF16)
    gate = _dot_nt(kmean_hi, q) + _dot_nt(kmean_lo, q)

    blk_id = lax.broadcasted_iota(jnp.int32, gate.shape, 0)
    q_blk = lax.shift_right_logical(lax.broadcasted_iota(jnp.int32, gate.shape, 1),
                                    int(math.log2(t)))
    rank = jnp.zeros(gate.shape, jnp.int32)
    for i in range(n_blocks):
        gi = gate[i:i + 1, :]
        beats = jnp.where(gi > gate, 1, jnp.where((gi == gate) & (i < blk_id), 1, 0))
        rank = rank + jnp.where(i < q_blk, beats, 0)
    keep = (blk_id == q_blk) | ((blk_id < q_blk) & (rank < MOBA_TOPK))
    bias = jnp.where(keep, 0.0, NEG_INF).astype(BF16)

    lane_id = lax.broadcasted_iota(jnp.int32, (rows_pad, LANES), 1)
    eye = jnp.where(lane_id == lax.broadcasted_iota(jnp.int32, (rows_pad, LANES), 0), 1.0, 0.0)
    bias_cols = _dot_tn(bias, eye.astype(BF16)).astype(BF16)
    key_blk = lax.shift_right_logical(lax.broadcasted_iota(jnp.int32, (seq, LANES), 0),
                                      int(math.log2(t)))
    one_hot = jnp.where(key_blk == lax.broadcasted_iota(jnp.int32, (seq, LANES), 1), 1.0, 0.0)
    q_aug = jnp.concatenate([q, bias_cols], axis=1)
    k_aug = jnp.concatenate([k, one_hot.astype(BF16)], axis=1)

    row, col = _causal_tile_mask()
    causal = col <= row
    to_log2 = SCORE_SCALE * LOG2_E
    tiles = [slice(qi * t, (qi + 1) * t) for qi in range(n_blocks)]
    s_own = [jnp.where(causal, _dot_nt(q_aug[r], k_aug[r]), NEG_INF) * to_log2 for r in tiles]
    s_past = [_dot_nt(q_aug[r], k_aug[:r.start]) * to_log2 for r in tiles[1:]]
    m = [jnp.max(x, axis=1, keepdims=True) for x in s_own]
    m = m[:1] + [jnp.maximum(mo, jnp.max(x, axis=1, keepdims=True)) for mo, x in zip(m[1:], s_past)]
    p_own = [jnp.exp2(x - mx) for x, mx in zip(s_own, m)]
    p_past = [jnp.exp2(x - mx) for x, mx in zip(s_past, m[1:])]
    l = [jnp.sum(x, axis=1, keepdims=True) for x in p_own]
    l = l[:1] + [lo + jnp.sum(x, axis=1, keepdims=True) for lo, x in zip(l[1:], p_past)]
    for qi, r in enumerate(tiles):
        pv = _dot(p_own[qi].astype(BF16), v_ref[r, :])
        if qi > 0:
            pv = pv + _dot(p_past[qi - 1].astype(BF16), v_ref[:r.start, :])
        o_ref[r, :] = (pv / l[qi]).astype(o_ref.dtype)


def _log2_keep(neg_z):
    neg_abs = lax.bitcast_convert_type(
        lax.bitcast_convert_type(neg_z, jnp.uint32) | jnp.uint32(0x80000000), F32)
    return jnp.minimum(neg_z, 0.0) - jnp.log2(1.0 + jnp.exp2(neg_abs))


def _split_hi_lo(x):
    hi = lax.bitcast_convert_type(
        lax.bitcast_convert_type(x, jnp.uint32) & jnp.uint32(0xFFFF0000), F32)
    return jnp.concatenate([hi.astype(BF16), (x - hi).astype(BF16)], axis=1)


def _stickbreak_kernel(q_ref, k_ref, v_ref, o_ref, acc_ref):
    t = ATTN_TILE
    seq = k_ref.shape[0]
    n_blocks = seq // t
    q = q_ref[...]
    k = k_ref[...]
    to_neg_log2 = -SCORE_SCALE * LOG2_E
    row, col = _causal_tile_mask()
    later = jnp.where(row > col, 1.0, 0.0).astype(BF16)
    later2 = jnp.concatenate([later, later], axis=0)

    def tile(x, i):
        return x[i * t:(i + 1) * t]

    neg_z = jnp.concatenate(
        [_dot_nt(tile(q, i), tile(k, i)) for i in range(n_blocks)]
        + [_dot_nt(tile(q, i), tile(k, i - 1)) for i in range(1, n_blocks)], axis=0) * to_neg_log2
    row_in_tile = lax.broadcasted_iota(jnp.int32, (seq, t), 0) & (t - 1)
    strict = lax.broadcasted_iota(jnp.int32, (seq, t), 1) < row_in_tile
    log_keep = _log2_keep(neg_z)
    log_keep = jnp.concatenate([jnp.where(strict, log_keep[:seq], 0.0), log_keep[seq:]], axis=0)
    after = _dot(_split_hi_lo(log_keep), later2)
    block_sum = after[:, 0:1] + log_keep[:, 0:1]
    log_w = log_keep - neg_z + after
    a_own = jnp.where(strict, jnp.exp2(log_w[:seq]), 0.0)
    a_prev = jnp.exp2(log_w[seq:] + block_sum[t:seq])
    carry = block_sum[t:seq] + block_sum[seq:]

    acc_ref[:t, :] = _dot(tile(a_own, 0).astype(BF16), v_ref[:t, :])
    for i in range(1, n_blocks):
        a = jnp.concatenate([tile(a_prev, i - 1), tile(a_own, i)], axis=1).astype(BF16)
        acc_ref[i * t:(i + 1) * t, :] = _dot(a, v_ref[(i - 1) * t:(i + 1) * t, :])

    far_needed = [jnp.max(tile(carry, i - 1)) >= UNDERFLOW_LOG2 for i in range(2, n_blocks)]
    for i in range(2, n_blocks):
        @pl.when(far_needed[i - 2])
        def _(i=i):
            n = i - 1
            neg_z = _dot_nt(tile(q, i), k[:n * t]) * to_neg_log2
            log_keep = _log2_keep(neg_z)
            stacked = jnp.concatenate([log_keep[:, j * t:(j + 1) * t] for j in range(n)], axis=0)
            after = _dot(_split_hi_lo(stacked), later2)
            c = tile(carry, i - 1)
            weights = [None] * n
            for j in range(n - 1, -1, -1):
                blk = slice(j * t, (j + 1) * t)
                weights[j] = jnp.exp2(log_keep[:, blk] - neg_z[:, blk] + after[blk] + c)
                c = c + after[blk][:, 0:1] + log_keep[:, j * t:j * t + 1]
            a = weights[0] if n == 1 else jnp.concatenate(weights, axis=1)
            acc_ref[i * t:(i + 1) * t, :] += _dot(a.astype(BF16), v_ref[:n * t, :])

    o_ref[...] = acc_ref[...].astype(o_ref.dtype)


def _dilated_log2_counts(seq):
    t = ATTN_TILE
    delta = np.arange(t)[:, None] - np.arange(seq)[None, :] + (seq - t)
    counts = np.zeros(delta.shape, np.float64)
    for window, dil in DILATED_PATTERNS:
        counts += (delta >= 0) & (delta % dil == 0) & (delta <= window)
    return np.where(counts > 0, np.log2(np.maximum(counts, 1.0)), NEG_INF).astype(np.float32)


def _dilated_kernel(q_ref, k_ref, v_ref, log2cnt_ref, o_ref):
    t = ATTN_TILE
    seq = k_ref.shape[0]
    n_blocks = seq // t
    q = q_ref[...]
    k = k_ref[...]
    widths = [(qi + 1) * t for qi in range(n_blocks)]
    s2 = [_dot_nt(q[w - t:w], k[:w]) * (SCORE_SCALE * LOG2_E) + log2cnt_ref[:, seq - w:]
          for w in widths]
    m = [jnp.max(x, axis=1, keepdims=True) for x in s2]
    p = [jnp.exp2(x - mx) for x, mx in zip(s2, m)]
    l = [jnp.sum(x, axis=1, keepdims=True) for x in p]
    for w, pw, lw in zip(widths, p, l):
        o_ref[w - t:w, :] = (_dot(pw.astype(BF16), v_ref[:w, :]) / lw).astype(o_ref.dtype)


def _mixers_kernel(qa_ref, ka_ref, va_ref, qb_ref, kb_ref, vb_ref, qd_ref, kd_ref, vd_ref,
                   log2cnt_ref, oa_ref, ob_ref, od_ref, acc_ref):
    _moba_kernel(qa_ref, ka_ref, va_ref, oa_ref)
    _dilated_kernel(qd_ref, kd_ref, vd_ref, log2cnt_ref, od_ref)
    _stickbreak_kernel(qb_ref, kb_ref, vb_ref, ob_ref, acc_ref)


def _attention_mixers(proj, log2cnt, *, batch, seq, n_heads):
    head_blk = (seq, HEAD_DIM)
    in_specs = [pl.BlockSpec(head_blk, lambda b, h, off=i * n_heads: (b, off + h))
                for i in range(9)]
    in_specs.append(pl.BlockSpec(log2cnt.shape, lambda b, h: (0, 0)))
    out_spec = pl.BlockSpec(head_blk, lambda b, h: (b, h))
    out_sds = jax.ShapeDtypeStruct((batch * seq, n_heads * HEAD_DIM), BF16)
    blk = 12 * _nbytes(head_blk, BF16) + _nbytes(log2cnt.shape, log2cnt.dtype)
    return pl.pallas_call(
        _mixers_kernel, grid=(batch, n_heads),
        in_specs=in_specs, out_specs=[out_spec] * 3, out_shape=[out_sds] * 3,
        scratch_shapes=[pltpu.VMEM(head_blk, F32)],
        compiler_params=_params(blk, scratch_bytes=_nbytes(head_blk, F32),
                                temp_bytes=8 * _nbytes((2 * seq, ATTN_TILE), F32)),
        name="attention_mixers",
    )(*([proj] * 9), log2cnt)


def _pool_kernel(u_ref, w_ref, scale_ref, o_ref):
    seq = u_ref.shape[0]
    g = pl.program_id(1)
    t_idx = lax.broadcasted_iota(jnp.int32, (seq, 1), 0)
    for gi, window in enumerate(POOL_WINDOWS):
        @pl.when(g == gi)
        def _(window=window):
            u = u_ref[...]
            win_sum = u
            span = 1
            while span < window:
                win_sum = win_sum + jnp.where(t_idx >= span, pltpu.roll(win_sum, span, axis=0), 0.0)
                span *= 2
            count = jnp.minimum(t_idx + 1, window).astype(F32)
            d = win_sum / count - u
            o_ref[...] = (_dot(d.astype(BF16), w_ref[...]) * scale_ref[...]).astype(o_ref.dtype)


def _multiscale_pool(u, w_pool, pool_scale, layer, *, batch, seq):
    _, n_groups, gd, _ = w_pool.shape
    assert all(w & (w - 1) == 0 for w in POOL_WINDOWS) and n_groups == len(POOL_WINDOWS)
    blk = _nbytes((seq, gd), F32) + _nbytes((seq, gd), BF16) + _nbytes((gd, gd), BF16)
    return pl.pallas_call(
        _pool_kernel, grid=(batch, n_groups),
        in_specs=[pl.BlockSpec((seq, gd), lambda b, g: (b, g)),
                  pl.BlockSpec((None, None, gd, gd), lambda b, g: (layer, g, 0, 0)),
                  pl.BlockSpec((1, gd), lambda b, g: (0, g))],
        out_specs=pl.BlockSpec((seq, gd), lambda b, g: (b, g)),
        out_shape=jax.ShapeDtypeStruct((u.shape[0], n_groups * gd), BF16),
        compiler_params=_params(blk, temp_bytes=6 * _nbytes((seq, gd), F32)),
        name="multiscale_pool",
    )(u, w_pool, pool_scale.reshape(1, n_groups * gd))


def kernel(x, ln_mix_pre, w_in, w_pool, pool_scale, mix_out_norm, w_out, ln_mix_post,
           ln_ffn_pre, w_gate, w_up, w_down, ln_ffn_post):
    batch, seq, d_model = x.shape
    depth = w_in.shape[0]
    group = w_out.shape[1] // N_MIXERS
    n_heads = group // HEAD_DIM
    n_blocks = seq // MOBA_BLOCK
    assert seq % ATTN_TILE == 0 and ATTN_TILE == MOBA_BLOCK and ATTN_TILE & (ATTN_TILE - 1) == 0
    assert MOBA_TOPK < n_blocks <= BF16_SUBLANES
    assert w_in.shape[2] == 10 * group

    w_pool, w_out = w_pool.astype(BF16), w_out.astype(BF16)
    log2cnt = jnp.asarray(_dilated_log2_counts(seq))

    x = x.reshape(batch * seq, d_model)
    h = _rmsnorm_cast(x, ln_mix_pre[0])
    for l in range(depth):
        qkv, u = _in_proj(h, w_in, l, qkv_width=9 * group, tm=1024, tn=group // 2)
        y_a, y_b, y_d = _attention_mixers(qkv, log2cnt, batch=batch, seq=seq, n_heads=n_heads)
        y_c = _multiscale_pool(u, w_pool, pool_scale[l], l, batch=batch, seq=seq)
        m = _out_proj((y_a, y_b, y_c, y_d), mix_out_norm[l], w_out, l)
        x, act, w_down_bf16 = _ffn_up(x, m, ln_mix_post[l], ln_ffn_pre[l], w_gate, w_up, w_down, l)
        f = _matmul(act, w_down_bf16[None], 0, out_dtype=BRANCH_DTYPE, tm=512, tn=512,
                    name="ffn_down")
        last = l + 1 == depth
        x, h = _residual_norm(x, f, ln_ffn_post[l], ln_mix_pre[0 if last else l + 1],
                              emit_next=not last)
    return x.reshape(batch, seq, d_model)
```

```python
import math

import numpy as np
import jax
import jax.numpy as jnp
from jax import lax
from jax.experimental import pallas as pl
from jax.experimental.pallas import tpu as pltpu

HEAD_DIM = 128
N_MIXERS = 4
MOBA_BLOCK = 256
MOBA_TOPK = 3
POOL_WINDOWS = (2, 4, 8, 16)
DILATED_PATTERNS = ((128, 1), (512, 4), (2048, 16))
RMS_EPS = 1e-6
NEG_INF = -1e30

ATTN_TILE = 256
BF16_SUBLANES = 16
LANES = 128
V7X_VMEM_BYTES = 64 * 1024 * 1024
VMEM_LIMIT_CAP = V7X_VMEM_BYTES - 8 * 1024 * 1024
LOG2_E = math.log2(math.e)
UNDERFLOW_LOG2 = -160.0
SCORE_SCALE = HEAD_DIM ** -0.5

F32 = jnp.float32
BF16 = jnp.bfloat16
BRANCH_DTYPE = BF16


def _nbytes(shape, dtype):
    return int(np.prod(shape)) * jnp.dtype(dtype).itemsize


def _params(block_bytes, scratch_bytes=0, temp_bytes=0, n_axes=2):
    limit = 2 * block_bytes + scratch_bytes + temp_bytes
    limit = min(max(limit, 16 * 1024 * 1024), VMEM_LIMIT_CAP)
    return pltpu.CompilerParams(
        dimension_semantics=("arbitrary",) * n_axes, vmem_limit_bytes=int(limit))


def _dot(a, b):
    return jnp.dot(a, b, preferred_element_type=F32)


def _dot_nt(a, b):
    return lax.dot_general(a, b, (((1,), (1,)), ((), ())), preferred_element_type=F32)


def _dot_tn(a, b):
    return lax.dot_general(a, b, (((0,), (0,)), ((), ())), preferred_element_type=F32)


def _rms_scale(x):
    return lax.rsqrt(jnp.mean(x * x, axis=-1, keepdims=True) + RMS_EPS)


def _rmsnorm_cast_kernel(x_ref, g_ref, o_ref):
    x = x_ref[...]
    o_ref[...] = (x * _rms_scale(x) * g_ref[...]).astype(o_ref.dtype)


def _rmsnorm_cast(x, g, *, tm=512):
    m, d = x.shape
    blk = _nbytes((tm, d), F32) + _nbytes((tm, d), BF16)
    return pl.pallas_call(
        _rmsnorm_cast_kernel, grid=(m // tm,),
        in_specs=[pl.BlockSpec((tm, d), lambda i: (i, 0)),
                  pl.BlockSpec((1, d), lambda i: (0, 0))],
        out_specs=pl.BlockSpec((tm, d), lambda i: (i, 0)),
        out_shape=jax.ShapeDtypeStruct((m, d), BF16),
        compiler_params=_params(blk, temp_bytes=_nbytes((tm, d), F32), n_axes=1),
        name="rmsnorm_cast",
    )(x, g.reshape(1, d))


def _residual_norm_kernel(x_ref, m_ref, g_post_ref, g_next_ref, xo_ref, ho_ref=None):
    m = m_ref[...].astype(F32)
    x = x_ref[...] + m * _rms_scale(m) * g_post_ref[...]
    xo_ref[...] = x
    if ho_ref is not None:
        ho_ref[...] = (x * _rms_scale(x) * g_next_ref[...]).astype(ho_ref.dtype)


def _residual_norm(x, m, g_post, g_next, *, emit_next=True, tm=256):
    rows, d = x.shape
    blk = (2 * _nbytes((tm, d), F32) + _nbytes((tm, d), m.dtype)
           + (_nbytes((tm, d), BF16) if emit_next else 0))
    row_spec = pl.BlockSpec((tm, d), lambda i: (i, 0))
    gain_spec = pl.BlockSpec((1, d), lambda i: (0, 0))
    out_shape = [jax.ShapeDtypeStruct((rows, d), F32)]
    if emit_next:
        out_shape.append(jax.ShapeDtypeStruct((rows, d), BF16))
    outs = pl.pallas_call(
        _residual_norm_kernel, grid=(rows // tm,),
        in_specs=[row_spec, row_spec, gain_spec, gain_spec],
        out_specs=[row_spec] * len(out_shape),
        out_shape=out_shape,
        compiler_params=_params(blk, temp_bytes=2 * _nbytes((tm, d), F32), n_axes=1),
        name="residual_norm",
    )(x, m, g_post.reshape(1, d), g_next.reshape(1, d))
    return (outs[0], outs[1]) if emit_next else (outs[0], None)


def _matmul_kernel(a_ref, w_ref, o_ref):
    o_ref[...] = _dot(a_ref[...], w_ref[...].astype(BF16)).astype(o_ref.dtype)


def _matmul(a, w, layer, *, out_dtype, tm, tn, col_block_offset=0, n_out=None, name="matmul"):
    m, k = a.shape
    n_out = w.shape[2] if n_out is None else n_out
    blk = _nbytes((tm, k), a.dtype) + _nbytes((k, tn), w.dtype) + _nbytes((tm, tn), out_dtype)
    blk += _nbytes((k, tn), BF16) // 2 if w.dtype != BF16 else 0
    return pl.pallas_call(
        _matmul_kernel, grid=(m // tm, n_out // tn),
        in_specs=[pl.BlockSpec((tm, k), lambda i, j: (i, 0)),
                  pl.BlockSpec((None, k, tn), lambda i, j: (layer, 0, j + col_block_offset))],
        out_specs=pl.BlockSpec((tm, tn), lambda i, j: (i, j)),
        out_shape=jax.ShapeDtypeStruct((m, n_out), out_dtype),
        compiler_params=_params(blk, temp_bytes=2 * _nbytes((tm, tn), F32)),
        name=name,
    )(a, w)


def _in_proj_kernel(h_ref, w_ref, *refs):
    if len(refs) == 4:
        side_ref, qkv_ref, u_ref, side_bf16_ref = refs
        side_bf16_ref[...] = side_ref[...].astype(side_bf16_ref.dtype)
    else:
        qkv_ref, u_ref = refs
    y = _dot(h_ref[...], w_ref[...].astype(BF16))
    qkv_ref[...] = y.astype(qkv_ref.dtype)
    u_ref[...] = y


def _in_proj(h, w_in, layer, *, qkv_width, tm, tn, also_round=None, slab=64):
    m, k = h.shape
    n = w_in.shape[2]
    n_cols = n // tn
    qkv_blocks = qkv_width // tn
    u_blocks = (n - qkv_width) // tn
    blk = (_nbytes((tm, k), BF16) + _nbytes((k, tn), w_in.dtype) + _nbytes((k, tn), BF16) // 2
           + _nbytes((tm, tn), BF16) + _nbytes((tm, tn), F32))
    in_specs = [pl.BlockSpec((tm, k), lambda i, j: (i, 0)),
                pl.BlockSpec((None, k, tn), lambda i, j: (layer, 0, j))]
    out_specs = [pl.BlockSpec((tm, tn), lambda i, j: (i, jnp.minimum(j, qkv_blocks))),
                 pl.BlockSpec((tm, tn), lambda i, j: (
                     i, jnp.where(j < qkv_blocks, u_blocks, j - qkv_blocks)))]
    out_shape = [jax.ShapeDtypeStruct((m, qkv_width + tn), BF16),
                 jax.ShapeDtypeStruct((m, n - qkv_width + tn), F32)]
    operands = [h, w_in]
    if also_round is not None:
        rows, cols = also_round.shape
        n_slabs = rows // slab
        assert n_slabs * slab == rows and n_slabs <= (m // tm) * n_cols
        side_spec = pl.BlockSpec((slab, cols),
                                 lambda i, j: (jnp.minimum(i * n_cols + j, n_slabs - 1), 0))
        in_specs.append(side_spec)
        out_specs.append(side_spec)
        out_shape.append(jax.ShapeDtypeStruct((rows, cols), BF16))
        operands.append(also_round)
        blk += _nbytes((slab, cols), F32) + _nbytes((slab, cols), BF16)
    return pl.pallas_call(
        _in_proj_kernel, grid=(m // tm, n_cols),
        in_specs=in_specs, out_specs=out_specs, out_shape=out_shape,
        compiler_params=_params(blk, temp_bytes=2 * _nbytes((tm, tn), F32)),
        name="in_proj",
    )(*operands)


def _out_proj_kernel(ya_ref, yb_ref, yc_ref, yd_ref, g_ref, w_ref, o_ref, hm_ref):
    @pl.when(pl.program_id(1) == 0)
    def _():
        gw = ya_ref.shape[1]
        for i, y_ref in enumerate((ya_ref, yb_ref, yc_ref, yd_ref)):
            y = y_ref[...].astype(F32)
            cols = slice(i * gw, (i + 1) * gw)
            hm_ref[:, cols] = (y * _rms_scale(y) * g_ref[:, cols]).astype(hm_ref.dtype)

    o_ref[...] = _dot(hm_ref[...], w_ref[...]).astype(o_ref.dtype)


def _out_proj(ys, g, w_out, layer, *, tm=1024, tn=512):
    rows, gw = ys[0].shape
    d = gw * len(ys)
    n = w_out.shape[2]
    blk = (len(ys) * _nbytes((tm, gw), ys[0].dtype) + _nbytes((d, tn), w_out.dtype)
           + _nbytes((tm, tn), BRANCH_DTYPE))
    y_spec = pl.BlockSpec((tm, gw), lambda i, j: (i, 0))
    return pl.pallas_call(
        _out_proj_kernel, grid=(rows // tm, n // tn),
        in_specs=[y_spec] * len(ys) + [pl.BlockSpec((1, d), lambda i, j: (0, 0)),
                                       pl.BlockSpec((None, d, tn), lambda i, j: (layer, 0, j))],
        out_specs=pl.BlockSpec((tm, tn), lambda i, j: (i, j)),
        out_shape=jax.ShapeDtypeStruct((rows, n), BRANCH_DTYPE),
        scratch_shapes=[pltpu.VMEM((tm, d), BF16)],
        compiler_params=_params(blk, scratch_bytes=_nbytes((tm, d), BF16),
                                temp_bytes=2 * _nbytes((tm, tn), F32)),
        name="out_proj",
    )(*ys, g.reshape(1, d), w_out)


def _swiglu_up_kernel(h_ref, wg_ref, wu_ref, wd_ref, o_ref, wd_bf16_ref):
    h = h_ref[...]
    g = _dot(h, wg_ref[...].astype(BF16))
    u = _dot(h, wu_ref[...].astype(BF16))
    o_ref[...] = (g / (1.0 + jnp.exp(-g)) * u).astype(o_ref.dtype)
    wd_bf16_ref[...] = wd_ref[...].astype(wd_bf16_ref.dtype)


def _swiglu_up(h, wg, wu, wd, layer, *, tm=1024, tn=256):
    m, k = h.shape
    n = wg.shape[2]
    grid = (m // tm, n // tn)
    d_ff, d_out = wd.shape[1:]
    slab = d_ff // (grid[0] * grid[1])
    assert slab * grid[0] * grid[1] == d_ff and slab % BF16_SUBLANES == 0
    blk = (_nbytes((tm, k), BF16) + 2 * _nbytes((k, tn), wg.dtype) + _nbytes((k, tn), BF16)
           + _nbytes((tm, tn), BF16) + _nbytes((slab, d_out), F32) + _nbytes((slab, d_out), BF16))
    w_spec = pl.BlockSpec((None, k, tn), lambda i, j: (layer, 0, j))
    return pl.pallas_call(
        _swiglu_up_kernel, grid=grid,
        in_specs=[pl.BlockSpec((tm, k), lambda i, j: (i, 0)), w_spec, w_spec,
                  pl.BlockSpec((None, slab, d_out), lambda i, j: (layer, i * grid[1] + j, 0))],
        out_specs=[pl.BlockSpec((tm, tn), lambda i, j: (i, j)),
                   pl.BlockSpec((slab, d_out), lambda i, j: (i * grid[1] + j, 0))],
        out_shape=[jax.ShapeDtypeStruct((m, n), BF16), jax.ShapeDtypeStruct((d_ff, d_out), BF16)],
        compiler_params=_params(blk, temp_bytes=3 * _nbytes((tm, tn), F32)),
        name="swiglu_up",
    )(h, wg, wu, wd)


def _causal_tile_mask():
    row = lax.broadcasted_iota(jnp.int32, (ATTN_TILE, ATTN_TILE), 0)
    col = lax.broadcasted_iota(jnp.int32, (ATTN_TILE, ATTN_TILE), 1)
    return row, col


def _softmax_pv(s2, v):
    m = jnp.max(s2, axis=1, keepdims=True)
    p = jnp.exp2(s2 - m)
    l = jnp.sum(p, axis=1, keepdims=True)
    return _dot(p.astype(BF16), v) / l


def _moba_kernel(q_ref, k_ref, v_ref, o_ref):
    t = ATTN_TILE
    seq = k_ref.shape[0]
    n_blocks = seq // t
    rows_pad = BF16_SUBLANES
    q = q_ref[...]
    k = k_ref[...]

    kmean = jnp.sum(k.astype(F32).reshape(n_blocks, t, HEAD_DIM), axis=1) * (1.0 / t)
    kmean = jnp.concatenate([kmean, jnp.zeros((rows_pad - n_blocks, HEAD_DIM), F32)], axis=0)
    kmean_hi = kmean.astype(BF16)
    kmean_lo = (kmean - kmean_hi.astype(F32)).astype(BF16)
    gate = _dot_nt(kmean_hi, q) + _dot_nt(kmean_lo, q)

    blk_id = lax.broadcasted_iota(jnp.int32, gate.shape, 0)
    q_blk = lax.shift_right_logical(lax.broadcasted_iota(jnp.int32, gate.shape, 1),
                                    int(math.log2(t)))
    rank = jnp.zeros(gate.shape, jnp.int32)
    for i in range(n_blocks):
        gi = gate[i:i + 1, :]
        beats = jnp.where(gi > gate, 1, jnp.where((gi == gate) & (i < blk_id), 1, 0))
        rank = rank + jnp.where(i < q_blk, beats, 0)
    keep = (blk_id == q_blk) | ((blk_id < q_blk) & (rank < MOBA_TOPK))
    bias = jnp.where(keep, 0.0, NEG_INF).astype(BF16)

    lane_id = lax.broadcasted_iota(jnp.int32, (rows_pad, LANES), 1)
    eye = jnp.where(lane_id == lax.broadcasted_iota(jnp.int32, (rows_pad, LANES), 0), 1.0, 0.0)
    bias_cols = _dot_tn(bias, eye.astype(BF16)).astype(BF16)
    key_blk = lax.shift_right_logical(lax.broadcasted_iota(jnp.int32, (seq, LANES), 0),
                                      int(math.log2(t)))
    one_hot = jnp.where(key_blk == lax.broadcasted_iota(jnp.int32, (seq, LANES), 1), 1.0, 0.0)
    q_aug = jnp.concatenate([q, bias_cols], axis=1)
    k_aug = jnp.concatenate([k, one_hot.astype(BF16)], axis=1)

    row, col = _causal_tile_mask()
    causal = col <= row
    to_log2 = SCORE_SCALE * LOG2_E
    tiles = [slice(qi * t, (qi + 1) * t) for qi in range(n_blocks)]
    s_own = [jnp.where(causal, _dot_nt(q_aug[r], k_aug[r]), NEG_INF) * to_log2 for r in tiles]
    s_past = [_dot_nt(q_aug[r], k_aug[:r.start]) * to_log2 for r in tiles[1:]]
    m = [jnp.max(x, axis=1, keepdims=True) for x in s_own]
    m = m[:1] + [jnp.maximum(mo, jnp.max(x, axis=1, keepdims=True)) for mo, x in zip(m[1:], s_past)]
    p_own = [jnp.exp2(x - mx) for x, mx in zip(s_own, m)]
    p_past = [jnp.exp2(x - mx) for x, mx in zip(s_past, m[1:])]
    l = [jnp.sum(x, axis=1, keepdims=True) for x in p_own]
    l = l[:1] + [lo + jnp.sum(x, axis=1, keepdims=True) for lo, x in zip(l[1:], p_past)]
    for qi, r in enumerate(tiles):
        pv = _dot(p_own[qi].astype(BF16), v_ref[r, :])
        if qi > 0:
            pv = pv + _dot(p_past[qi - 1].astype(BF16), v_ref[:r.start, :])
        o_ref[r, :] = (pv / l[qi]).astype(o_ref.dtype)


def _log2_keep(neg_z):
    neg_abs = lax.bitcast_convert_type(
        lax.bitcast_convert_type(neg_z, jnp.uint32) | jnp.uint32(0x80000000), F32)
    return jnp.minimum(neg_z, 0.0) - jnp.log2(1.0 + jnp.exp2(neg_abs))


def _split_hi_lo(x):
    hi = lax.bitcast_convert_type(
        lax.bitcast_convert_type(x, jnp.uint32) & jnp.uint32(0xFFFF0000), F32)
    return jnp.concatenate([hi.astype(BF16), (x - hi).astype(BF16)], axis=1)


def _stickbreak_kernel(q_ref, k_ref, v_ref, o_ref, acc_ref):
    t = ATTN_TILE
    seq = k_ref.shape[0]
    n_blocks = seq // t
    q = q_ref[...]
    k = k_ref[...]
    to_neg_log2 = -SCORE_SCALE * LOG2_E
    row, col = _causal_tile_mask()
    later = jnp.where(row > col, 1.0, 0.0).astype(BF16)
    later2 = jnp.concatenate([later, later], axis=0)

    def tile(x, i):
        return x[i * t:(i + 1) * t]

    neg_z = jnp.concatenate(
        [_dot_nt(tile(q, i), tile(k, i)) for i in range(n_blocks)]
        + [_dot_nt(tile(q, i), tile(k, i - 1)) for i in range(1, n_blocks)], axis=0) * to_neg_log2
    row_in_tile = lax.broadcasted_iota(jnp.int32, (seq, t), 0) & (t - 1)
    strict = lax.broadcasted_iota(jnp.int32, (seq, t), 1) < row_in_tile
    log_keep = _log2_keep(neg_z)
    log_keep = jnp.concatenate([jnp.where(strict, log_keep[:seq], 0.0), log_keep[seq:]], axis=0)
    after = _dot(_split_hi_lo(log_keep), later2)
    block_sum = after[:, 0:1] + log_keep[:, 0:1]
    log_w = log_keep - neg_z + after
    a_own = jnp.where(strict, jnp.exp2(log_w[:seq]), 0.0)
    a_prev = jnp.exp2(log_w[seq:] + block_sum[t:seq])
    carry = block_sum[t:seq] + block_sum[seq:]

    acc_ref[:t, :] = _dot(tile(a_own, 0).astype(BF16), v_ref[:t, :])
    for i in range(1, n_blocks):
        a = jnp.concatenate([tile(a_prev, i - 1), tile(a_own, i)], axis=1).astype(BF16)
        acc_ref[i * t:(i + 1) * t, :] = _dot(a, v_ref[(i - 1) * t:(i + 1) * t, :])

    far_needed = [jnp.max(tile(carry, i - 1)) >= UNDERFLOW_LOG2 for i in range(2, n_blocks)]
    for i in range(2, n_blocks):
        @pl.when(far_needed[i - 2])
        def _(i=i):
            n = i - 1
            neg_z = _dot_nt(tile(q, i), k[:n * t]) * to_neg_log2
            log_keep = _log2_keep(neg_z)
            stacked = jnp.concatenate([log_keep[:, j * t:(j + 1) * t] for j in range(n)], axis=0)
            after = _dot(_split_hi_lo(stacked), later2)
            c = tile(carry, i - 1)
            weights = [None] * n
            for j in range(n - 1, -1, -1):
                blk = slice(j * t, (j + 1) * t)
                weights[j] = jnp.exp2(log_keep[:, blk] - neg_z[:, blk] + after[blk] + c)
                c = c + after[blk][:, 0:1] + log_keep[:, j * t:j * t + 1]
            a = weights[0] if n == 1 else jnp.concatenate(weights, axis=1)
            acc_ref[i * t:(i + 1) * t, :] += _dot(a.astype(BF16), v_ref[:n * t, :])

    o_ref[...] = acc_ref[...].astype(o_ref.dtype)


def _dilated_log2_counts(seq):
    t = ATTN_TILE
    delta = np.arange(t)[:, None] - np.arange(seq)[None, :] + (seq - t)
    counts = np.zeros(delta.shape, np.float64)
    for window, dil in DILATED_PATTERNS:
        counts += (delta >= 0) & (delta % dil == 0) & (delta <= window)
    return np.where(counts > 0, np.log2(np.maximum(counts, 1.0)), NEG_INF).astype(np.float32)


def _dilated_kernel(q_ref, k_ref, v_ref, log2cnt_ref, o_ref):
    t = ATTN_TILE
    seq = k_ref.shape[0]
    n_blocks = seq // t
    q = q_ref[...]
    k = k_ref[...]
    widths = [(qi + 1) * t for qi in range(n_blocks)]
    s2 = [_dot_nt(q[w - t:w], k[:w]) * (SCORE_SCALE * LOG2_E) + log2cnt_ref[:, seq - w:]
          for w in widths]
    m = [jnp.max(x, axis=1, keepdims=True) for x in s2]
    p = [jnp.exp2(x - mx) for x, mx in zip(s2, m)]
    l = [jnp.sum(x, axis=1, keepdims=True) for x in p]
    for w, pw, lw in zip(widths, p, l):
        o_ref[w - t:w, :] = (_dot(pw.astype(BF16), v_ref[:w, :]) / lw).astype(o_ref.dtype)


def _mixers_kernel(qa_ref, ka_ref, va_ref, qb_ref, kb_ref, vb_ref, qd_ref, kd_ref, vd_ref,
                   log2cnt_ref, oa_ref, ob_ref, od_ref, acc_ref):
    _moba_kernel(qa_ref, ka_ref, va_ref, oa_ref)
    _dilated_kernel(qd_ref, kd_ref, vd_ref, log2cnt_ref, od_ref)
    _stickbreak_kernel(qb_ref, kb_ref, vb_ref, ob_ref, acc_ref)


def _attention_mixers(proj, log2cnt, *, batch, seq, n_heads):
    head_blk = (seq, HEAD_DIM)
    in_specs = [pl.BlockSpec(head_blk, lambda b, h, off=i * n_heads: (b, off + h))
                for i in range(9)]
    in_specs.append(pl.BlockSpec(log2cnt.shape, lambda b, h: (0, 0)))
    out_spec = pl.BlockSpec(head_blk, lambda b, h: (b, h))
    out_sds = jax.ShapeDtypeStruct((batch * seq, n_heads * HEAD_DIM), BF16)
    blk = 12 * _nbytes(head_blk, BF16) + _nbytes(log2cnt.shape, log2cnt.dtype)
    return pl.pallas_call(
        _mixers_kernel, grid=(batch, n_heads),
        in_specs=in_specs, out_specs=[out_spec] * 3, out_shape=[out_sds] * 3,
        scratch_shapes=[pltpu.VMEM(head_blk, F32)],
        compiler_params=_params(blk, scratch_bytes=_nbytes(head_blk, F32),
                                temp_bytes=8 * _nbytes((2 * seq, ATTN_TILE), F32)),
        name="attention_mixers",
    )(*([proj] * 9), log2cnt)


def _pool_kernel(u_ref, w_ref, scale_ref, o_ref):
    seq = u_ref.shape[0]
    g = pl.program_id(1)
    t_idx = lax.broadcasted_iota(jnp.int32, (seq, 1), 0)
    for gi, window in enumerate(POOL_WINDOWS):
        @pl.when(g == gi)
        def _(window=window):
            u = u_ref[...]
            win_sum = u
            span = 1
            while span < window:
                win_sum = win_sum + jnp.where(t_idx >= span, pltpu.roll(win_sum, span, axis=0), 0.0)
                span *= 2
            count = jnp.minimum(t_idx + 1, window).astype(F32)
            d = win_sum / count - u
            o_ref[...] = (_dot(d.astype(BF16), w_ref[...]) * scale_ref[...]).astype(o_ref.dtype)


def _multiscale_pool(u, w_pool, pool_scale, layer, *, batch, seq):
    _, n_groups, gd, _ = w_pool.shape
    assert all(w & (w - 1) == 0 for w in POOL_WINDOWS) and n_groups == len(POOL_WINDOWS)
    blk = _nbytes((seq, gd), F32) + _nbytes((seq, gd), BF16) + _nbytes((gd, gd), BF16)
    return pl.pallas_call(
        _pool_kernel, grid=(batch, n_groups),
        in_specs=[pl.BlockSpec((seq, gd), lambda b, g: (b, g)),
                  pl.BlockSpec((None, None, gd, gd), lambda b, g: (layer, g, 0, 0)),
                  pl.BlockSpec((1, gd), lambda b, g: (0, g))],
        out_specs=pl.BlockSpec((seq, gd), lambda b, g: (b, g)),
        out_shape=jax.ShapeDtypeStruct((u.shape[0], n_groups * gd), BF16),
        compiler_params=_params(blk, temp_bytes=6 * _nbytes((seq, gd), F32)),
        name="multiscale_pool",
    )(u, w_pool, pool_scale.reshape(1, n_groups * gd))


def kernel(x, ln_mix_pre, w_in, w_pool, pool_scale, mix_out_norm, w_out, ln_mix_post,
           ln_ffn_pre, w_gate, w_up, w_down, ln_ffn_post):
    batch, seq, d_model = x.shape
    depth = w_in.shape[0]
    group = w_out.shape[1] // N_MIXERS
    n_heads = group // HEAD_DIM
    n_blocks = seq // MOBA_BLOCK
    assert seq % ATTN_TILE == 0 and ATTN_TILE == MOBA_BLOCK and ATTN_TILE & (ATTN_TILE - 1) == 0
    assert MOBA_TOPK < n_blocks <= BF16_SUBLANES
    assert w_in.shape[2] == 10 * group

    w_pool = w_pool.astype(BF16)
    log2cnt = jnp.asarray(_dilated_log2_counts(seq))

    x = x.reshape(batch * seq, d_model)
    h = _rmsnorm_cast(x, ln_mix_pre[0])
    for l in range(depth):
        if l == 0:
            qkv, u, w_out = _in_proj(h, w_in, l, qkv_width=9 * group, tm=1024, tn=group // 2,
                                     also_round=w_out.reshape(-1, d_model))
            w_out = w_out.reshape(depth, -1, d_model)
        else:
            qkv, u = _in_proj(h, w_in, l, qkv_width=9 * group, tm=1024, tn=group // 2)
        y_a, y_b, y_d = _attention_mixers(qkv, log2cnt, batch=batch, seq=seq, n_heads=n_heads)
        y_c = _multiscale_pool(u, w_pool, pool_scale[l], l, batch=batch, seq=seq)
        m = _out_proj((y_a, y_b, y_c, y_d), mix_out_norm[l], w_out, l)
        x, h = _residual_norm(x, m, ln_mix_post[l], ln_ffn_pre[l])
        act, w_down_bf16 = _swiglu_up(h, w_gate, w_up, w_down, l)
        f = _matmul(act, w_down_bf16[None], 0, out_dtype=BRANCH_DTYPE, tm=512, tn=512,
                    name="ffn_down")
        last = l + 1 == depth
        x, h = _residual_norm(x, f, ln_ffn_post[l], ln_mix_pre[0 if last else l + 1],
                              emit_next=not last)
    return x.reshape(batch, seq, d_model)
```

```python
import functools
import math

import numpy as np
import jax
import jax.numpy as jnp
from jax import lax
from jax.experimental import pallas as pl
from jax.experimental.pallas import tpu as pltpu

HEAD_DIM = 128
N_MIXERS = 4
MOBA_BLOCK = 256
MOBA_TOPK = 3
POOL_WINDOWS = (2, 4, 8, 16)
DILATED_PATTERNS = ((128, 1), (512, 4), (2048, 16))
RMS_EPS = 1e-6
NEG_INF = -1e30

ATTN_TILE = 256
BF16_SUBLANES = 16
LANES = 128
V7X_VMEM_BYTES = 64 * 1024 * 1024
VMEM_LIMIT_CAP = V7X_VMEM_BYTES - 8 * 1024 * 1024
LOG2_E = math.log2(math.e)
UNDERFLOW_LOG2 = -160.0
SCORE_SCALE = HEAD_DIM ** -0.5
Q_SIGNS = (1.0, -1.0, 1.0)

F32 = jnp.float32
BF16 = jnp.bfloat16
BRANCH_DTYPE = BF16


def _nbytes(shape, dtype):
    return int(np.prod(shape)) * jnp.dtype(dtype).itemsize


def _params(block_bytes, scratch_bytes=0, temp_bytes=0, n_axes=2):
    limit = 2 * block_bytes + scratch_bytes + temp_bytes
    limit = min(max(limit, 16 * 1024 * 1024), VMEM_LIMIT_CAP)
    return pltpu.CompilerParams(
        dimension_semantics=("arbitrary",) * n_axes, vmem_limit_bytes=int(limit))


def _dot(a, b):
    return jnp.dot(a, b, preferred_element_type=F32)


def _dot_nt(a, b):
    return lax.dot_general(a, b, (((1,), (1,)), ((), ())), preferred_element_type=F32)


def _dot_tn(a, b):
    return lax.dot_general(a, b, (((0,), (0,)), ((), ())), preferred_element_type=F32)


def _rms_scale(x):
    return lax.rsqrt(jnp.mean(x * x, axis=-1, keepdims=True) + RMS_EPS)


def _rmsnorm_cast_kernel(x_ref, g_ref, o_ref):
    x = x_ref[...]
    o_ref[...] = (x * _rms_scale(x) * g_ref[...]).astype(o_ref.dtype)


def _rmsnorm_cast(x, g, *, tm=512):
    m, d = x.shape
    blk = _nbytes((tm, d), F32) + _nbytes((tm, d), BF16)
    return pl.pallas_call(
        _rmsnorm_cast_kernel, grid=(m // tm,),
        in_specs=[pl.BlockSpec((tm, d), lambda i: (i, 0)),
                  pl.BlockSpec((1, d), lambda i: (0, 0))],
        out_specs=pl.BlockSpec((tm, d), lambda i: (i, 0)),
        out_shape=jax.ShapeDtypeStruct((m, d), BF16),
        compiler_params=_params(blk, temp_bytes=_nbytes((tm, d), F32), n_axes=1),
        name="rmsnorm_cast",
    )(x, g.reshape(1, d))


def _residual_norm_kernel(x_ref, m_ref, g_post_ref, g_next_ref, xo_ref, ho_ref=None):
    m = m_ref[...].astype(F32)
    x = x_ref[...] + m * _rms_scale(m) * g_post_ref[...]
    xo_ref[...] = x
    if ho_ref is not None:
        ho_ref[...] = (x * _rms_scale(x) * g_next_ref[...]).astype(ho_ref.dtype)


def _residual_norm(x, m, g_post, g_next, *, emit_next=True, tm=256):
    rows, d = x.shape
    blk = (2 * _nbytes((tm, d), F32) + _nbytes((tm, d), m.dtype)
           + (_nbytes((tm, d), BF16) if emit_next else 0))
    row_spec = pl.BlockSpec((tm, d), lambda i: (i, 0))
    gain_spec = pl.BlockSpec((1, d), lambda i: (0, 0))
    out_shape = [jax.ShapeDtypeStruct((rows, d), F32)]
    if emit_next:
        out_shape.append(jax.ShapeDtypeStruct((rows, d), BF16))
    outs = pl.pallas_call(
        _residual_norm_kernel, grid=(rows // tm,),
        in_specs=[row_spec, row_spec, gain_spec, gain_spec],
        out_specs=[row_spec] * len(out_shape),
        out_shape=out_shape,
        compiler_params=_params(blk, temp_bytes=2 * _nbytes((tm, d), F32), n_axes=1),
        name="residual_norm",
    )(x, m, g_post.reshape(1, d), g_next.reshape(1, d))
    return (outs[0], outs[1]) if emit_next else (outs[0], None)


def _matmul_kernel(a_ref, w_ref, o_ref):
    o_ref[...] = _dot(a_ref[...], w_ref[...].astype(BF16)).astype(o_ref.dtype)


def _matmul(a, w, layer, *, out_dtype, tm, tn, col_block_offset=0, n_out=None, name="matmul"):
    m, k = a.shape
    n_out = w.shape[2] if n_out is None else n_out
    blk = _nbytes((tm, k), a.dtype) + _nbytes((k, tn), w.dtype) + _nbytes((tm, tn), out_dtype)
    blk += _nbytes((k, tn), BF16) // 2 if w.dtype != BF16 else 0
    return pl.pallas_call(
        _matmul_kernel, grid=(m // tm, n_out // tn),
        in_specs=[pl.BlockSpec((tm, k), lambda i, j: (i, 0)),
                  pl.BlockSpec((None, k, tn), lambda i, j: (layer, 0, j + col_block_offset))],
        out_specs=pl.BlockSpec((tm, tn), lambda i, j: (i, j)),
        out_shape=jax.ShapeDtypeStruct((m, n_out), out_dtype),
        compiler_params=_params(blk, temp_bytes=2 * _nbytes((tm, tn), F32)),
        name=name,
    )(a, w)


def _in_proj_kernel(h_ref, w_ref, *refs, q_blocks):
    if len(refs) == 4:
        side_ref, qkv_ref, u_ref, side_bf16_ref = refs
        side_bf16_ref[...] = side_ref[...].astype(side_bf16_ref.dtype)
    else:
        qkv_ref, u_ref = refs
    y = _dot(h_ref[...], w_ref[...].astype(BF16))
    j = pl.program_id(1)
    factor = jnp.float32(1.0)
    for (first, last), sign in zip(q_blocks, Q_SIGNS):
        factor = jnp.where((j >= first) & (j < last), sign * SCORE_SCALE * LOG2_E, factor)
    qkv_ref[...] = (y * factor).astype(qkv_ref.dtype)
    u_ref[...] = y


def _in_proj(h, w_in, layer, *, qkv_width, tm, tn, also_round=None, slab=64):
    m, k = h.shape
    n = w_in.shape[2]
    n_cols = n // tn
    group_blocks = qkv_width // (3 * len(Q_SIGNS)) // tn
    q_blocks = tuple((3 * i * group_blocks, (3 * i + 1) * group_blocks) for i in range(len(Q_SIGNS)))
    qkv_blocks = qkv_width // tn
    u_blocks = (n - qkv_width) // tn
    blk = (_nbytes((tm, k), BF16) + _nbytes((k, tn), w_in.dtype) + _nbytes((k, tn), BF16) // 2
           + _nbytes((tm, tn), BF16) + _nbytes((tm, tn), F32))
    in_specs = [pl.BlockSpec((tm, k), lambda i, j: (i, 0)),
                pl.BlockSpec((None, k, tn), lambda i, j: (layer, 0, j))]
    out_specs = [pl.BlockSpec((tm, tn), lambda i, j: (i, jnp.minimum(j, qkv_blocks))),
                 pl.BlockSpec((tm, tn), lambda i, j: (
                     i, jnp.where(j < qkv_blocks, u_blocks, j - qkv_blocks)))]
    out_shape = [jax.ShapeDtypeStruct((m, qkv_width + tn), BF16),
                 jax.ShapeDtypeStruct((m, n - qkv_width + tn), F32)]
    operands = [h, w_in]
    if also_round is not None:
        rows, cols = also_round.shape
        n_slabs = rows // slab
        assert n_slabs * slab == rows and n_slabs <= (m // tm) * n_cols
        side_spec = pl.BlockSpec((slab, cols),
                                 lambda i, j: (jnp.minimum(i * n_cols + j, n_slabs - 1), 0))
        in_specs.append(side_spec)
        out_specs.append(side_spec)
        out_shape.append(jax.ShapeDtypeStruct((rows, cols), BF16))
        operands.append(also_round)
        blk += _nbytes((slab, cols), F32) + _nbytes((slab, cols), BF16)
    return pl.pallas_call(
        functools.partial(_in_proj_kernel, q_blocks=q_blocks), grid=(m // tm, n_cols),
        in_specs=in_specs, out_specs=out_specs, out_shape=out_shape,
        compiler_params=_params(blk, temp_bytes=2 * _nbytes((tm, tn), F32)),
        name="in_proj",
    )(*operands)


def _out_proj_kernel(ya_ref, yb_ref, yc_ref, yd_ref, g_ref, w_ref, o_ref, hm_ref):
    @pl.when(pl.program_id(1) == 0)
    def _():
        gw = ya_ref.shape[1]
        for i, y_ref in enumerate((ya_ref, yb_ref, yc_ref, yd_ref)):
            y = y_ref[...].astype(F32)
            cols = slice(i * gw, (i + 1) * gw)
            hm_ref[:, cols] = (y * _rms_scale(y) * g_ref[:, cols]).astype(hm_ref.dtype)

    o_ref[...] = _dot(hm_ref[...], w_ref[...]).astype(o_ref.dtype)


def _out_proj(ys, g, w_out, layer, *, tm=1024, tn=512):
    rows, gw = ys[0].shape
    d = gw * len(ys)
    n = w_out.shape[2]
    blk = (len(ys) * _nbytes((tm, gw), ys[0].dtype) + _nbytes((d, tn), w_out.dtype)
           + _nbytes((tm, tn), BRANCH_DTYPE))
    y_spec = pl.BlockSpec((tm, gw), lambda i, j: (i, 0))
    return pl.pallas_call(
        _out_proj_kernel, grid=(rows // tm, n // tn),
        in_specs=[y_spec] * len(ys) + [pl.BlockSpec((1, d), lambda i, j: (0, 0)),
                                       pl.BlockSpec((None, d, tn), lambda i, j: (layer, 0, j))],
        out_specs=pl.BlockSpec((tm, tn), lambda i, j: (i, j)),
        out_shape=jax.ShapeDtypeStruct((rows, n), BRANCH_DTYPE),
        scratch_shapes=[pltpu.VMEM((tm, d), BF16)],
        compiler_params=_params(blk, scratch_bytes=_nbytes((tm, d), BF16),
                                temp_bytes=2 * _nbytes((tm, tn), F32)),
        name="out_proj",
    )(*ys, g.reshape(1, d), w_out)


def _swiglu_up_kernel(h_ref, wg_ref, wu_ref, wd_ref, o_ref, wd_bf16_ref):
    h = h_ref[...]
    g = _dot(h, wg_ref[...].astype(BF16))
    u = _dot(h, wu_ref[...].astype(BF16))
    o_ref[...] = (g / (1.0 + jnp.exp(-g)) * u).astype(o_ref.dtype)
    wd_bf16_ref[...] = wd_ref[...].astype(wd_bf16_ref.dtype)


def _swiglu_up(h, wg, wu, wd, layer, *, tm=1024, tn=256):
    m, k = h.shape
    n = wg.shape[2]
    grid = (m // tm, n // tn)
    d_ff, d_out = wd.shape[1:]
    slab = d_ff // (grid[0] * grid[1])
    assert slab * grid[0] * grid[1] == d_ff and slab % BF16_SUBLANES == 0
    blk = (_nbytes((tm, k), BF16) + 2 * _nbytes((k, tn), wg.dtype) + _nbytes((k, tn), BF16)
           + _nbytes((tm, tn), BF16) + _nbytes((slab, d_out), F32) + _nbytes((slab, d_out), BF16))
    w_spec = pl.BlockSpec((None, k, tn), lambda i, j: (layer, 0, j))
    return pl.pallas_call(
        _swiglu_up_kernel, grid=grid,
        in_specs=[pl.BlockSpec((tm, k), lambda i, j: (i, 0)), w_spec, w_spec,
                  pl.BlockSpec((None, slab, d_out), lambda i, j: (layer, i * grid[1] + j, 0))],
        out_specs=[pl.BlockSpec((tm, tn), lambda i, j: (i, j)),
                   pl.BlockSpec((slab, d_out), lambda i, j: (i * grid[1] + j, 0))],
        out_shape=[jax.ShapeDtypeStruct((m, n), BF16), jax.ShapeDtypeStruct((d_ff, d_out), BF16)],
        compiler_params=_params(blk, temp_bytes=3 * _nbytes((tm, tn), F32)),
        name="swiglu_up",
    )(h, wg, wu, wd)


def _causal_tile_mask():
    row = lax.broadcasted_iota(jnp.int32, (ATTN_TILE, ATTN_TILE), 0)
    col = lax.broadcasted_iota(jnp.int32, (ATTN_TILE, ATTN_TILE), 1)
    return row, col


def _with_ones(v):
    return jnp.concatenate([v, jnp.ones((v.shape[0], LANES), v.dtype)], axis=1)


def _moba_kernel(q_ref, k_ref, v_ref, o_ref):
    t = ATTN_TILE
    seq = k_ref.shape[0]
    n_blocks = seq // t
    rows_pad = BF16_SUBLANES
    q = q_ref[...]
    k = k_ref[...]

    kmean = jnp.sum(k.astype(F32).reshape(n_blocks, t, HEAD_DIM), axis=1) * (1.0 / t)
    kmean = jnp.concatenate([kmean, jnp.zeros((rows_pad - n_blocks, HEAD_DIM), F32)], axis=0)
    kmean_hi = kmean.astype(BF16)
    kmean_lo = (kmean - kmean_hi.astype(F32)).astype(BF16)
    gate = _dot_nt(kmean_hi, q) + _dot_nt(kmean_lo, q)

    blk_id = lax.broadcasted_iota(jnp.int32, gate.shape, 0)
    q_blk = lax.shift_right_logical(lax.broadcasted_iota(jnp.int32, gate.shape, 1),
                                    int(math.log2(t)))
    rank = jnp.zeros(gate.shape, jnp.int32)
    for i in range(n_blocks):
        gi = gate[i:i + 1, :]
        beats = jnp.where(gi > gate, 1, jnp.where((gi == gate) & (i < blk_id), 1, 0))
        rank = rank + jnp.where(i < q_blk, beats, 0)
    keep = (blk_id == q_blk) | ((blk_id < q_blk) & (rank < MOBA_TOPK))
    bias = jnp.where(keep, 0.0, NEG_INF).astype(BF16)

    lane_id = lax.broadcasted_iota(jnp.int32, (rows_pad, LANES), 1)
    eye = jnp.where(lane_id == lax.broadcasted_iota(jnp.int32, (rows_pad, LANES), 0), 1.0, 0.0)
    bias_cols = _dot_tn(bias, eye.astype(BF16)).astype(BF16)
    key_blk = lax.shift_right_logical(lax.broadcasted_iota(jnp.int32, (seq, LANES), 0),
                                      int(math.log2(t)))
    one_hot = jnp.where(key_blk == lax.broadcasted_iota(jnp.int32, (seq, LANES), 1), 1.0, 0.0)
    q_aug = jnp.concatenate([q, bias_cols], axis=1)
    k_aug = jnp.concatenate([k, one_hot.astype(BF16)], axis=1)

    row, col = _causal_tile_mask()
    causal = col <= row
    tiles = [slice(qi * t, (qi + 1) * t) for qi in range(n_blocks)]
    s_own = [jnp.where(causal, _dot_nt(q_aug[r], k_aug[r]), NEG_INF) for r in tiles]
    s_past = [_dot_nt(q_aug[r], k_aug[:r.start]) for r in tiles[1:]]
    m = [jnp.max(x, axis=1, keepdims=True) for x in s_own]
    m = m[:1] + [jnp.maximum(mo, jnp.max(x, axis=1, keepdims=True)) for mo, x in zip(m[1:], s_past)]
    p_own = [jnp.exp2(x - mx) for x, mx in zip(s_own, m)]
    p_past = [jnp.exp2(x - mx) for x, mx in zip(s_past, m[1:])]
    v_aug = _with_ones(v_ref[...])
    for qi, r in enumerate(tiles):
        pv = _dot(p_own[qi].astype(BF16), v_aug[r])
        if qi > 0:
            pv = pv + _dot(p_past[qi - 1].astype(BF16), v_aug[:r.start])
        o_ref[r, :] = (pv[:, :HEAD_DIM] / pv[:, HEAD_DIM:]).astype(o_ref.dtype)


def _log2_keep(neg_z):
    neg_abs = lax.bitcast_convert_type(
        lax.bitcast_convert_type(neg_z, jnp.uint32) | jnp.uint32(0x80000000), F32)
    return jnp.minimum(neg_z, 0.0) - jnp.log2(1.0 + jnp.exp2(neg_abs))


def _split_hi_lo(x):
    hi = lax.bitcast_convert_type(
        lax.bitcast_convert_type(x, jnp.uint32) & jnp.uint32(0xFFFF0000), F32)
    return jnp.concatenate([hi.astype(BF16), (x - hi).astype(BF16)], axis=1)


def _stickbreak_kernel(q_ref, k_ref, v_ref, o_ref, acc_ref):
    t = ATTN_TILE
    seq = k_ref.shape[0]
    n_blocks = seq // t
    q = q_ref[...]
    k = k_ref[...]
    row, col = _causal_tile_mask()
    later = jnp.where(row > col, 1.0, 0.0).astype(BF16)
    later2 = jnp.concatenate([later, later], axis=0)

    def tile(x, i):
        return x[i * t:(i + 1) * t]

    neg_z = jnp.concatenate(
        [_dot_nt(tile(q, i), tile(k, i)) for i in range(n_blocks)]
        + [_dot_nt(tile(q, i), tile(k, i - 1)) for i in range(1, n_blocks)], axis=0)
    row_in_tile = lax.broadcasted_iota(jnp.int32, (seq, t), 0) & (t - 1)
    strict = lax.broadcasted_iota(jnp.int32, (seq, t), 1) < row_in_tile
    log_keep = _log2_keep(neg_z)
    log_keep = jnp.concatenate([jnp.where(strict, log_keep[:seq], 0.0), log_keep[seq:]], axis=0)
    after = _dot(_split_hi_lo(log_keep), later2)
    block_sum = after[:, 0:1] + log_keep[:, 0:1]
    log_w = log_keep - neg_z + after
    a_own = jnp.where(strict, jnp.exp2(log_w[:seq]), 0.0)
    a_prev = jnp.exp2(log_w[seq:] + block_sum[t:seq])
    carry = block_sum[t:seq] + block_sum[seq:]

    acc_ref[:t, :] = _dot(tile(a_own, 0).astype(BF16), v_ref[:t, :])
    for i in range(1, n_blocks):
        a = jnp.concatenate([tile(a_prev, i - 1), tile(a_own, i)], axis=1).astype(BF16)
        acc_ref[i * t:(i + 1) * t, :] = _dot(a, v_ref[(i - 1) * t:(i + 1) * t, :])

    far_needed = [jnp.max(tile(carry, i - 1)) >= UNDERFLOW_LOG2 for i in range(2, n_blocks)]
    for i in range(2, n_blocks):
        @pl.when(far_needed[i - 2])
        def _(i=i):
            n = i - 1
            neg_z = _dot_nt(tile(q, i), k[:n * t])
            log_keep = _log2_keep(neg_z)
            stacked = jnp.concatenate([log_keep[:, j * t:(j + 1) * t] for j in range(n)], axis=0)
            after = _dot(_split_hi_lo(stacked), later2)
            c = tile(carry, i - 1)
            weights = [None] * n
            for j in range(n - 1, -1, -1):
                blk = slice(j * t, (j + 1) * t)
                weights[j] = jnp.exp2(log_keep[:, blk] - neg_z[:, blk] + after[blk] + c)
                c = c + after[blk][:, 0:1] + log_keep[:, j * t:j * t + 1]
            a = weights[0] if n == 1 else jnp.concatenate(weights, axis=1)
            acc_ref[i * t:(i + 1) * t, :] += _dot(a.astype(BF16), v_ref[:n * t, :])

    o_ref[...] = acc_ref[...].astype(o_ref.dtype)


def _dilated_log2_counts(seq):
    t = ATTN_TILE
    delta = np.arange(t)[:, None] - np.arange(seq)[None, :] + (seq - t)
    counts = np.zeros(delta.shape, np.float64)
    for window, dil in DILATED_PATTERNS:
        counts += (delta >= 0) & (delta % dil == 0) & (delta <= window)
    return np.where(counts > 0, np.log2(np.maximum(counts, 1.0)), NEG_INF).astype(np.float32)


def _dilated_kernel(q_ref, k_ref, v_ref, log2cnt_ref, o_ref):
    t = ATTN_TILE
    seq = k_ref.shape[0]
    n_blocks = seq // t
    q = q_ref[...]
    k = k_ref[...]
    widths = [(qi + 1) * t for qi in range(n_blocks)]
    s2 = [_dot_nt(q[w - t:w], k[:w]) + log2cnt_ref[:, seq - w:]
          for w in widths]
    m = [jnp.max(x, axis=1, keepdims=True) for x in s2]
    p = [jnp.exp2(x - mx) for x, mx in zip(s2, m)]
    v_aug = _with_ones(v_ref[...])
    for w, pw in zip(widths, p):
        pv = _dot(pw.astype(BF16), v_aug[:w])
        o_ref[w - t:w, :] = (pv[:, :HEAD_DIM] / pv[:, HEAD_DIM:]).astype(o_ref.dtype)


def _mixers_kernel(qa_ref, ka_ref, va_ref, qb_ref, kb_ref, vb_ref, qd_ref, kd_ref, vd_ref,
                   log2cnt_ref, oa_ref, ob_ref, od_ref, acc_ref):
    _moba_kernel(qa_ref, ka_ref, va_ref, oa_ref)
    _dilated_kernel(qd_ref, kd_ref, vd_ref, log2cnt_ref, od_ref)
    _stickbreak_kernel(qb_ref, kb_ref, vb_ref, ob_ref, acc_ref)


def _attention_mixers(proj, log2cnt, *, batch, seq, n_heads):
    head_blk = (seq, HEAD_DIM)
    in_specs = [pl.BlockSpec(head_blk, lambda b, h, off=i * n_heads: (b, off + h))
                for i in range(9)]
    in_specs.append(pl.BlockSpec(log2cnt.shape, lambda b, h: (0, 0)))
    out_spec = pl.BlockSpec(head_blk, lambda b, h: (b, h))
    out_sds = jax.ShapeDtypeStruct((batch * seq, n_heads * HEAD_DIM), BF16)
    blk = 12 * _nbytes(head_blk, BF16) + _nbytes(log2cnt.shape, log2cnt.dtype)
    return pl.pallas_call(
        _mixers_kernel, grid=(batch, n_heads),
        in_specs=in_specs, out_specs=[out_spec] * 3, out_shape=[out_sds] * 3,
        scratch_shapes=[pltpu.VMEM(head_blk, F32)],
        compiler_params=_params(blk, scratch_bytes=_nbytes(head_blk, F32),
                                temp_bytes=8 * _nbytes((2 * seq, ATTN_TILE), F32)),
        name="attention_mixers",
    )(*([proj] * 9), log2cnt)


def _pool_kernel(u_ref, w_ref, scale_ref, o_ref):
    seq = u_ref.shape[0]
    g = pl.program_id(1)
    t_idx = lax.broadcasted_iota(jnp.int32, (seq, 1), 0)
    for gi, window in enumerate(POOL_WINDOWS):
        @pl.when(g == gi)
        def _(window=window):
            u = u_ref[...]
            win_sum = u
            span = 1
            while span < window:
                win_sum = win_sum + jnp.where(t_idx >= span, pltpu.roll(win_sum, span, axis=0), 0.0)
                span *= 2
            count = jnp.minimum(t_idx + 1, window).astype(F32)
            d = win_sum / count - u
            o_ref[...] = (_dot(d.astype(BF16), w_ref[...]) * scale_ref[...]).astype(o_ref.dtype)


def _multiscale_pool(u, w_pool, pool_scale, layer, *, batch, seq):
    _, n_groups, gd, _ = w_pool.shape
    assert all(w & (w - 1) == 0 for w in POOL_WINDOWS) and n_groups == len(POOL_WINDOWS)
    blk = _nbytes((seq, gd), F32) + _nbytes((seq, gd), BF16) + _nbytes((gd, gd), BF16)
    return pl.pallas_call(
        _pool_kernel, grid=(batch, n_groups),
        in_specs=[pl.BlockSpec((seq, gd), lambda b, g: (b, g)),
                  pl.BlockSpec((None, None, gd, gd), lambda b, g: (layer, g, 0, 0)),
                  pl.BlockSpec((1, gd), lambda b, g: (0, g))],
        out_specs=pl.BlockSpec((seq, gd), lambda b, g: (b, g)),
        out_shape=jax.ShapeDtypeStruct((u.shape[0], n_groups * gd), BF16),
        compiler_params=_params(blk, temp_bytes=6 * _nbytes((seq, gd), F32)),
        name="multiscale_pool",
    )(u, w_pool, pool_scale.reshape(1, n_groups * gd))


def kernel(x, ln_mix_pre, w_in, w_pool, pool_scale, mix_out_norm, w_out, ln_mix_post,
           ln_ffn_pre, w_gate, w_up, w_down, ln_ffn_post):
    batch, seq, d_model = x.shape
    depth = w_in.shape[0]
    group = w_out.shape[1] // N_MIXERS
    n_heads = group // HEAD_DIM
    n_blocks = seq // MOBA_BLOCK
    assert seq % ATTN_TILE == 0 and ATTN_TILE == MOBA_BLOCK and ATTN_TILE & (ATTN_TILE - 1) == 0
    assert MOBA_TOPK < n_blocks <= BF16_SUBLANES
    assert w_in.shape[2] == 10 * group

    w_pool = w_pool.astype(BF16)
    log2cnt = jnp.asarray(_dilated_log2_counts(seq))

    x = x.reshape(batch * seq, d_model)
    h = _rmsnorm_cast(x, ln_mix_pre[0])
    for l in range(depth):
        if l == 0:
            qkv, u, w_out = _in_proj(h, w_in, l, qkv_width=9 * group, tm=1024, tn=group // 2,
                                     also_round=w_out.reshape(-1, d_model))
            w_out = w_out.reshape(depth, -1, d_model)
        else:
            qkv, u = _in_proj(h, w_in, l, qkv_width=9 * group, tm=1024, tn=group // 2)
        y_a, y_b, y_d = _attention_mixers(qkv, log2cnt, batch=batch, seq=seq, n_heads=n_heads)
        y_c = _multiscale_pool(u, w_pool, pool_scale[l], l, batch=batch, seq=seq)
        m = _out_proj((y_a, y_b, y_c, y_d), mix_out_norm[l], w_out, l)
        x, h = _residual_norm(x, m, ln_mix_post[l], ln_ffn_pre[l])
        act, w_down_bf16 = _swiglu_up(h, w_gate, w_up, w_down, l)
        f = _matmul(act, w_down_bf16[None], 0, out_dtype=BRANCH_DTYPE, tm=512, tn=512,
                    name="ffn_down")
        last = l + 1 == depth
        x, h = _residual_norm(x, f, ln_ffn_post[l], ln_mix_pre[0 if last else l + 1],
                              emit_next=not last)
    return x.reshape(batch, seq, d_model)
```

```python
import functools
import math

import numpy as np
import jax
import jax.numpy as jnp
from jax import lax
from jax.experimental import pallas as pl
from jax.experimental.pallas import tpu as pltpu

HEAD_DIM = 128
N_MIXERS = 4
MOBA_BLOCK = 256
MOBA_TOPK = 3
POOL_WINDOWS = (2, 4, 8, 16)
DILATED_PATTERNS = ((128, 1), (512, 4), (2048, 16))
RMS_EPS = 1e-6
NEG_INF = -1e30

ATTN_TILE = 256
SB_BLOCK = 128
SB_NEAR_BLOCKS = 3
BF16_SUBLANES = 16
LANES = 128
V7X_VMEM_BYTES = 64 * 1024 * 1024
VMEM_LIMIT_CAP = V7X_VMEM_BYTES - 8 * 1024 * 1024
LOG2_E = math.log2(math.e)
UNDERFLOW_LOG2 = -160.0
SCORE_SCALE = HEAD_DIM ** -0.5
Q_SIGNS = (1.0, -1.0, 1.0)

F32 = jnp.float32
BF16 = jnp.bfloat16
BRANCH_DTYPE = BF16


def _nbytes(shape, dtype):
    return int(np.prod(shape)) * jnp.dtype(dtype).itemsize


def _params(block_bytes, scratch_bytes=0, temp_bytes=0, n_axes=2):
    limit = 2 * block_bytes + scratch_bytes + temp_bytes
    limit = min(max(limit, 16 * 1024 * 1024), VMEM_LIMIT_CAP)
    return pltpu.CompilerParams(
        dimension_semantics=("arbitrary",) * n_axes, vmem_limit_bytes=int(limit))


def _dot(a, b):
    return jnp.dot(a, b, preferred_element_type=F32)


def _dot_nt(a, b):
    return lax.dot_general(a, b, (((1,), (1,)), ((), ())), preferred_element_type=F32)


def _dot_tn(a, b):
    return lax.dot_general(a, b, (((0,), (0,)), ((), ())), preferred_element_type=F32)


def _rms_scale(x):
    return lax.rsqrt(jnp.mean(x * x, axis=-1, keepdims=True) + RMS_EPS)


def _rmsnorm_cast_kernel(x_ref, g_ref, o_ref):
    x = x_ref[...]
    o_ref[...] = (x * _rms_scale(x) * g_ref[...]).astype(o_ref.dtype)


def _rmsnorm_cast(x, g, *, tm=512):
    m, d = x.shape
    blk = _nbytes((tm, d), F32) + _nbytes((tm, d), BF16)
    return pl.pallas_call(
        _rmsnorm_cast_kernel, grid=(m // tm,),
        in_specs=[pl.BlockSpec((tm, d), lambda i: (i, 0)),
                  pl.BlockSpec((1, d), lambda i: (0, 0))],
        out_specs=pl.BlockSpec((tm, d), lambda i: (i, 0)),
        out_shape=jax.ShapeDtypeStruct((m, d), BF16),
        compiler_params=_params(blk, temp_bytes=_nbytes((tm, d), F32), n_axes=1),
        name="rmsnorm_cast",
    )(x, g.reshape(1, d))


def _residual_norm_kernel(x_ref, m_ref, g_post_ref, g_next_ref, xo_ref, ho_ref=None):
    m = m_ref[...].astype(F32)
    x = x_ref[...] + m * _rms_scale(m) * g_post_ref[...]
    xo_ref[...] = x
    if ho_ref is not None:
        ho_ref[...] = (x * _rms_scale(x) * g_next_ref[...]).astype(ho_ref.dtype)


def _residual_norm(x, m, g_post, g_next, *, emit_next=True, tm=256):
    rows, d = x.shape
    blk = (2 * _nbytes((tm, d), F32) + _nbytes((tm, d), m.dtype)
           + (_nbytes((tm, d), BF16) if emit_next else 0))
    row_spec = pl.BlockSpec((tm, d), lambda i: (i, 0))
    gain_spec = pl.BlockSpec((1, d), lambda i: (0, 0))
    out_shape = [jax.ShapeDtypeStruct((rows, d), F32)]
    if emit_next:
        out_shape.append(jax.ShapeDtypeStruct((rows, d), BF16))
    outs = pl.pallas_call(
        _residual_norm_kernel, grid=(rows // tm,),
        in_specs=[row_spec, row_spec, gain_spec, gain_spec],
        out_specs=[row_spec] * len(out_shape),
        out_shape=out_shape,
        compiler_params=_params(blk, temp_bytes=2 * _nbytes((tm, d), F32), n_axes=1),
        name="residual_norm",
    )(x, m, g_post.reshape(1, d), g_next.reshape(1, d))
    return (outs[0], outs[1]) if emit_next else (outs[0], None)


def _matmul_kernel(a_ref, w_ref, o_ref):
    o_ref[...] = _dot(a_ref[...], w_ref[...].astype(BF16)).astype(o_ref.dtype)


def _matmul(a, w, layer, *, out_dtype, tm, tn, col_block_offset=0, n_out=None, name="matmul"):
    m, k = a.shape
    n_out = w.shape[2] if n_out is None else n_out
    blk = _nbytes((tm, k), a.dtype) + _nbytes((k, tn), w.dtype) + _nbytes((tm, tn), out_dtype)
    blk += _nbytes((k, tn), BF16) // 2 if w.dtype != BF16 else 0
    return pl.pallas_call(
        _matmul_kernel, grid=(m // tm, n_out // tn),
        in_specs=[pl.BlockSpec((tm, k), lambda i, j: (i, 0)),
                  pl.BlockSpec((None, k, tn), lambda i, j: (layer, 0, j + col_block_offset))],
        out_specs=pl.BlockSpec((tm, tn), lambda i, j: (i, j)),
        out_shape=jax.ShapeDtypeStruct((m, n_out), out_dtype),
        compiler_params=_params(blk, temp_bytes=2 * _nbytes((tm, tn), F32)),
        name=name,
    )(a, w)


def _in_proj_kernel(h_ref, w_ref, *refs, q_blocks):
    if len(refs) == 4:
        side_ref, qkv_ref, u_ref, side_bf16_ref = refs
        side_bf16_ref[...] = side_ref[...].astype(side_bf16_ref.dtype)
    else:
        qkv_ref, u_ref = refs
    y = _dot(h_ref[...], w_ref[...].astype(BF16))
    j = pl.program_id(1)
    factor = jnp.float32(1.0)
    for (first, last), sign in zip(q_blocks, Q_SIGNS):
        factor = jnp.where((j >= first) & (j < last), sign * SCORE_SCALE * LOG2_E, factor)
    qkv_ref[...] = (y * factor).astype(qkv_ref.dtype)
    u_ref[...] = y


def _in_proj(h, w_in, layer, *, qkv_width, tm, tn, also_round=None, slab=64):
    m, k = h.shape
    n = w_in.shape[2]
    n_cols = n // tn
    group_blocks = qkv_width // (3 * len(Q_SIGNS)) // tn
    q_blocks = tuple((3 * i * group_blocks, (3 * i + 1) * group_blocks) for i in range(len(Q_SIGNS)))
    qkv_blocks = qkv_width // tn
    u_blocks = (n - qkv_width) // tn
    blk = (_nbytes((tm, k), BF16) + _nbytes((k, tn), w_in.dtype) + _nbytes((k, tn), BF16) // 2
           + _nbytes((tm, tn), BF16) + _nbytes((tm, tn), F32))
    in_specs = [pl.BlockSpec((tm, k), lambda i, j: (i, 0)),
                pl.BlockSpec((None, k, tn), lambda i, j: (layer, 0, j))]
    out_specs = [pl.BlockSpec((tm, tn), lambda i, j: (i, jnp.minimum(j, qkv_blocks))),
                 pl.BlockSpec((tm, tn), lambda i, j: (
                     i, jnp.where(j < qkv_blocks, u_blocks, j - qkv_blocks)))]
    out_shape = [jax.ShapeDtypeStruct((m, qkv_width + tn), BF16),
                 jax.ShapeDtypeStruct((m, n - qkv_width + tn), F32)]
    operands = [h, w_in]
    if also_round is not None:
        rows, cols = also_round.shape
        n_slabs = rows // slab
        assert n_slabs * slab == rows and n_slabs <= (m // tm) * n_cols
        side_spec = pl.BlockSpec((slab, cols),
                                 lambda i, j: (jnp.minimum(i * n_cols + j, n_slabs - 1), 0))
        in_specs.append(side_spec)
        out_specs.append(side_spec)
        out_shape.append(jax.ShapeDtypeStruct((rows, cols), BF16))
        operands.append(also_round)
        blk += _nbytes((slab, cols), F32) + _nbytes((slab, cols), BF16)
    return pl.pallas_call(
        functools.partial(_in_proj_kernel, q_blocks=q_blocks), grid=(m // tm, n_cols),
        in_specs=in_specs, out_specs=out_specs, out_shape=out_shape,
        compiler_params=_params(blk, temp_bytes=2 * _nbytes((tm, tn), F32)),
        name="in_proj",
    )(*operands)


def _out_proj_kernel(ya_ref, yb_ref, yc_ref, yd_ref, g_ref, w_ref, o_ref, hm_ref):
    @pl.when(pl.program_id(1) == 0)
    def _():
        gw = ya_ref.shape[1]
        for i, y_ref in enumerate((ya_ref, yb_ref, yc_ref, yd_ref)):
            y = y_ref[...].astype(F32)
            cols = slice(i * gw, (i + 1) * gw)
            hm_ref[:, cols] = (y * _rms_scale(y) * g_ref[:, cols]).astype(hm_ref.dtype)

    o_ref[...] = _dot(hm_ref[...], w_ref[...]).astype(o_ref.dtype)


def _out_proj(ys, g, w_out, layer, *, tm=1024, tn=512):
    rows, gw = ys[0].shape
    d = gw * len(ys)
    n = w_out.shape[2]
    blk = (len(ys) * _nbytes((tm, gw), ys[0].dtype) + _nbytes((d, tn), w_out.dtype)
           + _nbytes((tm, tn), BRANCH_DTYPE))
    y_spec = pl.BlockSpec((tm, gw), lambda i, j: (i, 0))
    return pl.pallas_call(
        _out_proj_kernel, grid=(rows // tm, n // tn),
        in_specs=[y_spec] * len(ys) + [pl.BlockSpec((1, d), lambda i, j: (0, 0)),
                                       pl.BlockSpec((None, d, tn), lambda i, j: (layer, 0, j))],
        out_specs=pl.BlockSpec((tm, tn), lambda i, j: (i, j)),
        out_shape=jax.ShapeDtypeStruct((rows, n), BRANCH_DTYPE),
        scratch_shapes=[pltpu.VMEM((tm, d), BF16)],
        compiler_params=_params(blk, scratch_bytes=_nbytes((tm, d), BF16),
                                temp_bytes=2 * _nbytes((tm, tn), F32)),
        name="out_proj",
    )(*ys, g.reshape(1, d), w_out)


def _swiglu_up_kernel(h_ref, wg_ref, wu_ref, wd_ref, o_ref, wd_bf16_ref):
    h = h_ref[...]
    g = _dot(h, wg_ref[...].astype(BF16))
    u = _dot(h, wu_ref[...].astype(BF16))
    o_ref[...] = (g / (1.0 + jnp.exp(-g)) * u).astype(o_ref.dtype)
    wd_bf16_ref[...] = wd_ref[...].astype(wd_bf16_ref.dtype)


def _swiglu_up(h, wg, wu, wd, layer, *, tm=1024, tn=256):
    m, k = h.shape
    n = wg.shape[2]
    grid = (m // tm, n // tn)
    d_ff, d_out = wd.shape[1:]
    slab = d_ff // (grid[0] * grid[1])
    assert slab * grid[0] * grid[1] == d_ff and slab % BF16_SUBLANES == 0
    blk = (_nbytes((tm, k), BF16) + 2 * _nbytes((k, tn), wg.dtype) + _nbytes((k, tn), BF16)
           + _nbytes((tm, tn), BF16) + _nbytes((slab, d_out), F32) + _nbytes((slab, d_out), BF16))
    w_spec = pl.BlockSpec((None, k, tn), lambda i, j: (layer, 0, j))
    return pl.pallas_call(
        _swiglu_up_kernel, grid=grid,
        in_specs=[pl.BlockSpec((tm, k), lambda i, j: (i, 0)), w_spec, w_spec,
                  pl.BlockSpec((None, slab, d_out), lambda i, j: (layer, i * grid[1] + j, 0))],
        out_specs=[pl.BlockSpec((tm, tn), lambda i, j: (i, j)),
                   pl.BlockSpec((slab, d_out), lambda i, j: (i * grid[1] + j, 0))],
        out_shape=[jax.ShapeDtypeStruct((m, n), BF16), jax.ShapeDtypeStruct((d_ff, d_out), BF16)],
        compiler_params=_params(blk, temp_bytes=3 * _nbytes((tm, tn), F32)),
        name="swiglu_up",
    )(h, wg, wu, wd)


def _causal_tile_mask():
    row = lax.broadcasted_iota(jnp.int32, (ATTN_TILE, ATTN_TILE), 0)
    col = lax.broadcasted_iota(jnp.int32, (ATTN_TILE, ATTN_TILE), 1)
    return row, col


def _with_ones(v):
    return jnp.concatenate([v, jnp.ones((v.shape[0], LANES), v.dtype)], axis=1)


def _moba_kernel(q_ref, k_ref, v_ref, o_ref):
    t = ATTN_TILE
    seq = k_ref.shape[0]
    n_blocks = seq // t
    rows_pad = BF16_SUBLANES
    q = q_ref[...]
    k = k_ref[...]

    kmean = jnp.sum(k.astype(F32).reshape(n_blocks, t, HEAD_DIM), axis=1) * (1.0 / t)
    kmean = jnp.concatenate([kmean, jnp.zeros((rows_pad - n_blocks, HEAD_DIM), F32)], axis=0)
    kmean_hi = kmean.astype(BF16)
    kmean_lo = (kmean - kmean_hi.astype(F32)).astype(BF16)
    gate = _dot_nt(kmean_hi, q) + _dot_nt(kmean_lo, q)

    blk_id = lax.broadcasted_iota(jnp.int32, gate.shape, 0)
    q_blk = lax.shift_right_logical(lax.broadcasted_iota(jnp.int32, gate.shape, 1),
                                    int(math.log2(t)))
    rank = jnp.zeros(gate.shape, jnp.int32)
    for i in range(n_blocks):
        gi = gate[i:i + 1, :]
        beats = jnp.where(gi > gate, 1, jnp.where((gi == gate) & (i < blk_id), 1, 0))
        rank = rank + jnp.where(i < q_blk, beats, 0)
    keep = (blk_id == q_blk) | ((blk_id < q_blk) & (rank < MOBA_TOPK))
    bias = jnp.where(keep, 0.0, NEG_INF).astype(BF16)

    lane_id = lax.broadcasted_iota(jnp.int32, (rows_pad, LANES), 1)
    eye = jnp.where(lane_id == lax.broadcasted_iota(jnp.int32, (rows_pad, LANES), 0), 1.0, 0.0)
    bias_cols = _dot_tn(bias, eye.astype(BF16)).astype(BF16)
    key_blk = lax.shift_right_logical(lax.broadcasted_iota(jnp.int32, (seq, LANES), 0),
                                      int(math.log2(t)))
    one_hot = jnp.where(key_blk == lax.broadcasted_iota(jnp.int32, (seq, LANES), 1), 1.0, 0.0)
    q_aug = jnp.concatenate([q, bias_cols], axis=1)
    k_aug = jnp.concatenate([k, one_hot.astype(BF16)], axis=1)

    row, col = _causal_tile_mask()
    causal = col <= row
    tiles = [slice(qi * t, (qi + 1) * t) for qi in range(n_blocks)]
    s_own = [jnp.where(causal, _dot_nt(q_aug[r], k_aug[r]), NEG_INF) for r in tiles]
    s_past = [_dot_nt(q_aug[r], k_aug[:r.start]) for r in tiles[1:]]
    m = [jnp.max(x, axis=1, keepdims=True) for x in s_own]
    m = m[:1] + [jnp.maximum(mo, jnp.max(x, axis=1, keepdims=True)) for mo, x in zip(m[1:], s_past)]
    p_own = [jnp.exp2(x - mx) for x, mx in zip(s_own, m)]
    p_past = [jnp.exp2(x - mx) for x, mx in zip(s_past, m[1:])]
    v_aug = _with_ones(v_ref[...])
    for qi, r in enumerate(tiles):
        pv = _dot(p_own[qi].astype(BF16), v_aug[r])
        if qi > 0:
            pv = pv + _dot(p_past[qi - 1].astype(BF16), v_aug[:r.start])
        o_ref[r, :] = (pv[:, :HEAD_DIM] / pv[:, HEAD_DIM:]).astype(o_ref.dtype)


def _log2_keep(neg_z):
    neg_abs = lax.bitcast_convert_type(
        lax.bitcast_convert_type(neg_z, jnp.uint32) | jnp.uint32(0x80000000), F32)
    return jnp.minimum(neg_z, 0.0) - jnp.log2(1.0 + jnp.exp2(neg_abs))


def _split_hi_lo(x):
    hi = lax.bitcast_convert_type(
        lax.bitcast_convert_type(x, jnp.uint32) & jnp.uint32(0xFFFF0000), F32)
    return jnp.concatenate([hi.astype(BF16), (x - hi).astype(BF16)], axis=1)


def _stickbreak_kernel(q_ref, k_ref, v_ref, o_ref, acc_ref):
    t = SB_BLOCK
    near = SB_NEAR_BLOCKS
    seq = k_ref.shape[0]
    n_blocks = seq // t
    q = q_ref[...]
    k = k_ref[...]
    row = lax.broadcasted_iota(jnp.int32, (t, t), 0)
    col = lax.broadcasted_iota(jnp.int32, (t, t), 1)
    later = jnp.where(row > col, 1.0, 0.0).astype(BF16)
    later2 = jnp.concatenate([later, later], axis=0)

    def tile(x, i):
        return x[i * t:(i + 1) * t]

    groups = [jnp.concatenate([_dot_nt(tile(q, i), tile(k, i - g)) for i in range(g, n_blocks)],
                              axis=0) for g in range(near)]
    starts = [0]
    for g in range(near):
        starts.append(starts[-1] + (n_blocks - g) * t)
    neg_z = jnp.concatenate(groups, axis=0)
    row_in_tile = lax.broadcasted_iota(jnp.int32, (seq, t), 0) & (t - 1)
    strict = lax.broadcasted_iota(jnp.int32, (seq, t), 1) < row_in_tile
    log_keep = _log2_keep(neg_z)
    log_keep = jnp.concatenate([jnp.where(strict, log_keep[:seq], 0.0), log_keep[seq:]], axis=0)
    after = _dot(_split_hi_lo(log_keep), later2)
    block_sum = after[:, 0:1] + log_keep[:, 0:1]
    log_w = log_keep - neg_z + after
    weights = [jnp.where(strict, jnp.exp2(log_w[:seq]), 0.0)]
    carry = block_sum[:seq]
    for g in range(1, near):
        rows = slice(starts[g], starts[g + 1])
        carry = carry[t:]
        weights.append(jnp.exp2(log_w[rows] + carry))
        carry = carry + block_sum[rows]

    for i in range(n_blocks):
        back = min(i, near - 1)
        parts = [tile(weights[g], i - g) for g in range(back, -1, -1)]
        a = parts[0] if back == 0 else jnp.concatenate(parts, axis=1)
        acc_ref[i * t:(i + 1) * t, :] = _dot(a.astype(BF16), v_ref[(i - back) * t:(i + 1) * t, :])

    far_needed = [jnp.max(tile(carry, i - (near - 1))) >= UNDERFLOW_LOG2
                  for i in range(near, n_blocks)]
    for i in range(near, n_blocks):
        @pl.when(far_needed[i - near])
        def _(i=i):
            n = i - (near - 1)
            neg_z = _dot_nt(tile(q, i), k[:n * t])
            log_keep = _log2_keep(neg_z)
            stacked = jnp.concatenate([log_keep[:, j * t:(j + 1) * t] for j in range(n)], axis=0)
            after = _dot(_split_hi_lo(stacked), later2)
            c = tile(carry, i - (near - 1))
            weights = [None] * n
            for j in range(n - 1, -1, -1):
                blk = slice(j * t, (j + 1) * t)
                weights[j] = jnp.exp2(log_keep[:, blk] - neg_z[:, blk] + after[blk] + c)
                c = c + after[blk][:, 0:1] + log_keep[:, j * t:j * t + 1]
            a = weights[0] if n == 1 else jnp.concatenate(weights, axis=1)
            acc_ref[i * t:(i + 1) * t, :] += _dot(a.astype(BF16), v_ref[:n * t, :])

    o_ref[...] = acc_ref[...].astype(o_ref.dtype)


def _dilated_log2_counts(seq):
    t = ATTN_TILE
    delta = np.arange(t)[:, None] - np.arange(seq)[None, :] + (seq - t)
    counts = np.zeros(delta.shape, np.float64)
    for window, dil in DILATED_PATTERNS:
        counts += (delta >= 0) & (delta % dil == 0) & (delta <= window)
    return np.where(counts > 0, np.log2(np.maximum(counts, 1.0)), NEG_INF).astype(np.float32)


def _dilated_kernel(q_ref, k_ref, v_ref, log2cnt_ref, o_ref):
    t = ATTN_TILE
    seq = k_ref.shape[0]
    n_blocks = seq // t
    q = q_ref[...]
    k = k_ref[...]
    widths = [(qi + 1) * t for qi in range(n_blocks)]
    s2 = [_dot_nt(q[w - t:w], k[:w]) + log2cnt_ref[:, seq - w:]
          for w in widths]
    m = [jnp.max(x, axis=1, keepdims=True) for x in s2]
    p = [jnp.exp2(x - mx) for x, mx in zip(s2, m)]
    v_aug = _with_ones(v_ref[...])
    for w, pw in zip(widths, p):
        pv = _dot(pw.astype(BF16), v_aug[:w])
        o_ref[w - t:w, :] = (pv[:, :HEAD_DIM] / pv[:, HEAD_DIM:]).astype(o_ref.dtype)


def _mixers_kernel(qa_ref, ka_ref, va_ref, qb_ref, kb_ref, vb_ref, qd_ref, kd_ref, vd_ref,
                   log2cnt_ref, oa_ref, ob_ref, od_ref, acc_ref):
    _moba_kernel(qa_ref, ka_ref, va_ref, oa_ref)
    _dilated_kernel(qd_ref, kd_ref, vd_ref, log2cnt_ref, od_ref)
    _stickbreak_kernel(qb_ref, kb_ref, vb_ref, ob_ref, acc_ref)


def _attention_mixers(proj, log2cnt, *, batch, seq, n_heads):
    head_blk = (seq, HEAD_DIM)
    in_specs = [pl.BlockSpec(head_blk, lambda b, h, off=i * n_heads: (b, off + h))
                for i in range(9)]
    in_specs.append(pl.BlockSpec(log2cnt.shape, lambda b, h: (0, 0)))
    out_spec = pl.BlockSpec(head_blk, lambda b, h: (b, h))
    out_sds = jax.ShapeDtypeStruct((batch * seq, n_heads * HEAD_DIM), BF16)
    blk = 12 * _nbytes(head_blk, BF16) + _nbytes(log2cnt.shape, log2cnt.dtype)
    return pl.pallas_call(
        _mixers_kernel, grid=(batch, n_heads),
        in_specs=in_specs, out_specs=[out_spec] * 3, out_shape=[out_sds] * 3,
        scratch_shapes=[pltpu.VMEM(head_blk, F32)],
        compiler_params=_params(blk, scratch_bytes=_nbytes(head_blk, F32),
                                temp_bytes=8 * _nbytes((2 * seq, ATTN_TILE), F32)),
        name="attention_mixers",
    )(*([proj] * 9), log2cnt)


def _pool_kernel(u_ref, w_ref, scale_ref, o_ref):
    seq = u_ref.shape[0]
    g = pl.program_id(1)
    t_idx = lax.broadcasted_iota(jnp.int32, (seq, 1), 0)
    for gi, window in enumerate(POOL_WINDOWS):
        @pl.when(g == gi)
        def _(window=window):
            u = u_ref[...]
            win_sum = u
            span = 1
            while span < window:
                win_sum = win_sum + jnp.where(t_idx >= span, pltpu.roll(win_sum, span, axis=0), 0.0)
                span *= 2
            count = jnp.minimum(t_idx + 1, window).astype(F32)
            d = win_sum / count - u
            o_ref[...] = (_dot(d.astype(BF16), w_ref[...]) * scale_ref[...]).astype(o_ref.dtype)


def _multiscale_pool(u, w_pool, pool_scale, layer, *, batch, seq):
    _, n_groups, gd, _ = w_pool.shape
    assert all(w & (w - 1) == 0 for w in POOL_WINDOWS) and n_groups == len(POOL_WINDOWS)
    blk = _nbytes((seq, gd), F32) + _nbytes((seq, gd), BF16) + _nbytes((gd, gd), BF16)
    return pl.pallas_call(
        _pool_kernel, grid=(batch, n_groups),
        in_specs=[pl.BlockSpec((seq, gd), lambda b, g: (b, g)),
                  pl.BlockSpec((None, None, gd, gd), lambda b, g: (layer, g, 0, 0)),
                  pl.BlockSpec((1, gd), lambda b, g: (0, g))],
        out_specs=pl.BlockSpec((seq, gd), lambda b, g: (b, g)),
        out_shape=jax.ShapeDtypeStruct((u.shape[0], n_groups * gd), BF16),
        compiler_params=_params(blk, temp_bytes=6 * _nbytes((seq, gd), F32)),
        name="multiscale_pool",
    )(u, w_pool, pool_scale.reshape(1, n_groups * gd))


def kernel(x, ln_mix_pre, w_in, w_pool, pool_scale, mix_out_norm, w_out, ln_mix_post,
           ln_ffn_pre, w_gate, w_up, w_down, ln_ffn_post):
    batch, seq, d_model = x.shape
    depth = w_in.shape[0]
    group = w_out.shape[1] // N_MIXERS
    n_heads = group // HEAD_DIM
    n_blocks = seq // MOBA_BLOCK
    assert seq % ATTN_TILE == 0 and ATTN_TILE == MOBA_BLOCK and ATTN_TILE & (ATTN_TILE - 1) == 0
    assert MOBA_TOPK < n_blocks <= BF16_SUBLANES
    assert seq % SB_BLOCK == 0 and SB_BLOCK & (SB_BLOCK - 1) == 0 and SB_BLOCK % LANES == 0
    assert w_in.shape[2] == 10 * group

    w_pool = w_pool.astype(BF16)
    log2cnt = jnp.asarray(_dilated_log2_counts(seq))

    x = x.reshape(batch * seq, d_model)
    h = _rmsnorm_cast(x, ln_mix_pre[0])
    for l in range(depth):
        if l == 0:
            qkv, u, w_out = _in_proj(h, w_in, l, qkv_width=9 * group, tm=1024, tn=group // 2,
                                     also_round=w_out.reshape(-1, d_model))
            w_out = w_out.reshape(depth, -1, d_model)
        else:
            qkv, u = _in_proj(h, w_in, l, qkv_width=9 * group, tm=1024, tn=group // 2)
        y_a, y_b, y_d = _attention_mixers(qkv, log2cnt, batch=batch, seq=seq, n_heads=n_heads)
        y_c = _multiscale_pool(u, w_pool, pool_scale[l], l, batch=batch, seq=seq)
        m = _out_proj((y_a, y_b, y_c, y_d), mix_out_norm[l], w_out, l)
        x, h = _residual_norm(x, m, ln_mix_post[l], ln_ffn_pre[l])
        act, w_down_bf16 = _swiglu_up(h, w_gate, w_up, w_down, l)
        f = _matmul(act, w_down_bf16[None], 0, out_dtype=BRANCH_DTYPE, tm=512, tn=512,
                    name="ffn_down")
        last = l + 1 == depth
        x, h = _residual_norm(x, f, ln_ffn_post[l], ln_mix_pre[0 if last else l + 1],
                              emit_next=not last)
    return x.reshape(batch, seq, d_model)
```

```python
import functools
import math

import numpy as np
import jax
import jax.numpy as jnp
from jax import lax
from jax.experimental import pallas as pl
from jax.experimental.pallas import tpu as pltpu

HEAD_DIM = 128
N_MIXERS = 4
MOBA_BLOCK = 256
MOBA_TOPK = 3
POOL_WINDOWS = (2, 4, 8, 16)
DILATED_PATTERNS = ((128, 1), (512, 4), (2048, 16))
RMS_EPS = 1e-6
NEG_INF = -1e30

ATTN_TILE = 256
SB_BLOCK = 128
SB_NEAR_BLOCKS = 3
BF16_SUBLANES = 16
LANES = 128
V7X_VMEM_BYTES = 64 * 1024 * 1024
VMEM_LIMIT_CAP = V7X_VMEM_BYTES - 8 * 1024 * 1024
LOG2_E = math.log2(math.e)
UNDERFLOW_LOG2 = -160.0
SCORE_SCALE = HEAD_DIM ** -0.5
Q_SIGNS = (1.0, -1.0, 1.0)

F32 = jnp.float32
BF16 = jnp.bfloat16
BRANCH_DTYPE = BF16


def _nbytes(shape, dtype):
    return int(np.prod(shape)) * jnp.dtype(dtype).itemsize


def _params(block_bytes, scratch_bytes=0, temp_bytes=0, n_axes=2):
    limit = 2 * block_bytes + scratch_bytes + temp_bytes
    limit = min(max(limit, 16 * 1024 * 1024), VMEM_LIMIT_CAP)
    return pltpu.CompilerParams(
        dimension_semantics=("arbitrary",) * n_axes, vmem_limit_bytes=int(limit))


def _dot(a, b):
    return jnp.dot(a, b, preferred_element_type=F32)


def _dot_nt(a, b):
    return lax.dot_general(a, b, (((1,), (1,)), ((), ())), preferred_element_type=F32)


def _dot_tn(a, b):
    return lax.dot_general(a, b, (((0,), (0,)), ((), ())), preferred_element_type=F32)


def _rms_scale(x):
    return lax.rsqrt(jnp.mean(x * x, axis=-1, keepdims=True) + RMS_EPS)


def _rmsnorm_cast_kernel(x_ref, g_ref, o_ref):
    x = x_ref[...]
    o_ref[...] = (x * _rms_scale(x) * g_ref[...]).astype(o_ref.dtype)


def _rmsnorm_cast(x, g, *, tm=512):
    m, d = x.shape
    blk = _nbytes((tm, d), F32) + _nbytes((tm, d), BF16)
    return pl.pallas_call(
        _rmsnorm_cast_kernel, grid=(m // tm,),
        in_specs=[pl.BlockSpec((tm, d), lambda i: (i, 0)),
                  pl.BlockSpec((1, d), lambda i: (0, 0))],
        out_specs=pl.BlockSpec((tm, d), lambda i: (i, 0)),
        out_shape=jax.ShapeDtypeStruct((m, d), BF16),
        compiler_params=_params(blk, temp_bytes=_nbytes((tm, d), F32), n_axes=1),
        name="rmsnorm_cast",
    )(x, g.reshape(1, d))


def _residual_norm_kernel(x_ref, m_ref, g_post_ref, g_next_ref, xo_ref, ho_ref=None):
    m = m_ref[...].astype(F32)
    x = x_ref[...] + m * _rms_scale(m) * g_post_ref[...]
    xo_ref[...] = x
    if ho_ref is not None:
        ho_ref[...] = (x * _rms_scale(x) * g_next_ref[...]).astype(ho_ref.dtype)


def _residual_norm(x, m, g_post, g_next, *, emit_next=True, tm=256):
    rows, d = x.shape
    blk = (2 * _nbytes((tm, d), F32) + _nbytes((tm, d), m.dtype)
           + (_nbytes((tm, d), BF16) if emit_next else 0))
    row_spec = pl.BlockSpec((tm, d), lambda i: (i, 0))
    gain_spec = pl.BlockSpec((1, d), lambda i: (0, 0))
    out_shape = [jax.ShapeDtypeStruct((rows, d), F32)]
    if emit_next:
        out_shape.append(jax.ShapeDtypeStruct((rows, d), BF16))
    outs = pl.pallas_call(
        _residual_norm_kernel, grid=(rows // tm,),
        in_specs=[row_spec, row_spec, gain_spec, gain_spec],
        out_specs=[row_spec] * len(out_shape),
        out_shape=out_shape,
        compiler_params=_params(blk, temp_bytes=2 * _nbytes((tm, d), F32), n_axes=1),
        name="residual_norm",
    )(x, m, g_post.reshape(1, d), g_next.reshape(1, d))
    return (outs[0], outs[1]) if emit_next else (outs[0], None)


def _matmul_kernel(a_ref, w_ref, o_ref):
    o_ref[...] = _dot(a_ref[...], w_ref[...].astype(BF16)).astype(o_ref.dtype)


def _matmul(a, w, layer, *, out_dtype, tm, tn, col_block_offset=0, n_out=None, name="matmul"):
    m, k = a.shape
    n_out = w.shape[2] if n_out is None else n_out
    blk = _nbytes((tm, k), a.dtype) + _nbytes((k, tn), w.dtype) + _nbytes((tm, tn), out_dtype)
    blk += _nbytes((k, tn), BF16) // 2 if w.dtype != BF16 else 0
    return pl.pallas_call(
        _matmul_kernel, grid=(m // tm, n_out // tn),
        in_specs=[pl.BlockSpec((tm, k), lambda i, j: (i, 0)),
                  pl.BlockSpec((None, k, tn), lambda i, j: (layer, 0, j + col_block_offset))],
        out_specs=pl.BlockSpec((tm, tn), lambda i, j: (i, j)),
        out_shape=jax.ShapeDtypeStruct((m, n_out), out_dtype),
        compiler_params=_params(blk, temp_bytes=2 * _nbytes((tm, tn), F32)),
        name=name,
    )(a, w)


def _in_proj_kernel(h_ref, w_ref, *refs, q_blocks):
    if len(refs) == 4:
        side_ref, qkv_ref, u_ref, side_bf16_ref = refs
        side_bf16_ref[...] = side_ref[...].astype(side_bf16_ref.dtype)
    else:
        qkv_ref, u_ref = refs
    y = _dot(h_ref[...], w_ref[...].astype(BF16))
    j = pl.program_id(1)
    factor = jnp.float32(1.0)
    for (first, last), sign in zip(q_blocks, Q_SIGNS):
        factor = jnp.where((j >= first) & (j < last), sign * SCORE_SCALE * LOG2_E, factor)
    qkv_ref[...] = (y * factor).astype(qkv_ref.dtype)
    u_ref[...] = y


def _in_proj(h, w_in, layer, *, qkv_width, tm, tn, also_round=None, slab=64):
    m, k = h.shape
    n = w_in.shape[2]
    n_cols = n // tn
    group_blocks = qkv_width // (3 * len(Q_SIGNS)) // tn
    q_blocks = tuple((3 * i * group_blocks, (3 * i + 1) * group_blocks) for i in range(len(Q_SIGNS)))
    qkv_blocks = qkv_width // tn
    u_blocks = (n - qkv_width) // tn
    blk = (_nbytes((tm, k), BF16) + _nbytes((k, tn), w_in.dtype) + _nbytes((k, tn), BF16) // 2
           + _nbytes((tm, tn), BF16) + _nbytes((tm, tn), F32))
    in_specs = [pl.BlockSpec((tm, k), lambda i, j: (i, 0)),
                pl.BlockSpec((None, k, tn), lambda i, j: (layer, 0, j))]
    out_specs = [pl.BlockSpec((tm, tn), lambda i, j: (i, jnp.minimum(j, qkv_blocks))),
                 pl.BlockSpec((tm, tn), lambda i, j: (
                     i, jnp.where(j < qkv_blocks, u_blocks, j - qkv_blocks)))]
    out_shape = [jax.ShapeDtypeStruct((m, qkv_width + tn), BF16),
                 jax.ShapeDtypeStruct((m, n - qkv_width + tn), F32)]
    operands = [h, w_in]
    if also_round is not None:
        rows, cols = also_round.shape
        n_slabs = rows // slab
        assert n_slabs * slab == rows and n_slabs <= (m // tm) * n_cols
        side_spec = pl.BlockSpec((slab, cols),
                                 lambda i, j: (jnp.minimum(i * n_cols + j, n_slabs - 1), 0))
        in_specs.append(side_spec)
        out_specs.append(side_spec)
        out_shape.append(jax.ShapeDtypeStruct((rows, cols), BF16))
        operands.append(also_round)
        blk += _nbytes((slab, cols), F32) + _nbytes((slab, cols), BF16)
    return pl.pallas_call(
        functools.partial(_in_proj_kernel, q_blocks=q_blocks), grid=(m // tm, n_cols),
        in_specs=in_specs, out_specs=out_specs, out_shape=out_shape,
        compiler_params=_params(blk, temp_bytes=2 * _nbytes((tm, tn), F32)),
        name="in_proj",
    )(*operands)


def _out_proj_kernel(ya_ref, yb_ref, yc_ref, yd_ref, g_ref, w_ref, o_ref, hm_ref):
    @pl.when(pl.program_id(1) == 0)
    def _():
        gw = ya_ref.shape[1]
        for i, y_ref in enumerate((ya_ref, yb_ref, yc_ref, yd_ref)):
            y = y_ref[...].astype(F32)
            cols = slice(i * gw, (i + 1) * gw)
            hm_ref[:, cols] = (y * _rms_scale(y) * g_ref[:, cols]).astype(hm_ref.dtype)

    o_ref[...] = _dot(hm_ref[...], w_ref[...]).astype(o_ref.dtype)


def _out_proj(ys, g, w_out, layer, *, tm=1024, tn=512):
    rows, gw = ys[0].shape
    d = gw * len(ys)
    n = w_out.shape[2]
    blk = (len(ys) * _nbytes((tm, gw), ys[0].dtype) + _nbytes((d, tn), w_out.dtype)
           + _nbytes((tm, tn), BRANCH_DTYPE))
    y_spec = pl.BlockSpec((tm, gw), lambda i, j: (i, 0))
    return pl.pallas_call(
        _out_proj_kernel, grid=(rows // tm, n // tn),
        in_specs=[y_spec] * len(ys) + [pl.BlockSpec((1, d), lambda i, j: (0, 0)),
                                       pl.BlockSpec((None, d, tn), lambda i, j: (layer, 0, j))],
        out_specs=pl.BlockSpec((tm, tn), lambda i, j: (i, j)),
        out_shape=jax.ShapeDtypeStruct((rows, n), BRANCH_DTYPE),
        scratch_shapes=[pltpu.VMEM((tm, d), BF16)],
        compiler_params=_params(blk, scratch_bytes=_nbytes((tm, d), BF16),
                                temp_bytes=2 * _nbytes((tm, tn), F32)),
        name="out_proj",
    )(*ys, g.reshape(1, d), w_out)


def _swiglu_up_kernel(h_ref, wg_ref, wu_ref, wd_ref, o_ref, wd_bf16_ref):
    h = h_ref[...]
    g = _dot(h, wg_ref[...].astype(BF16))
    u = _dot(h, wu_ref[...].astype(BF16))
    o_ref[...] = (g / (1.0 + jnp.exp(-g)) * u).astype(o_ref.dtype)
    wd_bf16_ref[...] = wd_ref[...].astype(wd_bf16_ref.dtype)


def _swiglu_up(h, wg, wu, wd, layer, *, tm=2048, tn=256):
    m, k = h.shape
    n = wg.shape[2]
    grid = (m // tm, n // tn)
    d_ff, d_out = wd.shape[1:]
    slab = d_ff // (grid[0] * grid[1])
    assert slab * grid[0] * grid[1] == d_ff and slab % BF16_SUBLANES == 0
    blk = (_nbytes((tm, k), BF16) // 2 + 2 * _nbytes((k, tn), wg.dtype) + _nbytes((k, tn), BF16)
           + _nbytes((tm, tn), BF16) + _nbytes((slab, d_out), F32) + _nbytes((slab, d_out), BF16))
    w_spec = pl.BlockSpec((None, k, tn), lambda i, j: (layer, 0, j))
    return pl.pallas_call(
        _swiglu_up_kernel, grid=grid,
        in_specs=[pl.BlockSpec((tm, k), lambda i, j: (i, 0), pipeline_mode=pl.Buffered(1)),
                  w_spec, w_spec,
                  pl.BlockSpec((None, slab, d_out), lambda i, j: (layer, i * grid[1] + j, 0))],
        out_specs=[pl.BlockSpec((tm, tn), lambda i, j: (i, j)),
                   pl.BlockSpec((slab, d_out), lambda i, j: (i * grid[1] + j, 0))],
        out_shape=[jax.ShapeDtypeStruct((m, n), BF16), jax.ShapeDtypeStruct((d_ff, d_out), BF16)],
        compiler_params=_params(blk, temp_bytes=3 * _nbytes((tm, tn), F32)),
        name="swiglu_up",
    )(h, wg, wu, wd)


def _causal_tile_mask():
    row = lax.broadcasted_iota(jnp.int32, (ATTN_TILE, ATTN_TILE), 0)
    col = lax.broadcasted_iota(jnp.int32, (ATTN_TILE, ATTN_TILE), 1)
    return row, col


def _with_ones(v):
    return jnp.concatenate([v, jnp.ones((v.shape[0], LANES), v.dtype)], axis=1)


def _moba_kernel(q_ref, k_ref, v_ref, o_ref):
    t = ATTN_TILE
    seq = k_ref.shape[0]
    n_blocks = seq // t
    rows_pad = BF16_SUBLANES
    q = q_ref[...]
    k = k_ref[...]

    kmean = jnp.sum(k.astype(F32).reshape(n_blocks, t, HEAD_DIM), axis=1) * (1.0 / t)
    kmean = jnp.concatenate([kmean, jnp.zeros((rows_pad - n_blocks, HEAD_DIM), F32)], axis=0)
    kmean_hi = kmean.astype(BF16)
    kmean_lo = (kmean - kmean_hi.astype(F32)).astype(BF16)
    gate = _dot_nt(kmean_hi, q) + _dot_nt(kmean_lo, q)

    blk_id = lax.broadcasted_iota(jnp.int32, gate.shape, 0)
    q_blk = lax.shift_right_logical(lax.broadcasted_iota(jnp.int32, gate.shape, 1),
                                    int(math.log2(t)))
    rank = jnp.zeros(gate.shape, jnp.int32)
    for i in range(n_blocks):
        gi = gate[i:i + 1, :]
        beats = jnp.where(gi > gate, 1, jnp.where((gi == gate) & (i < blk_id), 1, 0))
        rank = rank + jnp.where(i < q_blk, beats, 0)
    keep = (blk_id == q_blk) | ((blk_id < q_blk) & (rank < MOBA_TOPK))
    bias = jnp.where(keep, 0.0, NEG_INF).astype(BF16)

    lane_id = lax.broadcasted_iota(jnp.int32, (rows_pad, LANES), 1)
    eye = jnp.where(lane_id == lax.broadcasted_iota(jnp.int32, (rows_pad, LANES), 0), 1.0, 0.0)
    bias_cols = _dot_tn(bias, eye.astype(BF16)).astype(BF16)
    key_blk = lax.shift_right_logical(lax.broadcasted_iota(jnp.int32, (seq, LANES), 0),
                                      int(math.log2(t)))
    one_hot = jnp.where(key_blk == lax.broadcasted_iota(jnp.int32, (seq, LANES), 1), 1.0, 0.0)
    q_aug = jnp.concatenate([q, bias_cols], axis=1)
    k_aug = jnp.concatenate([k, one_hot.astype(BF16)], axis=1)

    row, col = _causal_tile_mask()
    causal = col <= row
    tiles = [slice(qi * t, (qi + 1) * t) for qi in range(n_blocks)]
    s_own = [jnp.where(causal, _dot_nt(q_aug[r], k_aug[r]), NEG_INF) for r in tiles]
    s_past = [_dot_nt(q_aug[r], k_aug[:r.start]) for r in tiles[1:]]
    m = [jnp.max(x, axis=1, keepdims=True) for x in s_own]
    m = m[:1] + [jnp.maximum(mo, jnp.max(x, axis=1, keepdims=True)) for mo, x in zip(m[1:], s_past)]
    p_own = [jnp.exp2(x - mx) for x, mx in zip(s_own, m)]
    p_past = [jnp.exp2(x - mx) for x, mx in zip(s_past, m[1:])]
    v_aug = _with_ones(v_ref[...])
    for qi, r in enumerate(tiles):
        pv = _dot(p_own[qi].astype(BF16), v_aug[r])
        if qi > 0:
            pv = pv + _dot(p_past[qi - 1].astype(BF16), v_aug[:r.start])
        o_ref[r, :] = (pv[:, :HEAD_DIM] / pv[:, HEAD_DIM:]).astype(o_ref.dtype)


def _log2_keep(neg_z):
    neg_abs = lax.bitcast_convert_type(
        lax.bitcast_convert_type(neg_z, jnp.uint32) | jnp.uint32(0x80000000), F32)
    return jnp.minimum(neg_z, 0.0) - jnp.log2(1.0 + jnp.exp2(neg_abs))


def _split_hi_lo(x):
    hi = lax.bitcast_convert_type(
        lax.bitcast_convert_type(x, jnp.uint32) & jnp.uint32(0xFFFF0000), F32)
    return jnp.concatenate([hi.astype(BF16), (x - hi).astype(BF16)], axis=1)


def _stickbreak_kernel(q_ref, k_ref, v_ref, o_ref, acc_ref):
    t = SB_BLOCK
    near = SB_NEAR_BLOCKS
    seq = k_ref.shape[0]
    n_blocks = seq // t
    q = q_ref[...]
    k = k_ref[...]
    row = lax.broadcasted_iota(jnp.int32, (t, t), 0)
    col = lax.broadcasted_iota(jnp.int32, (t, t), 1)
    later = jnp.where(row > col, 1.0, 0.0).astype(BF16)
    later2 = jnp.concatenate([later, later], axis=0)

    def tile(x, i):
        return x[i * t:(i + 1) * t]

    groups = [jnp.concatenate([_dot_nt(tile(q, i), tile(k, i - g)) for i in range(g, n_blocks)],
                              axis=0) for g in range(near)]
    starts = [0]
    for g in range(near):
        starts.append(starts[-1] + (n_blocks - g) * t)
    neg_z = jnp.concatenate(groups, axis=0)
    row_in_tile = lax.broadcasted_iota(jnp.int32, (seq, t), 0) & (t - 1)
    strict = lax.broadcasted_iota(jnp.int32, (seq, t), 1) < row_in_tile
    log_keep = _log2_keep(neg_z)
    log_keep = jnp.concatenate([jnp.where(strict, log_keep[:seq], 0.0), log_keep[seq:]], axis=0)
    after = _dot(_split_hi_lo(log_keep), later2)
    block_sum = after[:, 0:1] + log_keep[:, 0:1]
    log_w = log_keep - neg_z + after
    weights = [jnp.where(strict, jnp.exp2(log_w[:seq]), 0.0)]
    carry = block_sum[:seq]
    for g in range(1, near):
        rows = slice(starts[g], starts[g + 1])
        carry = carry[t:]
        weights.append(jnp.exp2(log_w[rows] + carry))
        carry = carry + block_sum[rows]

    for i in range(n_blocks):
        back = min(i, near - 1)
        parts = [tile(weights[g], i - g) for g in range(back, -1, -1)]
        a = parts[0] if back == 0 else jnp.concatenate(parts, axis=1)
        acc_ref[i * t:(i + 1) * t, :] = _dot(a.astype(BF16), v_ref[(i - back) * t:(i + 1) * t, :])

    far_needed = [jnp.max(tile(carry, i - (near - 1))) >= UNDERFLOW_LOG2
                  for i in range(near, n_blocks)]
    for i in range(near, n_blocks):
        @pl.when(far_needed[i - near])
        def _(i=i):
            n = i - (near - 1)
            neg_z = _dot_nt(tile(q, i), k[:n * t])
            log_keep = _log2_keep(neg_z)
            stacked = jnp.concatenate([log_keep[:, j * t:(j + 1) * t] for j in range(n)], axis=0)
            after = _dot(_split_hi_lo(stacked), later2)
            c = tile(carry, i - (near - 1))
            weights = [None] * n
            for j in range(n - 1, -1, -1):
                blk = slice(j * t, (j + 1) * t)
                weights[j] = jnp.exp2(log_keep[:, blk] - neg_z[:, blk] + after[blk] + c)
                c = c + after[blk][:, 0:1] + log_keep[:, j * t:j * t + 1]
            a = weights[0] if n == 1 else jnp.concatenate(weights, axis=1)
            acc_ref[i * t:(i + 1) * t, :] += _dot(a.astype(BF16), v_ref[:n * t, :])

    o_ref[...] = acc_ref[...].astype(o_ref.dtype)


def _dilated_log2_counts(seq):
    t = ATTN_TILE
    delta = np.arange(t)[:, None] - np.arange(seq)[None, :] + (seq - t)
    counts = np.zeros(delta.shape, np.float64)
    for window, dil in DILATED_PATTERNS:
        counts += (delta >= 0) & (delta % dil == 0) & (delta <= window)
    return np.where(counts > 0, np.log2(np.maximum(counts, 1.0)), NEG_INF).astype(np.float32)


def _dilated_kernel(q_ref, k_ref, v_ref, log2cnt_ref, o_ref):
    t = ATTN_TILE
    seq = k_ref.shape[0]
    n_blocks = seq // t
    q = q_ref[...]
    k = k_ref[...]
    widths = [(qi + 1) * t for qi in range(n_blocks)]
    s2 = [_dot_nt(q[w - t:w], k[:w]) + log2cnt_ref[:, seq - w:]
          for w in widths]
    m = [jnp.max(x, axis=1, keepdims=True) for x in s2]
    p = [jnp.exp2(x - mx) for x, mx in zip(s2, m)]
    v_aug = _with_ones(v_ref[...])
    for w, pw in zip(widths, p):
        pv = _dot(pw.astype(BF16), v_aug[:w])
        o_ref[w - t:w, :] = (pv[:, :HEAD_DIM] / pv[:, HEAD_DIM:]).astype(o_ref.dtype)


def _mixers_kernel(qa_ref, ka_ref, va_ref, qb_ref, kb_ref, vb_ref, qd_ref, kd_ref, vd_ref,
                   log2cnt_ref, oa_ref, ob_ref, od_ref, acc_ref):
    _moba_kernel(qa_ref, ka_ref, va_ref, oa_ref)
    _dilated_kernel(qd_ref, kd_ref, vd_ref, log2cnt_ref, od_ref)
    _stickbreak_kernel(qb_ref, kb_ref, vb_ref, ob_ref, acc_ref)


def _attention_mixers(proj, log2cnt, *, batch, seq, n_heads):
    head_blk = (seq, HEAD_DIM)
    in_specs = [pl.BlockSpec(head_blk, lambda b, h, off=i * n_heads: (b, off + h))
                for i in range(9)]
    in_specs.append(pl.BlockSpec(log2cnt.shape, lambda b, h: (0, 0)))
    out_spec = pl.BlockSpec(head_blk, lambda b, h: (b, h))
    out_sds = jax.ShapeDtypeStruct((batch * seq, n_heads * HEAD_DIM), BF16)
    blk = 12 * _nbytes(head_blk, BF16) + _nbytes(log2cnt.shape, log2cnt.dtype)
    return pl.pallas_call(
        _mixers_kernel, grid=(batch, n_heads),
        in_specs=in_specs, out_specs=[out_spec] * 3, out_shape=[out_sds] * 3,
        scratch_shapes=[pltpu.VMEM(head_blk, F32)],
        compiler_params=_params(blk, scratch_bytes=_nbytes(head_blk, F32),
                                temp_bytes=8 * _nbytes((2 * seq, ATTN_TILE), F32)),
        name="attention_mixers",
    )(*([proj] * 9), log2cnt)


def _pool_kernel(u_ref, w_ref, scale_ref, o_ref):
    seq = u_ref.shape[0]
    g = pl.program_id(1)
    t_idx = lax.broadcasted_iota(jnp.int32, (seq, 1), 0)
    for gi, window in enumerate(POOL_WINDOWS):
        @pl.when(g == gi)
        def _(window=window):
            u = u_ref[...]
            win_sum = u
            span = 1
            while span < window:
                win_sum = win_sum + jnp.where(t_idx >= span, pltpu.roll(win_sum, span, axis=0), 0.0)
                span *= 2
            count = jnp.minimum(t_idx + 1, window).astype(F32)
            d = win_sum / count - u
            o_ref[...] = (_dot(d.astype(BF16), w_ref[...]) * scale_ref[...]).astype(o_ref.dtype)


def _multiscale_pool(u, w_pool, pool_scale, layer, *, batch, seq):
    _, n_groups, gd, _ = w_pool.shape
    assert all(w & (w - 1) == 0 for w in POOL_WINDOWS) and n_groups == len(POOL_WINDOWS)
    blk = _nbytes((seq, gd), F32) + _nbytes((seq, gd), BF16) + _nbytes((gd, gd), BF16)
    return pl.pallas_call(
        _pool_kernel, grid=(batch, n_groups),
        in_specs=[pl.BlockSpec((seq, gd), lambda b, g: (b, g)),
                  pl.BlockSpec((None, None, gd, gd), lambda b, g: (layer, g, 0, 0)),
                  pl.BlockSpec((1, gd), lambda b, g: (0, g))],
        out_specs=pl.BlockSpec((seq, gd), lambda b, g: (b, g)),
        out_shape=jax.ShapeDtypeStruct((u.shape[0], n_groups * gd), BF16),
        compiler_params=_params(blk, temp_bytes=6 * _nbytes((seq, gd), F32)),
        name="multiscale_pool",
    )(u, w_pool, pool_scale.reshape(1, n_groups * gd))


def kernel(x, ln_mix_pre, w_in, w_pool, pool_scale, mix_out_norm, w_out, ln_mix_post,
           ln_ffn_pre, w_gate, w_up, w_down, ln_ffn_post):
    batch, seq, d_model = x.shape
    depth = w_in.shape[0]
    group = w_out.shape[1] // N_MIXERS
    n_heads = group // HEAD_DIM
    n_blocks = seq // MOBA_BLOCK
    assert seq % ATTN_TILE == 0 and ATTN_TILE == MOBA_BLOCK and ATTN_TILE & (ATTN_TILE - 1) == 0
    assert MOBA_TOPK < n_blocks <= BF16_SUBLANES
    assert seq % SB_BLOCK == 0 and SB_BLOCK & (SB_BLOCK - 1) == 0 and SB_BLOCK % LANES == 0
    assert w_in.shape[2] == 10 * group

    w_pool = w_pool.astype(BF16)
    log2cnt = jnp.asarray(_dilated_log2_counts(seq))

    x = x.reshape(batch * seq, d_model)
    h = _rmsnorm_cast(x, ln_mix_pre[0])
    for l in range(depth):
        if l == 0:
            qkv, u, w_out = _in_proj(h, w_in, l, qkv_width=9 * group, tm=1024, tn=group // 2,
                                     also_round=w_out.reshape(-1, d_model))
            w_out = w_out.reshape(depth, -1, d_model)
        else:
            qkv, u = _in_proj(h, w_in, l, qkv_width=9 * group, tm=1024, tn=group // 2)
        y_a, y_b, y_d = _attention_mixers(qkv, log2cnt, batch=batch, seq=seq, n_heads=n_heads)
        y_c = _multiscale_pool(u, w_pool, pool_scale[l], l, batch=batch, seq=seq)
        m = _out_proj((y_a, y_b, y_c, y_d), mix_out_norm[l], w_out, l)
        x, h = _residual_norm(x, m, ln_mix_post[l], ln_ffn_pre[l])
        act, w_down_bf16 = _swiglu_up(h, w_gate, w_up, w_down, l)
        f = _matmul(act, w_down_bf16[None], 0, out_dtype=BRANCH_DTYPE, tm=512, tn=512,
                    name="ffn_down")
        last = l + 1 == depth
        x, h = _residual_norm(x, f, ln_ffn_post[l], ln_mix_pre[0 if last else l + 1],
                              emit_next=not last)
    return x.reshape(batch, seq, d_model)
```

```python
import functools
import math

import numpy as np
import jax
import jax.numpy as jnp
from jax import lax
from jax.experimental import pallas as pl
from jax.experimental.pallas import tpu as pltpu

HEAD_DIM = 128
N_MIXERS = 4
MOBA_BLOCK = 256
MOBA_TOPK = 3
POOL_WINDOWS = (2, 4, 8, 16)
DILATED_PATTERNS = ((128, 1), (512, 4), (2048, 16))
RMS_EPS = 1e-6
NEG_INF = -1e30

ATTN_TILE = 256
SB_BLOCK = 128
SB_NEAR_BLOCKS = 3
BF16_SUBLANES = 16
LANES = 128
V7X_VMEM_BYTES = 64 * 1024 * 1024
VMEM_LIMIT_CAP = V7X_VMEM_BYTES - 8 * 1024 * 1024
LOG2_E = math.log2(math.e)
UNDERFLOW_LOG2 = -160.0
SCORE_SCALE = HEAD_DIM ** -0.5
Q_SIGNS = (1.0, -1.0, 1.0)

F32 = jnp.float32
BF16 = jnp.bfloat16
BRANCH_DTYPE = BF16


def _nbytes(shape, dtype):
    return int(np.prod(shape)) * jnp.dtype(dtype).itemsize


def _params(block_bytes, scratch_bytes=0, temp_bytes=0, n_axes=2):
    limit = 2 * block_bytes + scratch_bytes + temp_bytes
    limit = min(max(limit, 16 * 1024 * 1024), VMEM_LIMIT_CAP)
    return pltpu.CompilerParams(
        dimension_semantics=("arbitrary",) * n_axes, vmem_limit_bytes=int(limit))


def _dot(a, b):
    return jnp.dot(a, b, preferred_element_type=F32)


def _dot_nt(a, b):
    return lax.dot_general(a, b, (((1,), (1,)), ((), ())), preferred_element_type=F32)


def _dot_tn(a, b):
    return lax.dot_general(a, b, (((0,), (0,)), ((), ())), preferred_element_type=F32)


def _rms_scale(x):
    return lax.rsqrt(jnp.mean(x * x, axis=-1, keepdims=True) + RMS_EPS)


def _rmsnorm_cast_kernel(x_ref, g_ref, o_ref):
    x = x_ref[...]
    o_ref[...] = (x * _rms_scale(x) * g_ref[...]).astype(o_ref.dtype)


def _rmsnorm_cast(x, g, *, tm=512):
    m, d = x.shape
    blk = _nbytes((tm, d), F32) + _nbytes((tm, d), BF16)
    return pl.pallas_call(
        _rmsnorm_cast_kernel, grid=(m // tm,),
        in_specs=[pl.BlockSpec((tm, d), lambda i: (i, 0)),
                  pl.BlockSpec((1, d), lambda i: (0, 0))],
        out_specs=pl.BlockSpec((tm, d), lambda i: (i, 0)),
        out_shape=jax.ShapeDtypeStruct((m, d), BF16),
        compiler_params=_params(blk, temp_bytes=_nbytes((tm, d), F32), n_axes=1),
        name="rmsnorm_cast",
    )(x, g.reshape(1, d))


def _residual_norm_kernel(x_ref, m_ref, g_post_ref, g_next_ref, xo_ref, ho_ref=None):
    m = m_ref[...].astype(F32)
    x = x_ref[...] + m * _rms_scale(m) * g_post_ref[...]
    xo_ref[...] = x
    if ho_ref is not None:
        ho_ref[...] = (x * _rms_scale(x) * g_next_ref[...]).astype(ho_ref.dtype)


def _residual_norm(x, m, g_post, g_next, *, emit_next=True, tm=256):
    rows, d = x.shape
    blk = (2 * _nbytes((tm, d), F32) + _nbytes((tm, d), m.dtype)
           + (_nbytes((tm, d), BF16) if emit_next else 0))
    row_spec = pl.BlockSpec((tm, d), lambda i: (i, 0))
    gain_spec = pl.BlockSpec((1, d), lambda i: (0, 0))
    out_shape = [jax.ShapeDtypeStruct((rows, d), F32)]
    if emit_next:
        out_shape.append(jax.ShapeDtypeStruct((rows, d), BF16))
    outs = pl.pallas_call(
        _residual_norm_kernel, grid=(rows // tm,),
        in_specs=[row_spec, row_spec, gain_spec, gain_spec],
        out_specs=[row_spec] * len(out_shape),
        out_shape=out_shape,
        compiler_params=_params(blk, temp_bytes=2 * _nbytes((tm, d), F32), n_axes=1),
        name="residual_norm",
    )(x, m, g_post.reshape(1, d), g_next.reshape(1, d))
    return (outs[0], outs[1]) if emit_next else (outs[0], None)


def _matmul_kernel(a_ref, w_ref, o_ref):
    o_ref[...] = _dot(a_ref[...], w_ref[...].astype(BF16)).astype(o_ref.dtype)


def _matmul(a, w, layer, *, out_dtype, tm, tn, col_block_offset=0, n_out=None, name="matmul"):
    m, k = a.shape
    n_out = w.shape[2] if n_out is None else n_out
    blk = _nbytes((tm, k), a.dtype) + _nbytes((k, tn), w.dtype) + _nbytes((tm, tn), out_dtype)
    blk += _nbytes((k, tn), BF16) // 2 if w.dtype != BF16 else 0
    return pl.pallas_call(
        _matmul_kernel, grid=(m // tm, n_out // tn),
        in_specs=[pl.BlockSpec((tm, k), lambda i, j: (i, 0)),
                  pl.BlockSpec((None, k, tn), lambda i, j: (layer, 0, j + col_block_offset))],
        out_specs=pl.BlockSpec((tm, tn), lambda i, j: (i, j)),
        out_shape=jax.ShapeDtypeStruct((m, n_out), out_dtype),
        compiler_params=_params(blk, temp_bytes=2 * _nbytes((tm, tn), F32)),
        name=name,
    )(a, w)


def _in_proj_kernel(h_ref, w_ref, *refs, q_blocks):
    if len(refs) == 4:
        side_ref, qkv_ref, u_ref, side_bf16_ref = refs
        side_bf16_ref[...] = side_ref[...].astype(side_bf16_ref.dtype)
    else:
        qkv_ref, u_ref = refs
    y = _dot(h_ref[...], w_ref[...].astype(BF16))
    j = pl.program_id(1)
    factor = jnp.float32(1.0)
    for (first, last), sign in zip(q_blocks, Q_SIGNS):
        factor = jnp.where((j >= first) & (j < last), sign * SCORE_SCALE * LOG2_E, factor)
    qkv_ref[...] = (y * factor).astype(qkv_ref.dtype)
    u_ref[...] = y


def _in_proj(h, w_in, layer, *, qkv_width, tm, tn, also_round=None, slab=64):
    m, k = h.shape
    n = w_in.shape[2]
    n_cols = n // tn
    group_blocks = qkv_width // (3 * len(Q_SIGNS)) // tn
    q_blocks = tuple((3 * i * group_blocks, (3 * i + 1) * group_blocks) for i in range(len(Q_SIGNS)))
    qkv_blocks = qkv_width // tn
    u_blocks = (n - qkv_width) // tn
    blk = (_nbytes((tm, k), BF16) + _nbytes((k, tn), w_in.dtype) + _nbytes((k, tn), BF16) // 2
           + _nbytes((tm, tn), BF16) + _nbytes((tm, tn), F32))
    in_specs = [pl.BlockSpec((tm, k), lambda i, j: (i, 0)),
                pl.BlockSpec((None, k, tn), lambda i, j: (layer, 0, j))]
    out_specs = [pl.BlockSpec((tm, tn), lambda i, j: (i, jnp.minimum(j, qkv_blocks))),
                 pl.BlockSpec((tm, tn), lambda i, j: (
                     i, jnp.where(j < qkv_blocks, u_blocks, j - qkv_blocks)))]
    out_shape = [jax.ShapeDtypeStruct((m, qkv_width + tn), BF16),
                 jax.ShapeDtypeStruct((m, n - qkv_width + tn), F32)]
    operands = [h, w_in]
    if also_round is not None:
        rows, cols = also_round.shape
        n_slabs = rows // slab
        assert n_slabs * slab == rows and n_slabs <= (m // tm) * n_cols
        side_spec = pl.BlockSpec((slab, cols),
                                 lambda i, j: (jnp.minimum(i * n_cols + j, n_slabs - 1), 0))
        in_specs.append(side_spec)
        out_specs.append(side_spec)
        out_shape.append(jax.ShapeDtypeStruct((rows, cols), BF16))
        operands.append(also_round)
        blk += _nbytes((slab, cols), F32) + _nbytes((slab, cols), BF16)
    return pl.pallas_call(
        functools.partial(_in_proj_kernel, q_blocks=q_blocks), grid=(m // tm, n_cols),
        in_specs=in_specs, out_specs=out_specs, out_shape=out_shape,
        compiler_params=_params(blk, temp_bytes=2 * _nbytes((tm, tn), F32)),
        name="in_proj",
    )(*operands)


def _out_proj_kernel(ya_ref, yb_ref, yc_ref, yd_ref, g_ref, w_ref, o_ref, hm_ref):
    @pl.when(pl.program_id(1) == 0)
    def _():
        gw = ya_ref.shape[1]
        for i, y_ref in enumerate((ya_ref, yb_ref, yc_ref, yd_ref)):
            y = y_ref[...].astype(F32)
            cols = slice(i * gw, (i + 1) * gw)
            hm_ref[:, cols] = (y * _rms_scale(y) * g_ref[:, cols]).astype(hm_ref.dtype)

    o_ref[...] = _dot(hm_ref[...], w_ref[...]).astype(o_ref.dtype)


def _out_proj(ys, g, w_out, layer, *, tm=1024, tn=512):
    rows, gw = ys[0].shape
    d = gw * len(ys)
    n = w_out.shape[2]
    blk = (len(ys) * _nbytes((tm, gw), ys[0].dtype) + _nbytes((d, tn), w_out.dtype)
           + _nbytes((tm, tn), BRANCH_DTYPE))
    y_spec = pl.BlockSpec((tm, gw), lambda i, j: (i, 0))
    return pl.pallas_call(
        _out_proj_kernel, grid=(rows // tm, n // tn),
        in_specs=[y_spec] * len(ys) + [pl.BlockSpec((1, d), lambda i, j: (0, 0)),
                                       pl.BlockSpec((None, d, tn), lambda i, j: (layer, 0, j))],
        out_specs=pl.BlockSpec((tm, tn), lambda i, j: (i, j)),
        out_shape=jax.ShapeDtypeStruct((rows, n), BRANCH_DTYPE),
        scratch_shapes=[pltpu.VMEM((tm, d), BF16)],
        compiler_params=_params(blk, scratch_bytes=_nbytes((tm, d), BF16),
                                temp_bytes=2 * _nbytes((tm, tn), F32)),
        name="out_proj",
    )(*ys, g.reshape(1, d), w_out)


def _swiglu_up_kernel(h_ref, wg_ref, wu_ref, wd_ref, o_ref, wd_bf16_ref):
    h = h_ref[...]
    g = _dot(h, wg_ref[...].astype(BF16))
    u = _dot(h, wu_ref[...].astype(BF16))
    o_ref[...] = (g / (1.0 + jnp.exp(-g)) * u).astype(o_ref.dtype)
    wd_bf16_ref[...] = wd_ref[...].astype(wd_bf16_ref.dtype)


def _swiglu_up(h, wg, wu, wd, layer, *, tm=1024, tn=256):
    m, k = h.shape
    n = wg.shape[2]
    grid = (m // tm, n // tn)
    d_ff, d_out = wd.shape[1:]
    slab = d_ff // (grid[0] * grid[1])
    assert slab * grid[0] * grid[1] == d_ff and slab % BF16_SUBLANES == 0
    blk = (_nbytes((tm, k), BF16) + 2 * _nbytes((k, tn), wg.dtype) + _nbytes((k, tn), BF16)
           + _nbytes((tm, tn), BF16) + _nbytes((slab, d_out), F32) + _nbytes((slab, d_out), BF16))
    w_spec = pl.BlockSpec((None, k, tn), lambda i, j: (layer, 0, j))
    return pl.pallas_call(
        _swiglu_up_kernel, grid=grid,
        in_specs=[pl.BlockSpec((tm, k), lambda i, j: (i, 0)), w_spec, w_spec,
                  pl.BlockSpec((None, slab, d_out), lambda i, j: (layer, i * grid[1] + j, 0))],
        out_specs=[pl.BlockSpec((tm, tn), lambda i, j: (i, j)),
                   pl.BlockSpec((slab, d_out), lambda i, j: (i * grid[1] + j, 0))],
        out_shape=[jax.ShapeDtypeStruct((m, n), BF16), jax.ShapeDtypeStruct((d_ff, d_out), BF16)],
        compiler_params=_params(blk, temp_bytes=3 * _nbytes((tm, tn), F32)),
        name="swiglu_up",
    )(h, wg, wu, wd)


def _causal_tile_mask():
    row = lax.broadcasted_iota(jnp.int32, (ATTN_TILE, ATTN_TILE), 0)
    col = lax.broadcasted_iota(jnp.int32, (ATTN_TILE, ATTN_TILE), 1)
    return row, col


def _with_ones(v):
    return jnp.concatenate([v, jnp.ones((v.shape[0], LANES), v.dtype)], axis=1)


def _moba_kernel(q_ref, k_ref, v_ref, o_ref):
    t = ATTN_TILE
    seq = k_ref.shape[0]
    n_blocks = seq // t
    rows_pad = BF16_SUBLANES
    q = q_ref[...]
    k = k_ref[...]

    kmean = jnp.sum(k.astype(F32).reshape(n_blocks, t, HEAD_DIM), axis=1) * (1.0 / t)
    kmean = jnp.concatenate([kmean, jnp.zeros((rows_pad - n_blocks, HEAD_DIM), F32)], axis=0)
    kmean_hi = kmean.astype(BF16)
    kmean_lo = (kmean - kmean_hi.astype(F32)).astype(BF16)
    gate = _dot_nt(kmean_hi, q) + _dot_nt(kmean_lo, q)

    blk_id = lax.broadcasted_iota(jnp.int32, gate.shape, 0)
    q_blk = lax.shift_right_logical(lax.broadcasted_iota(jnp.int32, gate.shape, 1),
                                    int(math.log2(t)))
    rank = jnp.zeros(gate.shape, jnp.int32)
    for i in range(n_blocks):
        gi = gate[i:i + 1, :]
        beats = jnp.where(gi > gate, 1, jnp.where((gi == gate) & (i < blk_id), 1, 0))
        rank = rank + jnp.where(i < q_blk, beats, 0)
    keep = (blk_id == q_blk) | ((blk_id < q_blk) & (rank < MOBA_TOPK))
    bias = jnp.where(keep, 0.0, NEG_INF).astype(BF16)

    lane_id = lax.broadcasted_iota(jnp.int32, (rows_pad, LANES), 1)
    eye = jnp.where(lane_id == lax.broadcasted_iota(jnp.int32, (rows_pad, LANES), 0), 1.0, 0.0)
    bias_cols = _dot_tn(bias, eye.astype(BF16)).astype(BF16)
    key_blk = lax.shift_right_logical(lax.broadcasted_iota(jnp.int32, (seq, LANES), 0),
                                      int(math.log2(t)))
    one_hot = jnp.where(key_blk == lax.broadcasted_iota(jnp.int32, (seq, LANES), 1), 1.0, 0.0)
    q_aug = jnp.concatenate([q, bias_cols], axis=1)
    k_aug = jnp.concatenate([k, one_hot.astype(BF16)], axis=1)

    row, col = _causal_tile_mask()
    causal = col <= row
    tiles = [slice(qi * t, (qi + 1) * t) for qi in range(n_blocks)]
    s_own = [jnp.where(causal, _dot_nt(q_aug[r], k_aug[r]), NEG_INF) for r in tiles]
    s_past = [_dot_nt(q_aug[r], k_aug[:r.start]) for r in tiles[1:]]
    m = [jnp.max(x, axis=1, keepdims=True) for x in s_own]
    m = m[:1] + [jnp.maximum(mo, jnp.max(x, axis=1, keepdims=True)) for mo, x in zip(m[1:], s_past)]
    p_own = [jnp.exp2(x - mx) for x, mx in zip(s_own, m)]
    p_past = [jnp.exp2(x - mx) for x, mx in zip(s_past, m[1:])]
    v_aug = _with_ones(v_ref[...])
    for qi, r in enumerate(tiles):
        pv = _dot(p_own[qi].astype(BF16), v_aug[r])
        if qi > 0:
            pv = pv + _dot(p_past[qi - 1].astype(BF16), v_aug[:r.start])
        o_ref[r, :] = (pv[:, :HEAD_DIM] / pv[:, HEAD_DIM:]).astype(o_ref.dtype)


def _log2_keep(neg_z):
    return jnp.minimum(neg_z, 0.0) - jnp.log2(1.0 + jnp.exp2(-jnp.abs(neg_z)))


def _split_hi_lo(x):
    hi = x.astype(BF16)
    return jnp.concatenate([hi, (x - hi.astype(F32)).astype(BF16)], axis=1)


def _stickbreak_kernel(q_ref, k_ref, v_ref, o_ref, acc_ref):
    t = SB_BLOCK
    near = SB_NEAR_BLOCKS
    seq = k_ref.shape[0]
    n_blocks = seq // t
    q = q_ref[...]
    k = k_ref[...]
    row = lax.broadcasted_iota(jnp.int32, (t, t), 0)
    col = lax.broadcasted_iota(jnp.int32, (t, t), 1)
    later = jnp.where(row > col, 1.0, 0.0).astype(BF16)
    later2 = jnp.concatenate([later, later], axis=0)

    def tile(x, i):
        return x[i * t:(i + 1) * t]

    groups = [jnp.concatenate([_dot_nt(tile(q, i), tile(k, i - g)) for i in range(g, n_blocks)],
                              axis=0) for g in range(near)]
    starts = [0]
    for g in range(near):
        starts.append(starts[-1] + (n_blocks - g) * t)
    neg_z = jnp.concatenate(groups, axis=0)
    row_in_tile = lax.broadcasted_iota(jnp.int32, (seq, t), 0) & (t - 1)
    strict = lax.broadcasted_iota(jnp.int32, (seq, t), 1) < row_in_tile
    log_keep = _log2_keep(neg_z)
    log_keep = jnp.concatenate([jnp.where(strict, log_keep[:seq], 0.0), log_keep[seq:]], axis=0)
    after = _dot(_split_hi_lo(log_keep), later2)
    block_sum = after[:, 0:1] + log_keep[:, 0:1]
    log_w = log_keep - neg_z + after
    weights = [jnp.where(strict, jnp.exp2(log_w[:seq]), 0.0)]
    carry = block_sum[:seq]
    for g in range(1, near):
        rows = slice(starts[g], starts[g + 1])
        carry = carry[t:]
        weights.append(jnp.exp2(log_w[rows] + carry))
        carry = carry + block_sum[rows]

    for i in range(n_blocks):
        back = min(i, near - 1)
        parts = [tile(weights[g], i - g) for g in range(back, -1, -1)]
        a = parts[0] if back == 0 else jnp.concatenate(parts, axis=1)
        acc_ref[i * t:(i + 1) * t, :] = _dot(a.astype(BF16), v_ref[(i - back) * t:(i + 1) * t, :])

    far_needed = [jnp.max(tile(carry, i - (near - 1))) >= UNDERFLOW_LOG2
                  for i in range(near, n_blocks)]
    for i in range(near, n_blocks):
        @pl.when(far_needed[i - near])
        def _(i=i):
            n = i - (near - 1)
            neg_z = _dot_nt(tile(q, i), k[:n * t])
            log_keep = _log2_keep(neg_z)
            stacked = jnp.concatenate([log_keep[:, j * t:(j + 1) * t] for j in range(n)], axis=0)
            after = _dot(_split_hi_lo(stacked), later2)
            c = tile(carry, i - (near - 1))
            weights = [None] * n
            for j in range(n - 1, -1, -1):
                blk = slice(j * t, (j + 1) * t)
                weights[j] = jnp.exp2(log_keep[:, blk] - neg_z[:, blk] + after[blk] + c)
                c = c + after[blk][:, 0:1] + log_keep[:, j * t:j * t + 1]
            a = weights[0] if n == 1 else jnp.concatenate(weights, axis=1)
            acc_ref[i * t:(i + 1) * t, :] += _dot(a.astype(BF16), v_ref[:n * t, :])

    o_ref[...] = acc_ref[...].astype(o_ref.dtype)


def _dilated_log2_counts(seq):
    t = ATTN_TILE
    delta = np.arange(t)[:, None] - np.arange(seq)[None, :] + (seq - t)
    counts = np.zeros(delta.shape, np.float64)
    for window, dil in DILATED_PATTERNS:
        counts += (delta >= 0) & (delta % dil == 0) & (delta <= window)
    return np.where(counts > 0, np.log2(np.maximum(counts, 1.0)), NEG_INF).astype(np.float32)


def _dilated_kernel(q_ref, k_ref, v_ref, log2cnt_ref, o_ref):
    t = ATTN_TILE
    seq = k_ref.shape[0]
    n_blocks = seq // t
    q = q_ref[...]
    k = k_ref[...]
    widths = [(qi + 1) * t for qi in range(n_blocks)]
    s2 = [_dot_nt(q[w - t:w], k[:w]) + log2cnt_ref[:, seq - w:]
          for w in widths]
    m = [jnp.max(x, axis=1, keepdims=True) for x in s2]
    p = [jnp.exp2(x - mx) for x, mx in zip(s2, m)]
    v_aug = _with_ones(v_ref[...])
    for w, pw in zip(widths, p):
        pv = _dot(pw.astype(BF16), v_aug[:w])
        o_ref[w - t:w, :] = (pv[:, :HEAD_DIM] / pv[:, HEAD_DIM:]).astype(o_ref.dtype)


def _mixers_kernel(qa_ref, ka_ref, va_ref, qb_ref, kb_ref, vb_ref, qd_ref, kd_ref, vd_ref,
                   log2cnt_ref, oa_ref, ob_ref, od_ref, acc_ref):
    _moba_kernel(qa_ref, ka_ref, va_ref, oa_ref)
    _dilated_kernel(qd_ref, kd_ref, vd_ref, log2cnt_ref, od_ref)
    _stickbreak_kernel(qb_ref, kb_ref, vb_ref, ob_ref, acc_ref)


def _attention_mixers(proj, log2cnt, *, batch, seq, n_heads):
    head_blk = (seq, HEAD_DIM)
    in_specs = [pl.BlockSpec(head_blk, lambda b, h, off=i * n_heads: (b, off + h))
                for i in range(9)]
    in_specs.append(pl.BlockSpec(log2cnt.shape, lambda b, h: (0, 0)))
    out_spec = pl.BlockSpec(head_blk, lambda b, h: (b, h))
    out_sds = jax.ShapeDtypeStruct((batch * seq, n_heads * HEAD_DIM), BF16)
    blk = 12 * _nbytes(head_blk, BF16) + _nbytes(log2cnt.shape, log2cnt.dtype)
    return pl.pallas_call(
        _mixers_kernel, grid=(batch, n_heads),
        in_specs=in_specs, out_specs=[out_spec] * 3, out_shape=[out_sds] * 3,
        scratch_shapes=[pltpu.VMEM(head_blk, F32)],
        compiler_params=_params(blk, scratch_bytes=_nbytes(head_blk, F32),
                                temp_bytes=8 * _nbytes((2 * seq, ATTN_TILE), F32)),
        name="attention_mixers",
    )(*([proj] * 9), log2cnt)


def _pool_kernel(u_ref, w_ref, scale_ref, o_ref):
    seq = u_ref.shape[0]
    g = pl.program_id(1)
    t_idx = lax.broadcasted_iota(jnp.int32, (seq, 1), 0)
    for gi, window in enumerate(POOL_WINDOWS):
        @pl.when(g == gi)
        def _(window=window):
            u = u_ref[...]
            win_sum = u
            span = 1
            while span < window:
                win_sum = win_sum + jnp.where(t_idx >= span, pltpu.roll(win_sum, span, axis=0), 0.0)
                span *= 2
            count = jnp.minimum(t_idx + 1, window).astype(F32)
            d = win_sum / count - u
            o_ref[...] = (_dot(d.astype(BF16), w_ref[...]) * scale_ref[...]).astype(o_ref.dtype)


def _multiscale_pool(u, w_pool, pool_scale, layer, *, batch, seq):
    _, n_groups, gd, _ = w_pool.shape
    assert all(w & (w - 1) == 0 for w in POOL_WINDOWS) and n_groups == len(POOL_WINDOWS)
    blk = _nbytes((seq, gd), F32) + _nbytes((seq, gd), BF16) + _nbytes((gd, gd), BF16)
    return pl.pallas_call(
        _pool_kernel, grid=(batch, n_groups),
        in_specs=[pl.BlockSpec((seq, gd), lambda b, g: (b, g)),
                  pl.BlockSpec((None, None, gd, gd), lambda b, g: (layer, g, 0, 0)),
                  pl.BlockSpec((1, gd), lambda b, g: (0, g))],
        out_specs=pl.BlockSpec((seq, gd), lambda b, g: (b, g)),
        out_shape=jax.ShapeDtypeStruct((u.shape[0], n_groups * gd), BF16),
        compiler_params=_params(blk, temp_bytes=6 * _nbytes((seq, gd), F32)),
        name="multiscale_pool",
    )(u, w_pool, pool_scale.reshape(1, n_groups * gd))


def kernel(x, ln_mix_pre, w_in, w_pool, pool_scale, mix_out_norm, w_out, ln_mix_post,
           ln_ffn_pre, w_gate, w_up, w_down, ln_ffn_post):
    batch, seq, d_model = x.shape
    depth = w_in.shape[0]
    group = w_out.shape[1] // N_MIXERS
    n_heads = group // HEAD_DIM
    n_blocks = seq // MOBA_BLOCK
    assert seq % ATTN_TILE == 0 and ATTN_TILE == MOBA_BLOCK and ATTN_TILE & (ATTN_TILE - 1) == 0
    assert MOBA_TOPK < n_blocks <= BF16_SUBLANES
    assert seq % SB_BLOCK == 0 and SB_BLOCK & (SB_BLOCK - 1) == 0 and SB_BLOCK % LANES == 0
    assert w_in.shape[2] == 10 * group

    w_pool = w_pool.astype(BF16)
    log2cnt = jnp.asarray(_dilated_log2_counts(seq))

    x = x.reshape(batch * seq, d_model)
    h = _rmsnorm_cast(x, ln_mix_pre[0])
    for l in range(depth):
        if l == 0:
            qkv, u, w_out = _in_proj(h, w_in, l, qkv_width=9 * group, tm=1024, tn=group // 2,
                                     also_round=w_out.reshape(-1, d_model))
            w_out = w_out.reshape(depth, -1, d_model)
        else:
            qkv, u = _in_proj(h, w_in, l, qkv_width=9 * group, tm=1024, tn=group // 2)
        y_a, y_b, y_d = _attention_mixers(qkv, log2cnt, batch=batch, seq=seq, n_heads=n_heads)
        y_c = _multiscale_pool(u, w_pool, pool_scale[l], l, batch=batch, seq=seq)
        m = _out_proj((y_a, y_b, y_c, y_d), mix_out_norm[l], w_out, l)
        x, h = _residual_norm(x, m, ln_mix_post[l], ln_ffn_pre[l])
        act, w_down_bf16 = _swiglu_up(h, w_gate, w_up, w_down, l)
        f = _matmul(act, w_down_bf16[None], 0, out_dtype=BRANCH_DTYPE, tm=512, tn=512,
                    name="ffn_down")
        last = l + 1 == depth
        x, h = _residual_norm(x, f, ln_ffn_post[l], ln_mix_pre[0 if last else l + 1],
                              emit_next=not last)
    return x.reshape(batch, seq, d_model)
```

```python
import functools
import math

import numpy as np
import jax
import jax.numpy as jnp
from jax import lax
from jax.experimental import pallas as pl
from jax.experimental.pallas import tpu as pltpu

HEAD_DIM = 128
N_MIXERS = 4
MOBA_BLOCK = 256
MOBA_TOPK = 3
POOL_WINDOWS = (2, 4, 8, 16)
DILATED_PATTERNS = ((128, 1), (512, 4), (2048, 16))
RMS_EPS = 1e-6
NEG_INF = -1e30

ATTN_TILE = 256
SB_BLOCK = 128
SB_NEAR_BLOCKS = 3
BF16_SUBLANES = 16
LANES = 128
V7X_VMEM_BYTES = 64 * 1024 * 1024
VMEM_LIMIT_CAP = V7X_VMEM_BYTES - 8 * 1024 * 1024
LOG2_E = math.log2(math.e)
UNDERFLOW_LOG2 = -160.0
SCORE_SCALE = HEAD_DIM ** -0.5
Q_SIGNS = (1.0, -1.0, 1.0)

F32 = jnp.float32
BF16 = jnp.bfloat16
BRANCH_DTYPE = BF16


def _nbytes(shape, dtype):
    return int(np.prod(shape)) * jnp.dtype(dtype).itemsize


def _params(block_bytes, scratch_bytes=0, temp_bytes=0, n_axes=2):
    limit = 2 * block_bytes + scratch_bytes + temp_bytes
    limit = min(max(limit, 16 * 1024 * 1024), VMEM_LIMIT_CAP)
    return pltpu.CompilerParams(
        dimension_semantics=("arbitrary",) * n_axes, vmem_limit_bytes=int(limit))


def _dot(a, b):
    return jnp.dot(a, b, preferred_element_type=F32)


def _dot_nt(a, b):
    return lax.dot_general(a, b, (((1,), (1,)), ((), ())), preferred_element_type=F32)


def _dot_tn(a, b):
    return lax.dot_general(a, b, (((0,), (0,)), ((), ())), preferred_element_type=F32)


def _rms_scale(x):
    return lax.rsqrt(jnp.mean(x * x, axis=-1, keepdims=True) + RMS_EPS)


def _rmsnorm_cast_kernel(x_ref, g_ref, o_ref):
    x = x_ref[...]
    o_ref[...] = (x * _rms_scale(x) * g_ref[...]).astype(o_ref.dtype)


def _rmsnorm_cast(x, g, *, tm=512):
    m, d = x.shape
    blk = _nbytes((tm, d), F32) + _nbytes((tm, d), BF16)
    return pl.pallas_call(
        _rmsnorm_cast_kernel, grid=(m // tm,),
        in_specs=[pl.BlockSpec((tm, d), lambda i: (i, 0)),
                  pl.BlockSpec((1, d), lambda i: (0, 0))],
        out_specs=pl.BlockSpec((tm, d), lambda i: (i, 0)),
        out_shape=jax.ShapeDtypeStruct((m, d), BF16),
        compiler_params=_params(blk, temp_bytes=_nbytes((tm, d), F32), n_axes=1),
        name="rmsnorm_cast",
    )(x, g.reshape(1, d))


def _residual_norm_kernel(x_ref, m_ref, g_post_ref, g_next_ref, xo_ref, ho_ref=None):
    m = m_ref[...].astype(F32)
    x = x_ref[...] + m * _rms_scale(m) * g_post_ref[...]
    xo_ref[...] = x
    if ho_ref is not None:
        ho_ref[...] = (x * _rms_scale(x) * g_next_ref[...]).astype(ho_ref.dtype)


def _residual_norm(x, m, g_post, g_next, *, emit_next=True, tm=256):
    rows, d = x.shape
    blk = (2 * _nbytes((tm, d), F32) + _nbytes((tm, d), m.dtype)
           + (_nbytes((tm, d), BF16) if emit_next else 0))
    row_spec = pl.BlockSpec((tm, d), lambda i: (i, 0))
    gain_spec = pl.BlockSpec((1, d), lambda i: (0, 0))
    out_shape = [jax.ShapeDtypeStruct((rows, d), F32)]
    if emit_next:
        out_shape.append(jax.ShapeDtypeStruct((rows, d), BF16))
    outs = pl.pallas_call(
        _residual_norm_kernel, grid=(rows // tm,),
        in_specs=[row_spec, row_spec, gain_spec, gain_spec],
        out_specs=[row_spec] * len(out_shape),
        out_shape=out_shape,
        compiler_params=_params(blk, temp_bytes=2 * _nbytes((tm, d), F32), n_axes=1),
        name="residual_norm",
    )(x, m, g_post.reshape(1, d), g_next.reshape(1, d))
    return (outs[0], outs[1]) if emit_next else (outs[0], None)


def _matmul_kernel(a_ref, w_ref, o_ref):
    o_ref[...] = _dot(a_ref[...], w_ref[...]).astype(o_ref.dtype)


def _matmul(a, w, *, out_dtype, tm, tn, name):
    m, k = a.shape
    n = w.shape[1]
    blk = _nbytes((tm, k), a.dtype) + _nbytes((k, tn), w.dtype) + _nbytes((tm, tn), out_dtype)
    return pl.pallas_call(
        _matmul_kernel, grid=(m // tm, n // tn),
        in_specs=[pl.BlockSpec((tm, k), lambda i, j: (i, 0)),
                  pl.BlockSpec((k, tn), lambda i, j: (0, j))],
        out_specs=pl.BlockSpec((tm, tn), lambda i, j: (i, j)),
        out_shape=jax.ShapeDtypeStruct((m, n), out_dtype),
        compiler_params=_params(blk, temp_bytes=2 * _nbytes((tm, tn), F32)),
        name=name,
    )(a, w)


def _in_proj_kernel(h_ref, w_ref, *refs, q_blocks):
    if len(refs) == 4:
        side_ref, qkv_ref, u_ref, side_bf16_ref = refs
        side_bf16_ref[...] = side_ref[...].astype(side_bf16_ref.dtype)
    else:
        qkv_ref, u_ref = refs
    y = _dot(h_ref[...], w_ref[...].astype(BF16))
    j = pl.program_id(1)
    factor = jnp.float32(1.0)
    for (first, last), sign in zip(q_blocks, Q_SIGNS):
        factor = jnp.where((j >= first) & (j < last), sign * SCORE_SCALE * LOG2_E, factor)
    qkv_ref[...] = (y * factor).astype(qkv_ref.dtype)
    u_ref[...] = y


def _in_proj(h, w_in, layer, *, qkv_width, tm, tn, also_round=None, slab=64):
    m, k = h.shape
    n = w_in.shape[2]
    n_cols = n // tn
    group_blocks = qkv_width // (3 * len(Q_SIGNS)) // tn
    q_blocks = tuple((3 * i * group_blocks, (3 * i + 1) * group_blocks) for i in range(len(Q_SIGNS)))
    qkv_blocks = qkv_width // tn
    u_blocks = (n - qkv_width) // tn
    blk = (_nbytes((tm, k), BF16) + _nbytes((k, tn), w_in.dtype) + _nbytes((k, tn), BF16) // 2
           + _nbytes((tm, tn), BF16) + _nbytes((tm, tn), F32))
    in_specs = [pl.BlockSpec((tm, k), lambda i, j: (i, 0)),
                pl.BlockSpec((None, k, tn), lambda i, j: (layer, 0, j))]
    out_specs = [pl.BlockSpec((tm, tn), lambda i, j: (i, jnp.minimum(j, qkv_blocks))),
                 pl.BlockSpec((tm, tn), lambda i, j: (
                     i, jnp.where(j < qkv_blocks, u_blocks, j - qkv_blocks)))]
    out_shape = [jax.ShapeDtypeStruct((m, qkv_width + tn), BF16),
                 jax.ShapeDtypeStruct((m, n - qkv_width + tn), F32)]
    operands = [h, w_in]
    if also_round is not None:
        rows, cols = also_round.shape
        n_slabs = rows // slab
        assert n_slabs * slab == rows and n_slabs <= (m // tm) * n_cols
        side_spec = pl.BlockSpec((slab, cols),
                                 lambda i, j: (jnp.minimum(i * n_cols + j, n_slabs - 1), 0))
        in_specs.append(side_spec)
        out_specs.append(side_spec)
        out_shape.append(jax.ShapeDtypeStruct((rows, cols), BF16))
        operands.append(also_round)
        blk += _nbytes((slab, cols), F32) + _nbytes((slab, cols), BF16)
    return pl.pallas_call(
        functools.partial(_in_proj_kernel, q_blocks=q_blocks), grid=(m // tm, n_cols),
        in_specs=in_specs, out_specs=out_specs, out_shape=out_shape,
        compiler_params=_params(blk, temp_bytes=2 * _nbytes((tm, tn), F32)),
        name="in_proj",
    )(*operands)


def _out_proj_kernel(ya_ref, yb_ref, yc_ref, yd_ref, g_ref, w_ref, o_ref, hm_ref):
    @pl.when(pl.program_id(1) == 0)
    def _():
        gw = ya_ref.shape[1]
        for i, y_ref in enumerate((ya_ref, yb_ref, yc_ref, yd_ref)):
            y = y_ref[...].astype(F32)
            cols = slice(i * gw, (i + 1) * gw)
            hm_ref[:, cols] = (y * _rms_scale(y) * g_ref[:, cols]).astype(hm_ref.dtype)

    o_ref[...] = _dot(hm_ref[...], w_ref[...]).astype(o_ref.dtype)


def _out_proj(ys, g, w_out, layer, *, tm=1024, tn=512):
    rows, gw = ys[0].shape
    d = gw * len(ys)
    n = w_out.shape[2]
    blk = (len(ys) * _nbytes((tm, gw), ys[0].dtype) + _nbytes((d, tn), w_out.dtype)
           + _nbytes((tm, tn), BRANCH_DTYPE))
    y_spec = pl.BlockSpec((tm, gw), lambda i, j: (i, 0))
    return pl.pallas_call(
        _out_proj_kernel, grid=(rows // tm, n // tn),
        in_specs=[y_spec] * len(ys) + [pl.BlockSpec((1, d), lambda i, j: (0, 0)),
                                       pl.BlockSpec((None, d, tn), lambda i, j: (layer, 0, j))],
        out_specs=pl.BlockSpec((tm, tn), lambda i, j: (i, j)),
        out_shape=jax.ShapeDtypeStruct((rows, n), BRANCH_DTYPE),
        scratch_shapes=[pltpu.VMEM((tm, d), BF16)],
        compiler_params=_params(blk, scratch_bytes=_nbytes((tm, d), BF16),
                                temp_bytes=2 * _nbytes((tm, tn), F32)),
        name="out_proj",
    )(*ys, g.reshape(1, d), w_out)


def _swiglu_up_kernel(h_ref, wg_ref, wu_ref, wd_ref, o_ref, wd_bf16_ref):
    h = h_ref[...]
    g = _dot(h, wg_ref[...].astype(BF16))
    u = _dot(h, wu_ref[...].astype(BF16))
    o_ref[...] = (g / (1.0 + jnp.exp(-g)) * u).astype(o_ref.dtype)
    wd_bf16_ref[...] = wd_ref[...].astype(wd_bf16_ref.dtype)


def _swiglu_up(h, wg, wu, wd, layer, *, tm=1024, tn=256):
    m, k = h.shape
    n = wg.shape[2]
    grid = (m // tm, n // tn)
    d_ff, d_out = wd.shape[1:]
    slab = d_ff // (grid[0] * grid[1])
    assert slab * grid[0] * grid[1] == d_ff and slab % BF16_SUBLANES == 0
    blk = (_nbytes((tm, k), BF16) + 2 * _nbytes((k, tn), wg.dtype) + _nbytes((k, tn), BF16)
           + _nbytes((tm, tn), BF16) + _nbytes((slab, d_out), F32) + _nbytes((slab, d_out), BF16))
    w_spec = pl.BlockSpec((None, k, tn), lambda i, j: (layer, 0, j))
    return pl.pallas_call(
        _swiglu_up_kernel, grid=grid,
        in_specs=[pl.BlockSpec((tm, k), lambda i, j: (i, 0)), w_spec, w_spec,
                  pl.BlockSpec((None, slab, d_out), lambda i, j: (layer, i * grid[1] + j, 0))],
        out_specs=[pl.BlockSpec((tm, tn), lambda i, j: (i, j)),
                   pl.BlockSpec((slab, d_out), lambda i, j: (i * grid[1] + j, 0))],
        out_shape=[jax.ShapeDtypeStruct((m, n), BF16), jax.ShapeDtypeStruct((d_ff, d_out), BF16)],
        compiler_params=_params(blk, temp_bytes=3 * _nbytes((tm, tn), F32)),
        name="swiglu_up",
    )(h, wg, wu, wd)


def _causal_tile_mask():
    row = lax.broadcasted_iota(jnp.int32, (ATTN_TILE, ATTN_TILE), 0)
    col = lax.broadcasted_iota(jnp.int32, (ATTN_TILE, ATTN_TILE), 1)
    return row, col


def _with_ones(v):
    return jnp.concatenate([v, jnp.ones((v.shape[0], LANES), v.dtype)], axis=1)


def _moba_kernel(q_ref, k_ref, v_ref, o_ref):
    t = ATTN_TILE
    seq = k_ref.shape[0]
    n_blocks = seq // t
    rows_pad = BF16_SUBLANES
    q = q_ref[...]
    k = k_ref[...]

    kmean = jnp.sum(k.astype(F32).reshape(n_blocks, t, HEAD_DIM), axis=1) * (1.0 / t)
    kmean = jnp.concatenate([kmean, jnp.zeros((rows_pad - n_blocks, HEAD_DIM), F32)], axis=0)
    kmean_hi = kmean.astype(BF16)
    kmean_lo = (kmean - kmean_hi.astype(F32)).astype(BF16)
    gate = _dot_nt(kmean_hi, q) + _dot_nt(kmean_lo, q)

    blk_id = lax.broadcasted_iota(jnp.int32, gate.shape, 0)
    q_blk = lax.shift_right_logical(lax.broadcasted_iota(jnp.int32, gate.shape, 1),
                                    int(math.log2(t)))
    rank = jnp.zeros(gate.shape, jnp.int32)
    for i in range(n_blocks):
        gi = gate[i:i + 1, :]
        beats = jnp.where(gi > gate, 1, jnp.where((gi == gate) & (i < blk_id), 1, 0))
        rank = rank + jnp.where(i < q_blk, beats, 0)
    keep = (blk_id == q_blk) | ((blk_id < q_blk) & (rank < MOBA_TOPK))
    bias = jnp.where(keep, 0.0, NEG_INF).astype(BF16)

    lane_id = lax.broadcasted_iota(jnp.int32, (rows_pad, LANES), 1)
    eye = jnp.where(lane_id == lax.broadcasted_iota(jnp.int32, (rows_pad, LANES), 0), 1.0, 0.0)
    bias_cols = _dot_tn(bias, eye.astype(BF16)).astype(BF16)
    key_blk = lax.shift_right_logical(lax.broadcasted_iota(jnp.int32, (seq, LANES), 0),
                                      int(math.log2(t)))
    one_hot = jnp.where(key_blk == lax.broadcasted_iota(jnp.int32, (seq, LANES), 1), 1.0, 0.0)
    q_aug = jnp.concatenate([q, bias_cols], axis=1)
    k_aug = jnp.concatenate([k, one_hot.astype(BF16)], axis=1)

    row, col = _causal_tile_mask()
    causal = col <= row
    tiles = [slice(qi * t, (qi + 1) * t) for qi in range(n_blocks)]
    s_own = [jnp.where(causal, _dot_nt(q_aug[r], k_aug[r]), NEG_INF) for r in tiles]
    s_past = [_dot_nt(q_aug[r], k_aug[:r.start]) for r in tiles[1:]]
    m = [jnp.max(x, axis=1, keepdims=True) for x in s_own]
    m = m[:1] + [jnp.maximum(mo, jnp.max(x, axis=1, keepdims=True)) for mo, x in zip(m[1:], s_past)]
    p_own = [jnp.exp2(x - mx) for x, mx in zip(s_own, m)]
    p_past = [jnp.exp2(x - mx) for x, mx in zip(s_past, m[1:])]
    v_aug = _with_ones(v_ref[...])
    for qi, r in enumerate(tiles):
        pv = _dot(p_own[qi].astype(BF16), v_aug[r])
        if qi > 0:
            pv = pv + _dot(p_past[qi - 1].astype(BF16), v_aug[:r.start])
        o_ref[r, :] = (pv[:, :HEAD_DIM] / pv[:, HEAD_DIM:]).astype(o_ref.dtype)


def _log2_keep(neg_z):
    return jnp.minimum(neg_z, 0.0) - jnp.log2(1.0 + jnp.exp2(-jnp.abs(neg_z)))


def _split_hi_lo(x):
    hi = x.astype(BF16)
    return jnp.concatenate([hi, (x - hi.astype(F32)).astype(BF16)], axis=1)


def _stickbreak_kernel(q_ref, k_ref, v_ref, o_ref, acc_ref):
    t = SB_BLOCK
    near = SB_NEAR_BLOCKS
    seq = k_ref.shape[0]
    n_blocks = seq // t
    q = q_ref[...]
    k = k_ref[...]
    row = lax.broadcasted_iota(jnp.int32, (t, t), 0)
    col = lax.broadcasted_iota(jnp.int32, (t, t), 1)
    later = jnp.where(row > col, 1.0, 0.0).astype(BF16)
    later2 = jnp.concatenate([later, later], axis=0)

    def tile(x, i):
        return x[i * t:(i + 1) * t]

    groups = [jnp.concatenate([_dot_nt(tile(q, i), tile(k, i - g)) for i in range(g, n_blocks)],
                              axis=0) for g in range(near)]
    starts = [0]
    for g in range(near):
        starts.append(starts[-1] + (n_blocks - g) * t)
    neg_z = jnp.concatenate(groups, axis=0)
    row_in_tile = lax.broadcasted_iota(jnp.int32, (seq, t), 0) & (t - 1)
    strict = lax.broadcasted_iota(jnp.int32, (seq, t), 1) < row_in_tile
    log_keep = _log2_keep(neg_z)
    log_keep = jnp.concatenate([jnp.where(strict, log_keep[:seq], 0.0), log_keep[seq:]], axis=0)
    after = _dot(_split_hi_lo(log_keep), later2)
    block_sum = after[:, 0:1] + log_keep[:, 0:1]
    log_w = log_keep - neg_z + after
    weights = [jnp.where(strict, jnp.exp2(log_w[:seq]), 0.0)]
    carry = block_sum[:seq]
    for g in range(1, near):
        rows = slice(starts[g], starts[g + 1])
        carry = carry[t:]
        weights.append(jnp.exp2(log_w[rows] + carry))
        carry = carry + block_sum[rows]

    for i in range(n_blocks):
        back = min(i, near - 1)
        parts = [tile(weights[g], i - g) for g in range(back, -1, -1)]
        a = parts[0] if back == 0 else jnp.concatenate(parts, axis=1)
        acc_ref[i * t:(i + 1) * t, :] = _dot(a.astype(BF16), v_ref[(i - back) * t:(i + 1) * t, :])

    far_needed = [jnp.max(tile(carry, i - (near - 1))) >= UNDERFLOW_LOG2
                  for i in range(near, n_blocks)]
    for i in range(near, n_blocks):
        @pl.when(far_needed[i - near])
        def _(i=i):
            n = i - (near - 1)
            neg_z = _dot_nt(tile(q, i), k[:n * t])
            log_keep = _log2_keep(neg_z)
            stacked = jnp.concatenate([log_keep[:, j * t:(j + 1) * t] for j in range(n)], axis=0)
            after = _dot(_split_hi_lo(stacked), later2)
            c = tile(carry, i - (near - 1))
            weights = [None] * n
            for j in range(n - 1, -1, -1):
                blk = slice(j * t, (j + 1) * t)
                weights[j] = jnp.exp2(log_keep[:, blk] - neg_z[:, blk] + after[blk] + c)
                c = c + after[blk][:, 0:1] + log_keep[:, j * t:j * t + 1]
            a = weights[0] if n == 1 else jnp.concatenate(weights, axis=1)
            acc_ref[i * t:(i + 1) * t, :] += _dot(a.astype(BF16), v_ref[:n * t, :])

    o_ref[...] = acc_ref[...].astype(o_ref.dtype)


def _dilated_log2_counts(seq):
    t = ATTN_TILE
    delta = np.arange(t)[:, None] - np.arange(seq)[None, :] + (seq - t)
    counts = np.zeros(delta.shape, np.float64)
    for window, dil in DILATED_PATTERNS:
        counts += (delta >= 0) & (delta % dil == 0) & (delta <= window)
    return np.where(counts > 0, np.log2(np.maximum(counts, 1.0)), NEG_INF).astype(np.float32)


def _dilated_kernel(q_ref, k_ref, v_ref, log2cnt_ref, o_ref):
    t = ATTN_TILE
    seq = k_ref.shape[0]
    n_blocks = seq // t
    q = q_ref[...]
    k = k_ref[...]
    widths = [(qi + 1) * t for qi in range(n_blocks)]
    s2 = [_dot_nt(q[w - t:w], k[:w]) + log2cnt_ref[:, seq - w:]
          for w in widths]
    m = [jnp.max(x, axis=1, keepdims=True) for x in s2]
    p = [jnp.exp2(x - mx) for x, mx in zip(s2, m)]
    v_aug = _with_ones(v_ref[...])
    for w, pw in zip(widths, p):
        pv = _dot(pw.astype(BF16), v_aug[:w])
        o_ref[w - t:w, :] = (pv[:, :HEAD_DIM] / pv[:, HEAD_DIM:]).astype(o_ref.dtype)


def _mixers_kernel(qa_ref, ka_ref, va_ref, qb_ref, kb_ref, vb_ref, qd_ref, kd_ref, vd_ref,
                   log2cnt_ref, oa_ref, ob_ref, od_ref, acc_ref):
    _moba_kernel(qa_ref, ka_ref, va_ref, oa_ref)
    _dilated_kernel(qd_ref, kd_ref, vd_ref, log2cnt_ref, od_ref)
    _stickbreak_kernel(qb_ref, kb_ref, vb_ref, ob_ref, acc_ref)


def _attention_mixers(proj, log2cnt, *, batch, seq, n_heads):
    head_blk = (seq, HEAD_DIM)
    in_specs = [pl.BlockSpec(head_blk, lambda b, h, off=i * n_heads: (b, off + h))
                for i in range(9)]
    in_specs.append(pl.BlockSpec(log2cnt.shape, lambda b, h: (0, 0)))
    out_spec = pl.BlockSpec(head_blk, lambda b, h: (b, h))
    out_sds = jax.ShapeDtypeStruct((batch * seq, n_heads * HEAD_DIM), BF16)
    blk = 12 * _nbytes(head_blk, BF16) + _nbytes(log2cnt.shape, log2cnt.dtype)
    return pl.pallas_call(
        _mixers_kernel, grid=(batch, n_heads),
        in_specs=in_specs, out_specs=[out_spec] * 3, out_shape=[out_sds] * 3,
        scratch_shapes=[pltpu.VMEM(head_blk, F32)],
        compiler_params=_params(blk, scratch_bytes=_nbytes(head_blk, F32),
                                temp_bytes=8 * _nbytes((2 * seq, ATTN_TILE), F32)),
        name="attention_mixers",
    )(*([proj] * 9), log2cnt)


def _pool_kernel(u_ref, w_ref, scale_ref, o_ref):
    seq = u_ref.shape[0]
    g = pl.program_id(1)
    t_idx = lax.broadcasted_iota(jnp.int32, (seq, 1), 0)
    for gi, window in enumerate(POOL_WINDOWS):
        @pl.when(g == gi)
        def _(window=window):
            u = u_ref[...]
            win_sum = u
            span = 1
            while span < window:
                win_sum = win_sum + jnp.where(t_idx >= span, pltpu.roll(win_sum, span, axis=0), 0.0)
                span *= 2
            count = jnp.minimum(t_idx + 1, window).astype(F32)
            d = win_sum / count - u
            o_ref[...] = (_dot(d.astype(BF16), w_ref[...]) * scale_ref[...]).astype(o_ref.dtype)


def _multiscale_pool(u, w_pool, pool_scale, layer, *, batch, seq):
    _, n_groups, gd, _ = w_pool.shape
    assert all(w & (w - 1) == 0 for w in POOL_WINDOWS) and n_groups == len(POOL_WINDOWS)
    blk = _nbytes((seq, gd), F32) + _nbytes((seq, gd), BF16) + _nbytes((gd, gd), BF16)
    return pl.pallas_call(
        _pool_kernel, grid=(batch, n_groups),
        in_specs=[pl.BlockSpec((seq, gd), lambda b, g: (b, g)),
                  pl.BlockSpec((None, None, gd, gd), lambda b, g: (layer, g, 0, 0)),
                  pl.BlockSpec((1, gd), lambda b, g: (0, g))],
        out_specs=pl.BlockSpec((seq, gd), lambda b, g: (b, g)),
        out_shape=jax.ShapeDtypeStruct((u.shape[0], n_groups * gd), BF16),
        compiler_params=_params(blk, temp_bytes=6 * _nbytes((seq, gd), F32)),
        name="multiscale_pool",
    )(u, w_pool, pool_scale.reshape(1, n_groups * gd))


def kernel(x, ln_mix_pre, w_in, w_pool, pool_scale, mix_out_norm, w_out, ln_mix_post,
           ln_ffn_pre, w_gate, w_up, w_down, ln_ffn_post):
    batch, seq, d_model = x.shape
    depth = w_in.shape[0]
    group = w_out.shape[1] // N_MIXERS
    n_heads = group // HEAD_DIM
    n_blocks = seq // MOBA_BLOCK
    assert seq % ATTN_TILE == 0 and ATTN_TILE == MOBA_BLOCK and ATTN_TILE & (ATTN_TILE - 1) == 0
    assert MOBA_TOPK < n_blocks <= BF16_SUBLANES
    assert seq % SB_BLOCK == 0 and SB_BLOCK & (SB_BLOCK - 1) == 0 and SB_BLOCK % LANES == 0
    assert w_in.shape[2] == 10 * group

    w_pool = w_pool.astype(BF16)
    log2cnt = jnp.asarray(_dilated_log2_counts(seq))

    x = x.reshape(batch * seq, d_model)
    h = _rmsnorm_cast(x, ln_mix_pre[0])
    for l in range(depth):
        if l == 0:
            qkv, u, w_out = _in_proj(h, w_in, l, qkv_width=9 * group, tm=1024, tn=group // 2,
                                     also_round=w_out.reshape(-1, d_model))
            w_out = w_out.reshape(depth, -1, d_model)
        else:
            qkv, u = _in_proj(h, w_in, l, qkv_width=9 * group, tm=1024, tn=group // 2)
        y_a, y_b, y_d = _attention_mixers(qkv, log2cnt, batch=batch, seq=seq, n_heads=n_heads)
        y_c = _multiscale_pool(u, w_pool, pool_scale[l], l, batch=batch, seq=seq)
        m = _out_proj((y_a, y_b, y_c, y_d), mix_out_norm[l], w_out, l)
        x, h = _residual_norm(x, m, ln_mix_post[l], ln_ffn_pre[l])
        act, w_down_bf16 = _swiglu_up(h, w_gate, w_up, w_down, l)
        f = _matmul(act, w_down_bf16, out_dtype=BRANCH_DTYPE, tm=512, tn=512, name="ffn_down")
        last = l + 1 == depth
        x, h = _residual_norm(x, f, ln_ffn_post[l], ln_mix_pre[0 if last else l + 1],
                              emit_next=not last)
    return x.reshape(batch, seq, d_model)
```

```python
import functools
import math

import numpy as np
import jax
import jax.numpy as jnp
from jax import lax
from jax.experimental import pallas as pl
from jax.experimental.pallas import tpu as pltpu

HEAD_DIM = 128
N_MIXERS = 4
MOBA_BLOCK = 256
MOBA_TOPK = 3
POOL_WINDOWS = (2, 4, 8, 16)
DILATED_PATTERNS = ((128, 1), (512, 4), (2048, 16))
RMS_EPS = 1e-6
NEG_INF = -1e30

ATTN_TILE = 256
SB_BLOCK = 128
SB_NEAR_BLOCKS = 3
BF16_SUBLANES = 16
LANES = 128
V7X_VMEM_BYTES = 64 * 1024 * 1024
VMEM_LIMIT_CAP = V7X_VMEM_BYTES - 8 * 1024 * 1024
LOG2_E = math.log2(math.e)
UNDERFLOW_LOG2 = -160.0
SCORE_SCALE = HEAD_DIM ** -0.5
Q_SIGNS = (1.0, -1.0, 1.0)

F32 = jnp.float32
BF16 = jnp.bfloat16
BRANCH_DTYPE = BF16


def _nbytes(shape, dtype):
    return int(np.prod(shape)) * jnp.dtype(dtype).itemsize


def _params(block_bytes, scratch_bytes=0, temp_bytes=0, n_axes=2):
    limit = 2 * block_bytes + scratch_bytes + temp_bytes
    limit = min(max(limit, 16 * 1024 * 1024), VMEM_LIMIT_CAP)
    return pltpu.CompilerParams(
        dimension_semantics=("arbitrary",) * n_axes, vmem_limit_bytes=int(limit))


def _dot(a, b):
    return jnp.dot(a, b, preferred_element_type=F32)


def _dot_nt(a, b):
    return lax.dot_general(a, b, (((1,), (1,)), ((), ())), preferred_element_type=F32)


def _dot_tn(a, b):
    return lax.dot_general(a, b, (((0,), (0,)), ((), ())), preferred_element_type=F32)


def _rms_scale(x):
    return lax.rsqrt(jnp.mean(x * x, axis=-1, keepdims=True) + RMS_EPS)


def _rmsnorm_cast_kernel(x_ref, g_ref, o_ref):
    x = x_ref[...]
    o_ref[...] = (x * _rms_scale(x) * g_ref[...]).astype(o_ref.dtype)


def _rmsnorm_cast(x, g, *, tm=512):
    m, d = x.shape
    blk = _nbytes((tm, d), F32) + _nbytes((tm, d), BF16)
    return pl.pallas_call(
        _rmsnorm_cast_kernel, grid=(m // tm,),
        in_specs=[pl.BlockSpec((tm, d), lambda i: (i, 0)),
                  pl.BlockSpec((1, d), lambda i: (0, 0))],
        out_specs=pl.BlockSpec((tm, d), lambda i: (i, 0)),
        out_shape=jax.ShapeDtypeStruct((m, d), BF16),
        compiler_params=_params(blk, temp_bytes=_nbytes((tm, d), F32), n_axes=1),
        name="rmsnorm_cast",
    )(x, g.reshape(1, d))


def _branch_update(x, b_ref, g_ref):
    b = b_ref[...].astype(F32)
    return x + b * _rms_scale(b) * g_ref[...]


def _ffn_prenorm_kernel(x_ref, m_ref, g_mix_ref, g_pre_ref, ho_ref):
    x = _branch_update(x_ref[...], m_ref, g_mix_ref)
    ho_ref[...] = (x * _rms_scale(x) * g_pre_ref[...]).astype(ho_ref.dtype)


def _ffn_prenorm(x, m, g_mix, g_pre, *, tm=256):
    rows, d = x.shape
    blk = _nbytes((tm, d), F32) + _nbytes((tm, d), m.dtype) + _nbytes((tm, d), BF16)
    row_spec = pl.BlockSpec((tm, d), lambda i: (i, 0))
    gain_spec = pl.BlockSpec((1, d), lambda i: (0, 0))
    return pl.pallas_call(
        _ffn_prenorm_kernel, grid=(rows // tm,),
        in_specs=[row_spec, row_spec, gain_spec, gain_spec],
        out_specs=row_spec,
        out_shape=jax.ShapeDtypeStruct((rows, d), BF16),
        compiler_params=_params(blk, temp_bytes=2 * _nbytes((tm, d), F32), n_axes=1),
        name="ffn_prenorm",
    )(x, m, g_mix.reshape(1, d), g_pre.reshape(1, d))


def _layer_residual_kernel(x_ref, m_ref, f_ref, g_mix_ref, g_ffn_ref, g_next_ref, xo_ref,
                           ho_ref=None):
    x = _branch_update(x_ref[...], m_ref, g_mix_ref)
    x = _branch_update(x, f_ref, g_ffn_ref)
    xo_ref[...] = x
    if ho_ref is not None:
        ho_ref[...] = (x * _rms_scale(x) * g_next_ref[...]).astype(ho_ref.dtype)


def _layer_residual(x, m, f, g_mix, g_ffn, g_next, *, emit_next=True, tm=256):
    rows, d = x.shape
    blk = (2 * _nbytes((tm, d), F32) + _nbytes((tm, d), m.dtype) + _nbytes((tm, d), f.dtype)
           + (_nbytes((tm, d), BF16) if emit_next else 0))
    row_spec = pl.BlockSpec((tm, d), lambda i: (i, 0))
    gain_spec = pl.BlockSpec((1, d), lambda i: (0, 0))
    out_shape = [jax.ShapeDtypeStruct((rows, d), F32)]
    if emit_next:
        out_shape.append(jax.ShapeDtypeStruct((rows, d), BF16))
    outs = pl.pallas_call(
        _layer_residual_kernel, grid=(rows // tm,),
        in_specs=[row_spec] * 3 + [gain_spec] * 3,
        out_specs=[row_spec] * len(out_shape),
        out_shape=out_shape,
        compiler_params=_params(blk, temp_bytes=3 * _nbytes((tm, d), F32), n_axes=1),
        name="layer_residual",
    )(x, m, f, g_mix.reshape(1, d), g_ffn.reshape(1, d), g_next.reshape(1, d))
    return (outs[0], outs[1]) if emit_next else (outs[0], None)


def _matmul_kernel(a_ref, w_ref, o_ref):
    o_ref[...] = _dot(a_ref[...], w_ref[...]).astype(o_ref.dtype)


def _matmul(a, w, *, out_dtype, tm, tn, name):
    m, k = a.shape
    n = w.shape[1]
    blk = _nbytes((tm, k), a.dtype) + _nbytes((k, tn), w.dtype) + _nbytes((tm, tn), out_dtype)
    return pl.pallas_call(
        _matmul_kernel, grid=(m // tm, n // tn),
        in_specs=[pl.BlockSpec((tm, k), lambda i, j: (i, 0)),
                  pl.BlockSpec((k, tn), lambda i, j: (0, j))],
        out_specs=pl.BlockSpec((tm, tn), lambda i, j: (i, j)),
        out_shape=jax.ShapeDtypeStruct((m, n), out_dtype),
        compiler_params=_params(blk, temp_bytes=2 * _nbytes((tm, tn), F32)),
        name=name,
    )(a, w)


def _in_proj_kernel(h_ref, w_ref, *refs, q_blocks):
    if len(refs) == 4:
        side_ref, qkv_ref, u_ref, side_bf16_ref = refs
        side_bf16_ref[...] = side_ref[...].astype(side_bf16_ref.dtype)
    else:
        qkv_ref, u_ref = refs
    y = _dot(h_ref[...], w_ref[...].astype(BF16))
    j = pl.program_id(1)
    factor = jnp.float32(1.0)
    for (first, last), sign in zip(q_blocks, Q_SIGNS):
        factor = jnp.where((j >= first) & (j < last), sign * SCORE_SCALE * LOG2_E, factor)
    qkv_ref[...] = (y * factor).astype(qkv_ref.dtype)
    u_ref[...] = y


def _in_proj(h, w_in, layer, *, qkv_width, tm, tn, also_round=None, slab=64):
    m, k = h.shape
    n = w_in.shape[2]
    n_cols = n // tn
    group_blocks = qkv_width // (3 * len(Q_SIGNS)) // tn
    q_blocks = tuple((3 * i * group_blocks, (3 * i + 1) * group_blocks) for i in range(len(Q_SIGNS)))
    qkv_blocks = qkv_width // tn
    u_blocks = (n - qkv_width) // tn
    blk = (_nbytes((tm, k), BF16) + _nbytes((k, tn), w_in.dtype) + _nbytes((k, tn), BF16) // 2
           + _nbytes((tm, tn), BF16) + _nbytes((tm, tn), F32))
    in_specs = [pl.BlockSpec((tm, k), lambda i, j: (i, 0)),
                pl.BlockSpec((None, k, tn), lambda i, j: (layer, 0, j))]
    out_specs = [pl.BlockSpec((tm, tn), lambda i, j: (i, jnp.minimum(j, qkv_blocks))),
                 pl.BlockSpec((tm, tn), lambda i, j: (
                     i, jnp.where(j < qkv_blocks, u_blocks, j - qkv_blocks)))]
    out_shape = [jax.ShapeDtypeStruct((m, qkv_width + tn), BF16),
                 jax.ShapeDtypeStruct((m, n - qkv_width + tn), F32)]
    operands = [h, w_in]
    if also_round is not None:
        rows, cols = also_round.shape
        n_slabs = rows // slab
        assert n_slabs * slab == rows and n_slabs <= (m // tm) * n_cols
        side_spec = pl.BlockSpec((slab, cols),
                                 lambda i, j: (jnp.minimum(i * n_cols + j, n_slabs - 1), 0))
        in_specs.append(side_spec)
        out_specs.append(side_spec)
        out_shape.append(jax.ShapeDtypeStruct((rows, cols), BF16))
        operands.append(also_round)
        blk += _nbytes((slab, cols), F32) + _nbytes((slab, cols), BF16)
    return pl.pallas_call(
        functools.partial(_in_proj_kernel, q_blocks=q_blocks), grid=(m // tm, n_cols),
        in_specs=in_specs, out_specs=out_specs, out_shape=out_shape,
        compiler_params=_params(blk, temp_bytes=2 * _nbytes((tm, tn), F32)),
        name="in_proj",
    )(*operands)


def _out_proj_kernel(ya_ref, yb_ref, yc_ref, yd_ref, g_ref, w_ref, o_ref, hm_ref):
    @pl.when(pl.program_id(1) == 0)
    def _():
        gw = ya_ref.shape[1]
        for i, y_ref in enumerate((ya_ref, yb_ref, yc_ref, yd_ref)):
            y = y_ref[...].astype(F32)
            cols = slice(i * gw, (i + 1) * gw)
            hm_ref[:, cols] = (y * _rms_scale(y) * g_ref[:, cols]).astype(hm_ref.dtype)

    o_ref[...] = _dot(hm_ref[...], w_ref[...]).astype(o_ref.dtype)


def _out_proj(ys, g, w_out, layer, *, tm=1024, tn=512):
    rows, gw = ys[0].shape
    d = gw * len(ys)
    n = w_out.shape[2]
    blk = (len(ys) * _nbytes((tm, gw), ys[0].dtype) + _nbytes((d, tn), w_out.dtype)
           + _nbytes((tm, tn), BRANCH_DTYPE))
    y_spec = pl.BlockSpec((tm, gw), lambda i, j: (i, 0))
    return pl.pallas_call(
        _out_proj_kernel, grid=(rows // tm, n // tn),
        in_specs=[y_spec] * len(ys) + [pl.BlockSpec((1, d), lambda i, j: (0, 0)),
                                       pl.BlockSpec((None, d, tn), lambda i, j: (layer, 0, j))],
        out_specs=pl.BlockSpec((tm, tn), lambda i, j: (i, j)),
        out_shape=jax.ShapeDtypeStruct((rows, n), BRANCH_DTYPE),
        scratch_shapes=[pltpu.VMEM((tm, d), BF16)],
        compiler_params=_params(blk, scratch_bytes=_nbytes((tm, d), BF16),
                                temp_bytes=2 * _nbytes((tm, tn), F32)),
        name="out_proj",
    )(*ys, g.reshape(1, d), w_out)


def _swiglu_up_kernel(h_ref, wg_ref, wu_ref, wd_ref, o_ref, wd_bf16_ref):
    h = h_ref[...]
    g = _dot(h, wg_ref[...].astype(BF16))
    u = _dot(h, wu_ref[...].astype(BF16))
    o_ref[...] = (g / (1.0 + jnp.exp(-g)) * u).astype(o_ref.dtype)
    wd_bf16_ref[...] = wd_ref[...].astype(wd_bf16_ref.dtype)


def _swiglu_up(h, wg, wu, wd, layer, *, tm=1024, tn=256):
    m, k = h.shape
    n = wg.shape[2]
    grid = (m // tm, n // tn)
    d_ff, d_out = wd.shape[1:]
    slab = d_ff // (grid[0] * grid[1])
    assert slab * grid[0] * grid[1] == d_ff and slab % BF16_SUBLANES == 0
    blk = (_nbytes((tm, k), BF16) + 2 * _nbytes((k, tn), wg.dtype) + _nbytes((k, tn), BF16)
           + _nbytes((tm, tn), BF16) + _nbytes((slab, d_out), F32) + _nbytes((slab, d_out), BF16))
    w_spec = pl.BlockSpec((None, k, tn), lambda i, j: (layer, 0, j))
    return pl.pallas_call(
        _swiglu_up_kernel, grid=grid,
        in_specs=[pl.BlockSpec((tm, k), lambda i, j: (i, 0)), w_spec, w_spec,
                  pl.BlockSpec((None, slab, d_out), lambda i, j: (layer, i * grid[1] + j, 0))],
        out_specs=[pl.BlockSpec((tm, tn), lambda i, j: (i, j)),
                   pl.BlockSpec((slab, d_out), lambda i, j: (i * grid[1] + j, 0))],
        out_shape=[jax.ShapeDtypeStruct((m, n), BF16), jax.ShapeDtypeStruct((d_ff, d_out), BF16)],
        compiler_params=_params(blk, temp_bytes=3 * _nbytes((tm, tn), F32)),
        name="swiglu_up",
    )(h, wg, wu, wd)


def _causal_tile_mask():
    row = lax.broadcasted_iota(jnp.int32, (ATTN_TILE, ATTN_TILE), 0)
    col = lax.broadcasted_iota(jnp.int32, (ATTN_TILE, ATTN_TILE), 1)
    return row, col


def _with_ones(v):
    return jnp.concatenate([v, jnp.ones((v.shape[0], LANES), v.dtype)], axis=1)


def _moba_kernel(q_ref, k_ref, v_ref, o_ref):
    t = ATTN_TILE
    seq = k_ref.shape[0]
    n_blocks = seq // t
    rows_pad = BF16_SUBLANES
    q = q_ref[...]
    k = k_ref[...]

    kmean = jnp.sum(k.astype(F32).reshape(n_blocks, t, HEAD_DIM), axis=1) * (1.0 / t)
    kmean = jnp.concatenate([kmean, jnp.zeros((rows_pad - n_blocks, HEAD_DIM), F32)], axis=0)
    kmean_hi = kmean.astype(BF16)
    kmean_lo = (kmean - kmean_hi.astype(F32)).astype(BF16)
    gate = _dot_nt(kmean_hi, q) + _dot_nt(kmean_lo, q)

    blk_id = lax.broadcasted_iota(jnp.int32, gate.shape, 0)
    q_blk = lax.shift_right_logical(lax.broadcasted_iota(jnp.int32, gate.shape, 1),
                                    int(math.log2(t)))
    rank = jnp.zeros(gate.shape, jnp.int32)
    for i in range(n_blocks):
        gi = gate[i:i + 1, :]
        beats = jnp.where(gi > gate, 1, jnp.where((gi == gate) & (i < blk_id), 1, 0))
        rank = rank + jnp.where(i < q_blk, beats, 0)
    keep = (blk_id == q_blk) | ((blk_id < q_blk) & (rank < MOBA_TOPK))
    bias = jnp.where(keep, 0.0, NEG_INF).astype(BF16)

    lane_id = lax.broadcasted_iota(jnp.int32, (rows_pad, LANES), 1)
    eye = jnp.where(lane_id == lax.broadcasted_iota(jnp.int32, (rows_pad, LANES), 0), 1.0, 0.0)
    bias_cols = _dot_tn(bias, eye.astype(BF16)).astype(BF16)
    key_blk = lax.shift_right_logical(lax.broadcasted_iota(jnp.int32, (seq, LANES), 0),
                                      int(math.log2(t)))
    one_hot = jnp.where(key_blk == lax.broadcasted_iota(jnp.int32, (seq, LANES), 1), 1.0, 0.0)
    q_aug = jnp.concatenate([q, bias_cols], axis=1)
    k_aug = jnp.concatenate([k, one_hot.astype(BF16)], axis=1)

    row, col = _causal_tile_mask()
    causal = col <= row
    tiles = [slice(qi * t, (qi + 1) * t) for qi in range(n_blocks)]
    s_own = [jnp.where(causal, _dot_nt(q_aug[r], k_aug[r]), NEG_INF) for r in tiles]
    s_past = [_dot_nt(q_aug[r], k_aug[:r.start]) for r in tiles[1:]]
    m = [jnp.max(x, axis=1, keepdims=True) for x in s_own]
    m = m[:1] + [jnp.maximum(mo, jnp.max(x, axis=1, keepdims=True)) for mo, x in zip(m[1:], s_past)]
    p_own = [jnp.exp2(x - mx) for x, mx in zip(s_own, m)]
    p_past = [jnp.exp2(x - mx) for x, mx in zip(s_past, m[1:])]
    v_aug = _with_ones(v_ref[...])
    for qi, r in enumerate(tiles):
        pv = _dot(p_own[qi].astype(BF16), v_aug[r])
        if qi > 0:
            pv = pv + _dot(p_past[qi - 1].astype(BF16), v_aug[:r.start])
        o_ref[r, :] = (pv[:, :HEAD_DIM] / pv[:, HEAD_DIM:]).astype(o_ref.dtype)


def _log2_keep(neg_z):
    return jnp.minimum(neg_z, 0.0) - jnp.log2(1.0 + jnp.exp2(-jnp.abs(neg_z)))


def _split_hi_lo(x):
    hi = x.astype(BF16)
    return jnp.concatenate([hi, (x - hi.astype(F32)).astype(BF16)], axis=1)


def _stickbreak_kernel(q_ref, k_ref, v_ref, o_ref, acc_ref):
    t = SB_BLOCK
    near = SB_NEAR_BLOCKS
    seq = k_ref.shape[0]
    n_blocks = seq // t
    q = q_ref[...]
    k = k_ref[...]
    row = lax.broadcasted_iota(jnp.int32, (t, t), 0)
    col = lax.broadcasted_iota(jnp.int32, (t, t), 1)
    later = jnp.where(row > col, 1.0, 0.0).astype(BF16)
    later2 = jnp.concatenate([later, later], axis=0)

    def tile(x, i):
        return x[i * t:(i + 1) * t]

    groups = [jnp.concatenate([_dot_nt(tile(q, i), tile(k, i - g)) for i in range(g, n_blocks)],
                              axis=0) for g in range(near)]
    starts = [0]
    for g in range(near):
        starts.append(starts[-1] + (n_blocks - g) * t)
    neg_z = jnp.concatenate(groups, axis=0)
    row_in_tile = lax.broadcasted_iota(jnp.int32, (seq, t), 0) & (t - 1)
    strict = lax.broadcasted_iota(jnp.int32, (seq, t), 1) < row_in_tile
    log_keep = _log2_keep(neg_z)
    log_keep = jnp.concatenate([jnp.where(strict, log_keep[:seq], 0.0), log_keep[seq:]], axis=0)
    after = _dot(_split_hi_lo(log_keep), later2)
    block_sum = after[:, 0:1] + log_keep[:, 0:1]
    log_w = log_keep - neg_z + after
    weights = [jnp.where(strict, jnp.exp2(log_w[:seq]), 0.0)]
    carry = block_sum[:seq]
    for g in range(1, near):
        rows = slice(starts[g], starts[g + 1])
        carry = carry[t:]
        weights.append(jnp.exp2(log_w[rows] + carry))
        carry = carry + block_sum[rows]

    for i in range(n_blocks):
        back = min(i, near - 1)
        parts = [tile(weights[g], i - g) for g in range(back, -1, -1)]
        a = parts[0] if back == 0 else jnp.concatenate(parts, axis=1)
        acc_ref[i * t:(i + 1) * t, :] = _dot(a.astype(BF16), v_ref[(i - back) * t:(i + 1) * t, :])

    far_needed = [jnp.max(tile(carry, i - (near - 1))) >= UNDERFLOW_LOG2
                  for i in range(near, n_blocks)]
    for i in range(near, n_blocks):
        @pl.when(far_needed[i - near])
        def _(i=i):
            n = i - (near - 1)
            neg_z = _dot_nt(tile(q, i), k[:n * t])
            log_keep = _log2_keep(neg_z)
            stacked = jnp.concatenate([log_keep[:, j * t:(j + 1) * t] for j in range(n)], axis=0)
            after = _dot(_split_hi_lo(stacked), later2)
            c = tile(carry, i - (near - 1))
            weights = [None] * n
            for j in range(n - 1, -1, -1):
                blk = slice(j * t, (j + 1) * t)
                weights[j] = jnp.exp2(log_keep[:, blk] - neg_z[:, blk] + after[blk] + c)
                c = c + after[blk][:, 0:1] + log_keep[:, j * t:j * t + 1]
            a = weights[0] if n == 1 else jnp.concatenate(weights, axis=1)
            acc_ref[i * t:(i + 1) * t, :] += _dot(a.astype(BF16), v_ref[:n * t, :])

    o_ref[...] = acc_ref[...].astype(o_ref.dtype)


def _dilated_log2_counts(seq):
    t = ATTN_TILE
    delta = np.arange(t)[:, None] - np.arange(seq)[None, :] + (seq - t)
    counts = np.zeros(delta.shape, np.float64)
    for window, dil in DILATED_PATTERNS:
        counts += (delta >= 0) & (delta % dil == 0) & (delta <= window)
    return np.where(counts > 0, np.log2(np.maximum(counts, 1.0)), NEG_INF).astype(np.float32)


def _dilated_kernel(q_ref, k_ref, v_ref, log2cnt_ref, o_ref):
    t = ATTN_TILE
    seq = k_ref.shape[0]
    n_blocks = seq // t
    q = q_ref[...]
    k = k_ref[...]
    widths = [(qi + 1) * t for qi in range(n_blocks)]
    s2 = [_dot_nt(q[w - t:w], k[:w]) + log2cnt_ref[:, seq - w:]
          for w in widths]
    m = [jnp.max(x, axis=1, keepdims=True) for x in s2]
    p = [jnp.exp2(x - mx) for x, mx in zip(s2, m)]
    v_aug = _with_ones(v_ref[...])
    for w, pw in zip(widths, p):
        pv = _dot(pw.astype(BF16), v_aug[:w])
        o_ref[w - t:w, :] = (pv[:, :HEAD_DIM] / pv[:, HEAD_DIM:]).astype(o_ref.dtype)


def _mixers_kernel(qa_ref, ka_ref, va_ref, qb_ref, kb_ref, vb_ref, qd_ref, kd_ref, vd_ref,
                   log2cnt_ref, oa_ref, ob_ref, od_ref, acc_ref):
    _moba_kernel(qa_ref, ka_ref, va_ref, oa_ref)
    _dilated_kernel(qd_ref, kd_ref, vd_ref, log2cnt_ref, od_ref)
    _stickbreak_kernel(qb_ref, kb_ref, vb_ref, ob_ref, acc_ref)


def _attention_mixers(proj, log2cnt, *, batch, seq, n_heads):
    head_blk = (seq, HEAD_DIM)
    in_specs = [pl.BlockSpec(head_blk, lambda b, h, off=i * n_heads: (b, off + h))
                for i in range(9)]
    in_specs.append(pl.BlockSpec(log2cnt.shape, lambda b, h: (0, 0)))
    out_spec = pl.BlockSpec(head_blk, lambda b, h: (b, h))
    out_sds = jax.ShapeDtypeStruct((batch * seq, n_heads * HEAD_DIM), BF16)
    blk = 12 * _nbytes(head_blk, BF16) + _nbytes(log2cnt.shape, log2cnt.dtype)
    return pl.pallas_call(
        _mixers_kernel, grid=(batch, n_heads),
        in_specs=in_specs, out_specs=[out_spec] * 3, out_shape=[out_sds] * 3,
        scratch_shapes=[pltpu.VMEM(head_blk, F32)],
        compiler_params=_params(blk, scratch_bytes=_nbytes(head_blk, F32),
                                temp_bytes=8 * _nbytes((2 * seq, ATTN_TILE), F32)),
        name="attention_mixers",
    )(*([proj] * 9), log2cnt)


def _pool_kernel(u_ref, w_ref, scale_ref, o_ref):
    seq = u_ref.shape[0]
    g = pl.program_id(1)
    t_idx = lax.broadcasted_iota(jnp.int32, (seq, 1), 0)
    for gi, window in enumerate(POOL_WINDOWS):
        @pl.when(g == gi)
        def _(window=window):
            u = u_ref[...]
            win_sum = u
            span = 1
            while span < window:
                win_sum = win_sum + jnp.where(t_idx >= span, pltpu.roll(win_sum, span, axis=0), 0.0)
                span *= 2
            count = jnp.minimum(t_idx + 1, window).astype(F32)
            d = win_sum / count - u
            o_ref[...] = (_dot(d.astype(BF16), w_ref[...]) * scale_ref[...]).astype(o_ref.dtype)


def _multiscale_pool(u, w_pool, pool_scale, layer, *, batch, seq):
    _, n_groups, gd, _ = w_pool.shape
    assert all(w & (w - 1) == 0 for w in POOL_WINDOWS) and n_groups == len(POOL_WINDOWS)
    blk = _nbytes((seq, gd), F32) + _nbytes((seq, gd), BF16) + _nbytes((gd, gd), BF16)
    return pl.pallas_call(
        _pool_kernel, grid=(batch, n_groups),
        in_specs=[pl.BlockSpec((seq, gd), lambda b, g: (b, g)),
                  pl.BlockSpec((None, None, gd, gd), lambda b, g: (layer, g, 0, 0)),
                  pl.BlockSpec((1, gd), lambda b, g: (0, g))],
        out_specs=pl.BlockSpec((seq, gd), lambda b, g: (b, g)),
        out_shape=jax.ShapeDtypeStruct((u.shape[0], n_groups * gd), BF16),
        compiler_params=_params(blk, temp_bytes=6 * _nbytes((seq, gd), F32)),
        name="multiscale_pool",
    )(u, w_pool, pool_scale.reshape(1, n_groups * gd))


def kernel(x, ln_mix_pre, w_in, w_pool, pool_scale, mix_out_norm, w_out, ln_mix_post,
           ln_ffn_pre, w_gate, w_up, w_down, ln_ffn_post):
    batch, seq, d_model = x.shape
    depth = w_in.shape[0]
    group = w_out.shape[1] // N_MIXERS
    n_heads = group // HEAD_DIM
    n_blocks = seq // MOBA_BLOCK
    assert seq % ATTN_TILE == 0 and ATTN_TILE == MOBA_BLOCK and ATTN_TILE & (ATTN_TILE - 1) == 0
    assert MOBA_TOPK < n_blocks <= BF16_SUBLANES
    assert seq % SB_BLOCK == 0 and SB_BLOCK & (SB_BLOCK - 1) == 0 and SB_BLOCK % LANES == 0
    assert w_in.shape[2] == 10 * group

    w_pool = w_pool.astype(BF16)
    log2cnt = jnp.asarray(_dilated_log2_counts(seq))

    x = x.reshape(batch * seq, d_model)
    h = _rmsnorm_cast(x, ln_mix_pre[0])
    for l in range(depth):
        if l == 0:
            qkv, u, w_out = _in_proj(h, w_in, l, qkv_width=9 * group, tm=1024, tn=group // 2,
                                     also_round=w_out.reshape(-1, d_model))
            w_out = w_out.reshape(depth, -1, d_model)
        else:
            qkv, u = _in_proj(h, w_in, l, qkv_width=9 * group, tm=1024, tn=group // 2)
        y_a, y_b, y_d = _attention_mixers(qkv, log2cnt, batch=batch, seq=seq, n_heads=n_heads)
        y_c = _multiscale_pool(u, w_pool, pool_scale[l], l, batch=batch, seq=seq)
        m = _out_proj((y_a, y_b, y_c, y_d), mix_out_norm[l], w_out, l)
        h = _ffn_prenorm(x, m, ln_mix_post[l], ln_ffn_pre[l])
        act, w_down_bf16 = _swiglu_up(h, w_gate, w_up, w_down, l)
        f = _matmul(act, w_down_bf16, out_dtype=BRANCH_DTYPE, tm=512, tn=512, name="ffn_down")
        last = l + 1 == depth
        x, h = _layer_residual(x, m, f, ln_mix_post[l], ln_ffn_post[l],
                               ln_mix_pre[0 if last else l + 1], emit_next=not last)
    return x.reshape(batch, seq, d_model)
```

```python
import functools
import math

import numpy as np
import jax
import jax.numpy as jnp
from jax import lax
from jax.experimental import pallas as pl
from jax.experimental.pallas import tpu as pltpu

HEAD_DIM = 128
N_MIXERS = 4
MOBA_BLOCK = 256
MOBA_TOPK = 3
POOL_WINDOWS = (2, 4, 8, 16)
DILATED_PATTERNS = ((128, 1), (512, 4), (2048, 16))
RMS_EPS = 1e-6
NEG_INF = -1e30

ATTN_TILE = 256
SB_BLOCK = 128
SB_NEAR_BLOCKS = 3
BF16_SUBLANES = 16
LANES = 128
NORM_ROW_CHUNK = 128
V7X_VMEM_BYTES = 64 * 1024 * 1024
VMEM_LIMIT_CAP = V7X_VMEM_BYTES - 8 * 1024 * 1024
LOG2_E = math.log2(math.e)
UNDERFLOW_LOG2 = -160.0
SCORE_SCALE = HEAD_DIM ** -0.5
Q_SIGNS = (1.0, -1.0, 1.0)

F32 = jnp.float32
BF16 = jnp.bfloat16
BRANCH_DTYPE = BF16


def _nbytes(shape, dtype):
    return int(np.prod(shape)) * jnp.dtype(dtype).itemsize


def _params(block_bytes, scratch_bytes=0, temp_bytes=0, n_axes=2):
    limit = 2 * block_bytes + scratch_bytes + temp_bytes
    limit = min(max(limit, 16 * 1024 * 1024), VMEM_LIMIT_CAP)
    return pltpu.CompilerParams(
        dimension_semantics=("arbitrary",) * n_axes, vmem_limit_bytes=int(limit))


def _dot(a, b):
    return jnp.dot(a, b, preferred_element_type=F32)


def _dot_nt(a, b):
    return lax.dot_general(a, b, (((1,), (1,)), ((), ())), preferred_element_type=F32)


def _dot_tn(a, b):
    return lax.dot_general(a, b, (((0,), (0,)), ((), ())), preferred_element_type=F32)


def _rms_scale(x):
    return lax.rsqrt(jnp.mean(x * x, axis=-1, keepdims=True) + RMS_EPS)


def _rmsnorm_cast_kernel(x_ref, g_ref, o_ref):
    x = x_ref[...]
    o_ref[...] = (x * _rms_scale(x) * g_ref[...]).astype(o_ref.dtype)


def _rmsnorm_cast(x, g, *, tm=512):
    m, d = x.shape
    blk = _nbytes((tm, d), F32) + _nbytes((tm, d), BF16)
    return pl.pallas_call(
        _rmsnorm_cast_kernel, grid=(m // tm,),
        in_specs=[pl.BlockSpec((tm, d), lambda i: (i, 0)),
                  pl.BlockSpec((1, d), lambda i: (0, 0))],
        out_specs=pl.BlockSpec((tm, d), lambda i: (i, 0)),
        out_shape=jax.ShapeDtypeStruct((m, d), BF16),
        compiler_params=_params(blk, temp_bytes=_nbytes((tm, d), F32), n_axes=1),
        name="rmsnorm_cast",
    )(x, g.reshape(1, d))


def _branch_update(x, b_ref, g_ref):
    b = b_ref[...].astype(F32)
    return x + b * _rms_scale(b) * g_ref[...]


def _ffn_prenorm_kernel(x_ref, m_ref, g_mix_ref, g_pre_ref, ho_ref):
    for r0 in range(0, x_ref.shape[0], NORM_ROW_CHUNK):
        rows = pl.ds(r0, NORM_ROW_CHUNK)
        x = _branch_update(x_ref[rows, :], m_ref.at[rows, :], g_mix_ref)
        ho_ref[rows, :] = (x * _rms_scale(x) * g_pre_ref[...]).astype(ho_ref.dtype)


def _ffn_prenorm(x, m, g_mix, g_pre, *, tm=512):
    rows, d = x.shape
    blk = _nbytes((tm, d), F32) + _nbytes((tm, d), m.dtype) + _nbytes((tm, d), BF16)
    row_spec = pl.BlockSpec((tm, d), lambda i: (i, 0))
    gain_spec = pl.BlockSpec((1, d), lambda i: (0, 0))
    return pl.pallas_call(
        _ffn_prenorm_kernel, grid=(rows // tm,),
        in_specs=[row_spec, row_spec, gain_spec, gain_spec],
        out_specs=row_spec,
        out_shape=jax.ShapeDtypeStruct((rows, d), BF16),
        compiler_params=_params(blk, temp_bytes=6 * _nbytes((NORM_ROW_CHUNK, d), F32), n_axes=1),
        name="ffn_prenorm",
    )(x, m, g_mix.reshape(1, d), g_pre.reshape(1, d))


def _layer_residual_kernel(x_ref, m_ref, f_ref, g_mix_ref, g_ffn_ref, g_next_ref, xo_ref,
                           ho_ref=None):
    x = _branch_update(x_ref[...], m_ref, g_mix_ref)
    x = _branch_update(x, f_ref, g_ffn_ref)
    xo_ref[...] = x
    if ho_ref is not None:
        ho_ref[...] = (x * _rms_scale(x) * g_next_ref[...]).astype(ho_ref.dtype)


def _layer_residual(x, m, f, g_mix, g_ffn, g_next, *, emit_next=True, tm=256):
    rows, d = x.shape
    blk = (2 * _nbytes((tm, d), F32) + _nbytes((tm, d), m.dtype) + _nbytes((tm, d), f.dtype)
           + (_nbytes((tm, d), BF16) if emit_next else 0))
    row_spec = pl.BlockSpec((tm, d), lambda i: (i, 0))
    gain_spec = pl.BlockSpec((1, d), lambda i: (0, 0))
    out_shape = [jax.ShapeDtypeStruct((rows, d), F32)]
    if emit_next:
        out_shape.append(jax.ShapeDtypeStruct((rows, d), BF16))
    outs = pl.pallas_call(
        _layer_residual_kernel, grid=(rows // tm,),
        in_specs=[row_spec] * 3 + [gain_spec] * 3,
        out_specs=[row_spec] * len(out_shape),
        out_shape=out_shape,
        compiler_params=_params(blk, temp_bytes=3 * _nbytes((tm, d), F32), n_axes=1),
        name="layer_residual",
    )(x, m, f, g_mix.reshape(1, d), g_ffn.reshape(1, d), g_next.reshape(1, d))
    return (outs[0], outs[1]) if emit_next else (outs[0], None)


def _matmul_kernel(a_ref, w_ref, o_ref):
    o_ref[...] = _dot(a_ref[...], w_ref[...]).astype(o_ref.dtype)


def _matmul(a, w, *, out_dtype, tm, tn, name):
    m, k = a.shape
    n = w.shape[1]
    blk = _nbytes((tm, k), a.dtype) + _nbytes((k, tn), w.dtype) + _nbytes((tm, tn), out_dtype)
    return pl.pallas_call(
        _matmul_kernel, grid=(m // tm, n // tn),
        in_specs=[pl.BlockSpec((tm, k), lambda i, j: (i, 0)),
                  pl.BlockSpec((k, tn), lambda i, j: (0, j))],
        out_specs=pl.BlockSpec((tm, tn), lambda i, j: (i, j)),
        out_shape=jax.ShapeDtypeStruct((m, n), out_dtype),
        compiler_params=_params(blk, temp_bytes=2 * _nbytes((tm, tn), F32)),
        name=name,
    )(a, w)


def _in_proj_kernel(h_ref, w_ref, *refs, q_blocks):
    if len(refs) == 4:
        side_ref, qkv_ref, u_ref, side_bf16_ref = refs
        side_bf16_ref[...] = side_ref[...].astype(side_bf16_ref.dtype)
    else:
        qkv_ref, u_ref = refs
    y = _dot(h_ref[...], w_ref[...].astype(BF16))
    j = pl.program_id(1)
    factor = jnp.float32(1.0)
    for (first, last), sign in zip(q_blocks, Q_SIGNS):
        factor = jnp.where((j >= first) & (j < last), sign * SCORE_SCALE * LOG2_E, factor)
    qkv_ref[...] = (y * factor).astype(qkv_ref.dtype)
    u_ref[...] = y


def _in_proj(h, w_in, layer, *, qkv_width, tm, tn, also_round=None, slab=64):
    m, k = h.shape
    n = w_in.shape[2]
    n_cols = n // tn
    group_blocks = qkv_width // (3 * len(Q_SIGNS)) // tn
    q_blocks = tuple((3 * i * group_blocks, (3 * i + 1) * group_blocks) for i in range(len(Q_SIGNS)))
    qkv_blocks = qkv_width // tn
    u_blocks = (n - qkv_width) // tn
    blk = (_nbytes((tm, k), BF16) + _nbytes((k, tn), w_in.dtype) + _nbytes((k, tn), BF16) // 2
           + _nbytes((tm, tn), BF16) + _nbytes((tm, tn), F32))
    in_specs = [pl.BlockSpec((tm, k), lambda i, j: (i, 0)),
                pl.BlockSpec((None, k, tn), lambda i, j: (layer, 0, j))]
    out_specs = [pl.BlockSpec((tm, tn), lambda i, j: (i, jnp.minimum(j, qkv_blocks))),
                 pl.BlockSpec((tm, tn), lambda i, j: (
                     i, jnp.where(j < qkv_blocks, u_blocks, j - qkv_blocks)))]
    out_shape = [jax.ShapeDtypeStruct((m, qkv_width + tn), BF16),
                 jax.ShapeDtypeStruct((m, n - qkv_width + tn), F32)]
    operands = [h, w_in]
    if also_round is not None:
        rows, cols = also_round.shape
        n_slabs = rows // slab
        assert n_slabs * slab == rows and n_slabs <= (m // tm) * n_cols
        side_spec = pl.BlockSpec((slab, cols),
                                 lambda i, j: (jnp.minimum(i * n_cols + j, n_slabs - 1), 0))
        in_specs.append(side_spec)
        out_specs.append(side_spec)
        out_shape.append(jax.ShapeDtypeStruct((rows, cols), BF16))
        operands.append(also_round)
        blk += _nbytes((slab, cols), F32) + _nbytes((slab, cols), BF16)
    return pl.pallas_call(
        functools.partial(_in_proj_kernel, q_blocks=q_blocks), grid=(m // tm, n_cols),
        in_specs=in_specs, out_specs=out_specs, out_shape=out_shape,
        compiler_params=_params(blk, temp_bytes=2 * _nbytes((tm, tn), F32)),
        name="in_proj",
    )(*operands)


def _out_proj_kernel(ya_ref, yb_ref, yc_ref, yd_ref, g_ref, w_ref, o_ref, hm_ref):
    @pl.when(pl.program_id(1) == 0)
    def _():
        gw = ya_ref.shape[1]
        for i, y_ref in enumerate((ya_ref, yb_ref, yc_ref, yd_ref)):
            y = y_ref[...].astype(F32)
            cols = slice(i * gw, (i + 1) * gw)
            hm_ref[:, cols] = (y * _rms_scale(y) * g_ref[:, cols]).astype(hm_ref.dtype)

    o_ref[...] = _dot(hm_ref[...], w_ref[...]).astype(o_ref.dtype)


def _out_proj(ys, g, w_out, layer, *, tm=1024, tn=512):
    rows, gw = ys[0].shape
    d = gw * len(ys)
    n = w_out.shape[2]
    blk = (len(ys) * _nbytes((tm, gw), ys[0].dtype) + _nbytes((d, tn), w_out.dtype)
           + _nbytes((tm, tn), BRANCH_DTYPE))
    y_spec = pl.BlockSpec((tm, gw), lambda i, j: (i, 0))
    return pl.pallas_call(
        _out_proj_kernel, grid=(rows // tm, n // tn),
        in_specs=[y_spec] * len(ys) + [pl.BlockSpec((1, d), lambda i, j: (0, 0)),
                                       pl.BlockSpec((None, d, tn), lambda i, j: (layer, 0, j))],
        out_specs=pl.BlockSpec((tm, tn), lambda i, j: (i, j)),
        out_shape=jax.ShapeDtypeStruct((rows, n), BRANCH_DTYPE),
        scratch_shapes=[pltpu.VMEM((tm, d), BF16)],
        compiler_params=_params(blk, scratch_bytes=_nbytes((tm, d), BF16),
                                temp_bytes=2 * _nbytes((tm, tn), F32)),
        name="out_proj",
    )(*ys, g.reshape(1, d), w_out)


def _swiglu_up_kernel(h_ref, wg_ref, wu_ref, wd_ref, o_ref, wd_bf16_ref):
    h = h_ref[...]
    g = _dot(h, wg_ref[...].astype(BF16))
    u = _dot(h, wu_ref[...].astype(BF16))
    o_ref[...] = (g / (1.0 + jnp.exp(-g)) * u).astype(o_ref.dtype)
    wd_bf16_ref[...] = wd_ref[...].astype(wd_bf16_ref.dtype)


def _swiglu_up(h, wg, wu, wd, layer, *, tm=1024, tn=256):
    m, k = h.shape
    n = wg.shape[2]
    grid = (m // tm, n // tn)
    d_ff, d_out = wd.shape[1:]
    slab = d_ff // (grid[0] * grid[1])
    assert slab * grid[0] * grid[1] == d_ff and slab % BF16_SUBLANES == 0
    blk = (_nbytes((tm, k), BF16) + 2 * _nbytes((k, tn), wg.dtype) + _nbytes((k, tn), BF16)
           + _nbytes((tm, tn), BF16) + _nbytes((slab, d_out), F32) + _nbytes((slab, d_out), BF16))
    w_spec = pl.BlockSpec((None, k, tn), lambda i, j: (layer, 0, j))
    return pl.pallas_call(
        _swiglu_up_kernel, grid=grid,
        in_specs=[pl.BlockSpec((tm, k), lambda i, j: (i, 0)), w_spec, w_spec,
                  pl.BlockSpec((None, slab, d_out), lambda i, j: (layer, i * grid[1] + j, 0))],
        out_specs=[pl.BlockSpec((tm, tn), lambda i, j: (i, j)),
                   pl.BlockSpec((slab, d_out), lambda i, j: (i * grid[1] + j, 0))],
        out_shape=[jax.ShapeDtypeStruct((m, n), BF16), jax.ShapeDtypeStruct((d_ff, d_out), BF16)],
        compiler_params=_params(blk, temp_bytes=3 * _nbytes((tm, tn), F32)),
        name="swiglu_up",
    )(h, wg, wu, wd)


def _causal_tile_mask():
    row = lax.broadcasted_iota(jnp.int32, (ATTN_TILE, ATTN_TILE), 0)
    col = lax.broadcasted_iota(jnp.int32, (ATTN_TILE, ATTN_TILE), 1)
    return row, col


def _with_ones(v):
    return jnp.concatenate([v, jnp.ones((v.shape[0], LANES), v.dtype)], axis=1)


def _moba_kernel(q_ref, k_ref, v_ref, o_ref):
    t = ATTN_TILE
    seq = k_ref.shape[0]
    n_blocks = seq // t
    rows_pad = BF16_SUBLANES
    q = q_ref[...]
    k = k_ref[...]

    kmean = jnp.sum(k.astype(F32).reshape(n_blocks, t, HEAD_DIM), axis=1) * (1.0 / t)
    kmean = jnp.concatenate([kmean, jnp.zeros((rows_pad - n_blocks, HEAD_DIM), F32)], axis=0)
    kmean_hi = kmean.astype(BF16)
    kmean_lo = (kmean - kmean_hi.astype(F32)).astype(BF16)
    gate = _dot_nt(kmean_hi, q) + _dot_nt(kmean_lo, q)

    blk_id = lax.broadcasted_iota(jnp.int32, gate.shape, 0)
    q_blk = lax.shift_right_logical(lax.broadcasted_iota(jnp.int32, gate.shape, 1),
                                    int(math.log2(t)))
    rank = jnp.zeros(gate.shape, jnp.int32)
    for i in range(n_blocks):
        gi = gate[i:i + 1, :]
        beats = jnp.where(gi > gate, 1, jnp.where((gi == gate) & (i < blk_id), 1, 0))
        rank = rank + jnp.where(i < q_blk, beats, 0)
    keep = (blk_id == q_blk) | ((blk_id < q_blk) & (rank < MOBA_TOPK))
    bias = jnp.where(keep, 0.0, NEG_INF).astype(BF16)

    lane_id = lax.broadcasted_iota(jnp.int32, (rows_pad, LANES), 1)
    eye = jnp.where(lane_id == lax.broadcasted_iota(jnp.int32, (rows_pad, LANES), 0), 1.0, 0.0)
    bias_cols = _dot_tn(bias, eye.astype(BF16)).astype(BF16)
    key_blk = lax.shift_right_logical(lax.broadcasted_iota(jnp.int32, (seq, LANES), 0),
                                      int(math.log2(t)))
    one_hot = jnp.where(key_blk == lax.broadcasted_iota(jnp.int32, (seq, LANES), 1), 1.0, 0.0)
    q_aug = jnp.concatenate([q, bias_cols], axis=1)
    k_aug = jnp.concatenate([k, one_hot.astype(BF16)], axis=1)

    row, col = _causal_tile_mask()
    causal = col <= row
    tiles = [slice(qi * t, (qi + 1) * t) for qi in range(n_blocks)]
    s_own = [jnp.where(causal, _dot_nt(q_aug[r], k_aug[r]), NEG_INF) for r in tiles]
    s_past = [_dot_nt(q_aug[r], k_aug[:r.start]) for r in tiles[1:]]
    m = [jnp.max(x, axis=1, keepdims=True) for x in s_own]
    m = m[:1] + [jnp.maximum(mo, jnp.max(x, axis=1, keepdims=True)) for mo, x in zip(m[1:], s_past)]
    p_own = [jnp.exp2(x - mx) for x, mx in zip(s_own, m)]
    p_past = [jnp.exp2(x - mx) for x, mx in zip(s_past, m[1:])]
    v_aug = _with_ones(v_ref[...])
    for qi, r in enumerate(tiles):
        pv = _dot(p_own[qi].astype(BF16), v_aug[r])
        if qi > 0:
            pv = pv + _dot(p_past[qi - 1].astype(BF16), v_aug[:r.start])
        o_ref[r, :] = (pv[:, :HEAD_DIM] / pv[:, HEAD_DIM:]).astype(o_ref.dtype)


def _log2_keep(neg_z):
    return jnp.minimum(neg_z, 0.0) - jnp.log2(1.0 + jnp.exp2(-jnp.abs(neg_z)))


def _split_hi_lo(x):
    hi = x.astype(BF16)
    return jnp.concatenate([hi, (x - hi.astype(F32)).astype(BF16)], axis=1)


def _stickbreak_kernel(q_ref, k_ref, v_ref, o_ref, acc_ref):
    t = SB_BLOCK
    near = SB_NEAR_BLOCKS
    seq = k_ref.shape[0]
    n_blocks = seq // t
    q = q_ref[...]
    k = k_ref[...]
    row = lax.broadcasted_iota(jnp.int32, (t, t), 0)
    col = lax.broadcasted_iota(jnp.int32, (t, t), 1)
    later = jnp.where(row > col, 1.0, 0.0).astype(BF16)
    later2 = jnp.concatenate([later, later], axis=0)

    def tile(x, i):
        return x[i * t:(i + 1) * t]

    groups = [jnp.concatenate([_dot_nt(tile(q, i), tile(k, i - g)) for i in range(g, n_blocks)],
                              axis=0) for g in range(near)]
    starts = [0]
    for g in range(near):
        starts.append(starts[-1] + (n_blocks - g) * t)
    neg_z = jnp.concatenate(groups, axis=0)
    row_in_tile = lax.broadcasted_iota(jnp.int32, (seq, t), 0) & (t - 1)
    strict = lax.broadcasted_iota(jnp.int32, (seq, t), 1) < row_in_tile
    log_keep = _log2_keep(neg_z)
    log_keep = jnp.concatenate([jnp.where(strict, log_keep[:seq], 0.0), log_keep[seq:]], axis=0)
    after = _dot(_split_hi_lo(log_keep), later2)
    block_sum = after[:, 0:1] + log_keep[:, 0:1]
    log_w = log_keep - neg_z + after
    weights = [jnp.where(strict, jnp.exp2(log_w[:seq]), 0.0)]
    carry = block_sum[:seq]
    for g in range(1, near):
        rows = slice(starts[g], starts[g + 1])
        carry = carry[t:]
        weights.append(jnp.exp2(log_w[rows] + carry))
        carry = carry + block_sum[rows]

    for i in range(n_blocks):
        back = min(i, near - 1)
        parts = [tile(weights[g], i - g) for g in range(back, -1, -1)]
        a = parts[0] if back == 0 else jnp.concatenate(parts, axis=1)
        acc_ref[i * t:(i + 1) * t, :] = _dot(a.astype(BF16), v_ref[(i - back) * t:(i + 1) * t, :])

    far_needed = [jnp.max(tile(carry, i - (near - 1))) >= UNDERFLOW_LOG2
                  for i in range(near, n_blocks)]
    for i in range(near, n_blocks):
        @pl.when(far_needed[i - near])
        def _(i=i):
            n = i - (near - 1)
            neg_z = _dot_nt(tile(q, i), k[:n * t])
            log_keep = _log2_keep(neg_z)
            stacked = jnp.concatenate([log_keep[:, j * t:(j + 1) * t] for j in range(n)], axis=0)
            after = _dot(_split_hi_lo(stacked), later2)
            c = tile(carry, i - (near - 1))
            weights = [None] * n
            for j in range(n - 1, -1, -1):
                blk = slice(j * t, (j + 1) * t)
                weights[j] = jnp.exp2(log_keep[:, blk] - neg_z[:, blk] + after[blk] + c)
                c = c + after[blk][:, 0:1] + log_keep[:, j * t:j * t + 1]
            a = weights[0] if n == 1 else jnp.concatenate(weights, axis=1)
            acc_ref[i * t:(i + 1) * t, :] += _dot(a.astype(BF16), v_ref[:n * t, :])

    o_ref[...] = acc_ref[...].astype(o_ref.dtype)


def _dilated_log2_counts(seq):
    t = ATTN_TILE
    delta = np.arange(t)[:, None] - np.arange(seq)[None, :] + (seq - t)
    counts = np.zeros(delta.shape, np.float64)
    for window, dil in DILATED_PATTERNS:
        counts += (delta >= 0) & (delta % dil == 0) & (delta <= window)
    return np.where(counts > 0, np.log2(np.maximum(counts, 1.0)), NEG_INF).astype(np.float32)


def _dilated_kernel(q_ref, k_ref, v_ref, log2cnt_ref, o_ref):
    t = ATTN_TILE
    seq = k_ref.shape[0]
    n_blocks = seq // t
    q = q_ref[...]
    k = k_ref[...]
    widths = [(qi + 1) * t for qi in range(n_blocks)]
    s2 = [_dot_nt(q[w - t:w], k[:w]) + log2cnt_ref[:, seq - w:]
          for w in widths]
    m = [jnp.max(x, axis=1, keepdims=True) for x in s2]
    p = [jnp.exp2(x - mx) for x, mx in zip(s2, m)]
    v_aug = _with_ones(v_ref[...])
    for w, pw in zip(widths, p):
        pv = _dot(pw.astype(BF16), v_aug[:w])
        o_ref[w - t:w, :] = (pv[:, :HEAD_DIM] / pv[:, HEAD_DIM:]).astype(o_ref.dtype)


def _mixers_kernel(qa_ref, ka_ref, va_ref, qb_ref, kb_ref, vb_ref, qd_ref, kd_ref, vd_ref,
                   log2cnt_ref, oa_ref, ob_ref, od_ref, acc_ref):
    _moba_kernel(qa_ref, ka_ref, va_ref, oa_ref)
    _dilated_kernel(qd_ref, kd_ref, vd_ref, log2cnt_ref, od_ref)
    _stickbreak_kernel(qb_ref, kb_ref, vb_ref, ob_ref, acc_ref)


def _attention_mixers(proj, log2cnt, *, batch, seq, n_heads):
    head_blk = (seq, HEAD_DIM)
    in_specs = [pl.BlockSpec(head_blk, lambda b, h, off=i * n_heads: (b, off + h))
                for i in range(9)]
    in_specs.append(pl.BlockSpec(log2cnt.shape, lambda b, h: (0, 0)))
    out_spec = pl.BlockSpec(head_blk, lambda b, h: (b, h))
    out_sds = jax.ShapeDtypeStruct((batch * seq, n_heads * HEAD_DIM), BF16)
    blk = 12 * _nbytes(head_blk, BF16) + _nbytes(log2cnt.shape, log2cnt.dtype)
    return pl.pallas_call(
        _mixers_kernel, grid=(batch, n_heads),
        in_specs=in_specs, out_specs=[out_spec] * 3, out_shape=[out_sds] * 3,
        scratch_shapes=[pltpu.VMEM(head_blk, F32)],
        compiler_params=_params(blk, scratch_bytes=_nbytes(head_blk, F32),
                                temp_bytes=8 * _nbytes((2 * seq, ATTN_TILE), F32)),
        name="attention_mixers",
    )(*([proj] * 9), log2cnt)


def _pool_kernel(u_ref, w_ref, scale_ref, o_ref):
    seq = u_ref.shape[0]
    g = pl.program_id(1)
    t_idx = lax.broadcasted_iota(jnp.int32, (seq, 1), 0)
    for gi, window in enumerate(POOL_WINDOWS):
        @pl.when(g == gi)
        def _(window=window):
            u = u_ref[...]
            win_sum = u
            span = 1
            while span < window:
                win_sum = win_sum + jnp.where(t_idx >= span, pltpu.roll(win_sum, span, axis=0), 0.0)
                span *= 2
            count = jnp.minimum(t_idx + 1, window).astype(F32)
            d = win_sum / count - u
            o_ref[...] = (_dot(d.astype(BF16), w_ref[...]) * scale_ref[...]).astype(o_ref.dtype)


def _multiscale_pool(u, w_pool, pool_scale, layer, *, batch, seq):
    _, n_groups, gd, _ = w_pool.shape
    assert all(w & (w - 1) == 0 for w in POOL_WINDOWS) and n_groups == len(POOL_WINDOWS)
    blk = _nbytes((seq, gd), F32) + _nbytes((seq, gd), BF16) + _nbytes((gd, gd), BF16)
    return pl.pallas_call(
        _pool_kernel, grid=(batch, n_groups),
        in_specs=[pl.BlockSpec((seq, gd), lambda b, g: (b, g)),
                  pl.BlockSpec((None, None, gd, gd), lambda b, g: (layer, g, 0, 0)),
                  pl.BlockSpec((1, gd), lambda b, g: (0, g))],
        out_specs=pl.BlockSpec((seq, gd), lambda b, g: (b, g)),
        out_shape=jax.ShapeDtypeStruct((u.shape[0], n_groups * gd), BF16),
        compiler_params=_params(blk, temp_bytes=6 * _nbytes((seq, gd), F32)),
        name="multiscale_pool",
    )(u, w_pool, pool_scale.reshape(1, n_groups * gd))


def kernel(x, ln_mix_pre, w_in, w_pool, pool_scale, mix_out_norm, w_out, ln_mix_post,
           ln_ffn_pre, w_gate, w_up, w_down, ln_ffn_post):
    batch, seq, d_model = x.shape
    depth = w_in.shape[0]
    group = w_out.shape[1] // N_MIXERS
    n_heads = group // HEAD_DIM
    n_blocks = seq // MOBA_BLOCK
    assert seq % ATTN_TILE == 0 and ATTN_TILE == MOBA_BLOCK and ATTN_TILE & (ATTN_TILE - 1) == 0
    assert MOBA_TOPK < n_blocks <= BF16_SUBLANES
    assert seq % SB_BLOCK == 0 and SB_BLOCK & (SB_BLOCK - 1) == 0 and SB_BLOCK % LANES == 0
    assert w_in.shape[2] == 10 * group

    w_pool = w_pool.astype(BF16)
    log2cnt = jnp.asarray(_dilated_log2_counts(seq))

    x = x.reshape(batch * seq, d_model)
    h = _rmsnorm_cast(x, ln_mix_pre[0])
    for l in range(depth):
        if l == 0:
            qkv, u, w_out = _in_proj(h, w_in, l, qkv_width=9 * group, tm=1024, tn=group // 2,
                                     also_round=w_out.reshape(-1, d_model))
            w_out = w_out.reshape(depth, -1, d_model)
        else:
            qkv, u = _in_proj(h, w_in, l, qkv_width=9 * group, tm=1024, tn=group // 2)
        y_a, y_b, y_d = _attention_mixers(qkv, log2cnt, batch=batch, seq=seq, n_heads=n_heads)
        y_c = _multiscale_pool(u, w_pool, pool_scale[l], l, batch=batch, seq=seq)
        m = _out_proj((y_a, y_b, y_c, y_d), mix_out_norm[l], w_out, l)
        h = _ffn_prenorm(x, m, ln_mix_post[l], ln_ffn_pre[l])
        act, w_down_bf16 = _swiglu_up(h, w_gate, w_up, w_down, l)
        f = _matmul(act, w_down_bf16, out_dtype=BRANCH_DTYPE, tm=512, tn=512, name="ffn_down")
        last = l + 1 == depth
        x, h = _layer_residual(x, m, f, ln_mix_post[l], ln_ffn_post[l],
                               ln_mix_pre[0 if last else l + 1], emit_next=not last)
    return x.reshape(batch, seq, d_model)
```
